```python
import jax, jax.numpy as jnp
from jax import lax
import numpy as np

D_MODEL = 1024
BATCH = 16
SEQ = 4096
DEPTH = 2
DEC_BATCH = 2
DEC_SEQ = 8192
PAST_LEN = 128

EPS = 1e-6
GLA_HEADS = 4
GLA_DK = 64
GLA_DV = 128
GLA_LOWRANK = 16
GLA_GATE_NORM = 16.0
GLA_CHUNK = 64
LRU_WIDTH = 512
LRU_BLOCKS = 8
LRU_BLOCK = LRU_WIDTH // LRU_BLOCKS
CONV_WIDTH = 4
LRU_C = 8.0
MLA_HEADS = 4
MLA_Q_RANK = 256
MLA_KV_RANK = 128
MLA_NOPE = 128
MLA_ROPE = 64
MLA_V = 128
MLA_QK = MLA_NOPE + MLA_ROPE
ROPE_THETA = 10000.0
Q_BLOCK = 128
N_BRANCH = 3
D_FF = 2816
N_EXPERTS = 8
TOP_K = 2
EXPERT_FF = 1408
N_DENSE = (DEPTH + 1) // 2
N_MOE = DEPTH // 2
IN_SIZES = (GLA_HEADS * GLA_DK, GLA_HEADS * GLA_DK, GLA_HEADS * GLA_DV, GLA_HEADS * GLA_DV, 2 * GLA_LOWRANK, LRU_WIDTH, LRU_WIDTH, MLA_Q_RANK, MLA_KV_RANK, MLA_ROPE, N_BRANCH * D_MODEL)
IN_TOTAL = 2 * GLA_HEADS * GLA_DK + 2 * GLA_HEADS * GLA_DV + 2 * GLA_LOWRANK + 2 * LRU_WIDTH + MLA_Q_RANK + MLA_KV_RANK + MLA_ROPE + N_BRANCH * D_MODEL

kernel_name = 'hybrid_gla_rglru_mla_encoder'


def rmsnorm(x, g):
    xf = x.astype(jnp.float32)
    y = xf * lax.rsqrt(jnp.mean(xf * xf, axis=-1, keepdims=True) + EPS)
    return (y * g.astype(jnp.float32)).astype(x.dtype)


def split_cols(z, sizes):
    parts, start = [], 0
    for s in sizes:
        parts.append(z[..., start:start + s])
        start += s
    return parts


def rope_tables(length):
    pos = jnp.arange(length, dtype=jnp.float32)
    inv = ROPE_THETA ** (-jnp.arange(0, MLA_ROPE, 2, dtype=jnp.float32) / MLA_ROPE)
    ang = pos[:, None] * inv[None, :]
    return jnp.cos(ang), jnp.sin(ang)


def apply_rope(x, cos, sin):
    x_nope, x_rope = x[..., :MLA_NOPE], x[..., MLA_NOPE:]
    x1, x2 = x_rope[..., :MLA_ROPE // 2], x_rope[..., MLA_ROPE // 2:]
    c, s = cos[None, :, None, :], sin[None, :, None, :]
    return jnp.concatenate([x_nope, x1 * c - x2 * s, x2 * c + x1 * s], axis=-1)


def gla_one_direction(q, k, v, logg):
    B, L, H, DK = q.shape
    DV = v.shape[-1]
    C = GLA_CHUNK
    N = L // C

    def to_chunks(t):
        return t.reshape(B, N, C, H, t.shape[-1]).transpose(1, 0, 3, 2, 4)

    qc, kc, vc, gc = to_chunks(q), to_chunks(k), to_chunks(v), to_chunks(logg)
    b = jnp.cumsum(gc, axis=3)
    b_last = b[:, :, :, -1:, :]
    q_dec = qc * jnp.exp(b)
    k_inv = kc * jnp.exp(-b)
    k_end = kc * jnp.exp(b_last - b)
    chunk_decay = jnp.exp(b_last[:, :, :, 0, :])[..., None]
    mask = jnp.tril(jnp.ones((C, C), dtype=bool))
    scores = jnp.where(mask, jnp.einsum('nbhid,nbhjd->nbhij', q_dec, k_inv), 0.0)
    o_intra = jnp.einsum('nbhij,nbhjv->nbhiv', scores, vc)

    def step(state, inp):
        qd, ke, vv, dec = inp
        o = jnp.einsum('bhid,bhdv->bhiv', qd, state)
        state = dec * state + jnp.einsum('bhjd,bhjv->bhdv', ke, vv)
        return state, o

    s0 = jnp.zeros((B, H, DK, DV), jnp.float32)
    _, o_inter = lax.scan(step, s0, (q_dec, k_end, vc, chunk_decay))
    return (o_intra + o_inter).transpose(1, 0, 3, 2, 4).reshape(B, L, H, DV)


def gla_branch(q, k, v, og, dec_lr, w_dec, b_dec, norm_g, w_out):
    B, L, _ = q.shape
    qh = q.astype(jnp.float32).reshape(B, L, GLA_HEADS, GLA_DK) * (GLA_DK ** -0.5)
    kh = k.astype(jnp.float32).reshape(B, L, GLA_HEADS, GLA_DK)
    vh = v.astype(jnp.float32).reshape(B, L, GLA_HEADS, GLA_DV)
    lr = dec_lr.astype(jnp.float32).reshape(B, L, 2, GLA_LOWRANK)
    logg = jax.nn.log_sigmoid(jnp.einsum('blzr,zrk->blzk', lr, w_dec.astype(jnp.float32)) + b_dec.astype(jnp.float32)) / GLA_GATE_NORM
    logg = logg.reshape(B, L, 2, GLA_HEADS, GLA_DK)
    o_fwd = gla_one_direction(qh, kh, vh, logg[:, :, 0])
    rev = lambda t: jnp.flip(t, axis=1)
    o_bwd = rev(gla_one_direction(rev(qh), rev(kh), rev(vh), rev(logg[:, :, 1])))
    o = rmsnorm(o_fwd + o_bwd, norm_g).reshape(B, L, GLA_HEADS * GLA_DV)
    o = o * jax.nn.silu(og.astype(jnp.float32))
    return o.astype(q.dtype) @ w_out


def linear_scan(a, u, reverse):
    def combine(c1, c2):
        a1, u1 = c1
        a2, u2 = c2
        return a1 * a2, a2 * u1 + u2
    return lax.associative_scan(combine, (a, u), reverse=reverse, axis=1)[1]


def rglru_branch(xb, gate, conv_w, conv_b, w_a, b_a, w_i, b_i, lam, w_out):
    B, L, W = xb.shape
    pad_left = CONV_WIDTH // 2
    u = lax.conv_general_dilated(xb, conv_w.astype(xb.dtype)[:, None, :], window_strides=(1,), padding=[(pad_left, CONV_WIDTH - 1 - pad_left)], dimension_numbers=('NWC', 'WIO', 'NWC'), feature_group_count=W)
    uf = u.astype(jnp.float32) + conv_b.astype(jnp.float32)
    ub = uf.reshape(B, L, LRU_BLOCKS, LRU_BLOCK)
    r = jax.nn.sigmoid(jnp.einsum('blni,znij->blznj', ub, w_a.astype(jnp.float32)).reshape(B, L, 2, W) + b_a.astype(jnp.float32))
    i = jax.nn.sigmoid(jnp.einsum('blni,znij->blznj', ub, w_i.astype(jnp.float32)).reshape(B, L, 2, W) + b_i.astype(jnp.float32))
    log_a = -LRU_C * r * jax.nn.softplus(-lam.astype(jnp.float32))
    a = jnp.exp(log_a)
    xin = jnp.sqrt(-jnp.expm1(2.0 * log_a)) * i * uf[:, :, None, :]
    h = linear_scan(a[:, :, 0], xin[:, :, 0], False) + linear_scan(a[:, :, 1], xin[:, :, 1], True)
    y = h * jax.nn.gelu(gate.astype(jnp.float32))
    return y.astype(xb.dtype) @ w_out


def bidirectional_attention(q, k, v):
    B, L, H, Dh = q.shape
    scale = Dh ** -0.5
    qb = q.reshape(B, L // Q_BLOCK, Q_BLOCK, H, Dh).transpose(1, 0, 2, 3, 4)

    def one_block(q_blk):
        s = jnp.einsum('bqhd,bkhd->bhqk', q_blk, k) * scale
        p = jax.nn.softmax(s, axis=-1)
        return jnp.einsum('bhqk,bkhd->bqhd', p.astype(v.dtype), v)

    o = lax.map(one_block, qb)
    return o.transpose(1, 0, 2, 3, 4).reshape(B, L, H, v.shape[-1])


def mla_branch(qa, kva, k_rope, qa_g, w_qb, kva_g, w_kvb, qn_g, kn_g, w_out, cos, sin):
    B, L, _ = qa.shape
    q = (rmsnorm(qa, qa_g) @ w_qb).reshape(B, L, MLA_HEADS, MLA_QK)
    kv = (rmsnorm(kva, kva_g) @ w_kvb).reshape(B, L, MLA_HEADS, MLA_NOPE + MLA_V)
    k_nope, v = kv[..., :MLA_NOPE], kv[..., MLA_NOPE:]
    k = jnp.concatenate([k_nope, jnp.broadcast_to(k_rope[:, :, None, :], (B, L, MLA_HEADS, MLA_ROPE))], axis=-1)
    q = apply_rope(rmsnorm(q, qn_g).astype(jnp.float32), cos, sin)
    k = apply_rope(rmsnorm(k, kn_g).astype(jnp.float32), cos, sin)
    o = bidirectional_attention(q, k, v)
    return o.reshape(B, L, MLA_HEADS * MLA_V) @ w_out


def token_mixer(h, w_in, gla_w_dec, gla_b_dec, gla_norm_g, gla_w_out, lru_conv_w, lru_conv_b, lru_w_a, lru_b_a, lru_w_i, lru_b_i, lru_lambda, lru_w_out, mla_qa_g, mla_w_qb, mla_kva_g, mla_w_kvb, mla_qn_g, mla_kn_g, mla_w_out, w_o, cos, sin):
    B, L, D = h.shape
    g_q, g_k, g_v, g_og, g_dec, r_x, r_gate, m_qa, m_kva, m_kr, gate_logits = split_cols(h @ w_in, IN_SIZES)
    y_a = gla_branch(g_q, g_k, g_v, g_og, g_dec, gla_w_dec, gla_b_dec, gla_norm_g, gla_w_out)
    y_b = rglru_branch(r_x, r_gate, lru_conv_w, lru_conv_b, lru_w_a, lru_b_a, lru_w_i, lru_b_i, lru_lambda, lru_w_out)
    y_c = mla_branch(m_qa, m_kva, m_kr, mla_qa_g, mla_w_qb, mla_kva_g, mla_w_kvb, mla_qn_g, mla_kn_g, mla_w_out, cos, sin)
    gates = jax.nn.sigmoid(gate_logits.astype(jnp.float32)).reshape(B, L, N_BRANCH, D)
    merged = gates[:, :, 0] * y_a.astype(jnp.float32) + gates[:, :, 1] * y_b.astype(jnp.float32) + gates[:, :, 2] * y_c.astype(jnp.float32)
    return merged.astype(h.dtype) @ w_o


def swiglu(h, w1, w3, w2):
    return (jax.nn.silu(h @ w1) * (h @ w3)) @ w2


def moe_swiglu(h, w_router, b_router, w1, w3, w2):
    B, L, D = h.shape
    t = h.reshape(B * L, D)
    logits = t.astype(jnp.float32) @ w_router.astype(jnp.float32) + b_router.astype(jnp.float32)
    top_vals, top_idx = lax.top_k(logits, TOP_K)
    top_w = jax.nn.softmax(top_vals, axis=-1)
    combine = jnp.einsum('tk,tke->te', top_w, jax.nn.one_hot(top_idx, N_EXPERTS, dtype=jnp.float32))
    out = jnp.zeros((B * L, D), jnp.float32)
    for e in range(N_EXPERTS):
        he = jax.nn.silu(t @ w1[e]) * (t @ w3[e])
        out = out + combine[:, e:e + 1] * (he @ w2[e]).astype(jnp.float32)
    return out.astype(h.dtype).reshape(B, L, D)


def encoder_trunk(x, norm_mix_g, mixer_params, norm_ffn_g, ffn_w1, ffn_w3, ffn_w2, moe_w_router, moe_b_router, moe_w1, moe_w3, moe_w2):
    cos, sin = rope_tables(x.shape[1])
    for layer in range(DEPTH):
        h = rmsnorm(x, norm_mix_g[layer])
        x = x + token_mixer(h, *[p[layer] for p in mixer_params], cos, sin)
        h = rmsnorm(x, norm_ffn_g[layer])
        j = layer // 2
        if layer % 2 == 0:
            x = x + swiglu(h, ffn_w1[j], ffn_w3[j], ffn_w2[j])
        else:
            x = x + moe_swiglu(h, moe_w_router[j], moe_b_router[j], moe_w1[j], moe_w3[j], moe_w2[j])
    return x


def setup_inputs(seed: int = 0) -> dict:
    key = jax.random.key(seed)
    ks = iter(jax.random.split(key, 48))
    nrm = lambda shape, scale: jax.random.normal(next(ks), shape, jnp.float32) * scale
    gain = lambda shape: 1.0 + 0.02 * jax.random.normal(next(ks), shape, jnp.float32)
    u = jax.random.uniform(next(ks), (DEPTH, 2, LRU_WIDTH), jnp.float32, 0.9, 0.999)
    s = u ** (1.0 / LRU_C)
    lru_lambda = jnp.log(s) - jnp.log1p(-s)
    return {
        'x_prompt': nrm((BATCH, SEQ, D_MODEL), 1.0),
        'x_sample': nrm((DEC_BATCH, DEC_SEQ, D_MODEL), 1.0),
        'norm_mix_g': gain((DEPTH, D_MODEL)),
        'w_in': nrm((DEPTH, D_MODEL, IN_TOTAL), D_MODEL ** -0.5),
        'gla_w_dec': nrm((DEPTH, 2, GLA_LOWRANK, GLA_HEADS * GLA_DK), GLA_LOWRANK ** -0.5),
        'gla_b_dec': nrm((DEPTH, 2, GLA_HEADS * GLA_DK), 0.1),
        'gla_norm_g': gain((DEPTH, GLA_HEADS, GLA_DV)),
        'gla_w_out': nrm((DEPTH, GLA_HEADS * GLA_DV, D_MODEL), (GLA_HEADS * GLA_DV) ** -0.5),
        'lru_conv_w': nrm((DEPTH, CONV_WIDTH, LRU_WIDTH), CONV_WIDTH ** -0.5),
        'lru_conv_b': nrm((DEPTH, LRU_WIDTH), 0.02),
        'lru_w_a': nrm((DEPTH, 2, LRU_BLOCKS, LRU_BLOCK, LRU_BLOCK), LRU_BLOCK ** -0.5),
        'lru_b_a': nrm((DEPTH, 2, LRU_WIDTH), 0.02),
        'lru_w_i': nrm((DEPTH, 2, LRU_BLOCKS, LRU_BLOCK, LRU_BLOCK), LRU_BLOCK ** -0.5),
        'lru_b_i': nrm((DEPTH, 2, LRU_WIDTH), 0.02),
        'lru_lambda': lru_lambda,
        'lru_w_out': nrm((DEPTH, LRU_WIDTH, D_MODEL), LRU_WIDTH ** -0.5),
        'mla_qa_g': gain((DEPTH, MLA_Q_RANK)),
        'mla_w_qb': nrm((DEPTH, MLA_Q_RANK, MLA_HEADS * MLA_QK), MLA_Q_RANK ** -0.5),
        'mla_kva_g': gain((DEPTH, MLA_KV_RANK)),
        'mla_w_kvb': nrm((DEPTH, MLA_KV_RANK, MLA_HEADS * (MLA_NOPE + MLA_V)), MLA_KV_RANK ** -0.5),
        'mla_qn_g': gain((DEPTH, MLA_QK)),
        'mla_kn_g': gain((DEPTH, MLA_QK)),
        'mla_w_out': nrm((DEPTH, MLA_HEADS * MLA_V, D_MODEL), (MLA_HEADS * MLA_V) ** -0.5),
        'w_o': nrm((DEPTH, D_MODEL, D_MODEL), D_MODEL ** -0.5),
        'norm_ffn_g': gain((DEPTH, D_MODEL)),
        'ffn_w1': nrm((N_DENSE, D_MODEL, D_FF), D_MODEL ** -0.5),
        'ffn_w3': nrm((N_DENSE, D_MODEL, D_FF), D_MODEL ** -0.5),
        'ffn_w2': nrm((N_DENSE, D_FF, D_MODEL), D_FF ** -0.5),
        'moe_w_router': nrm((N_MOE, D_MODEL, N_EXPERTS), D_MODEL ** -0.5),
        'moe_b_router': nrm((N_MOE, N_EXPERTS), 0.01),
        'moe_w1': nrm((N_MOE, N_EXPERTS, D_MODEL, EXPERT_FF), D_MODEL ** -0.5),
        'moe_w3': nrm((N_MOE, N_EXPERTS, D_MODEL, EXPERT_FF), D_MODEL ** -0.5),
        'moe_w2': nrm((N_MOE, N_EXPERTS, EXPERT_FF, D_MODEL), EXPERT_FF ** -0.5),
    }


def reference(x_prompt, x_sample, norm_mix_g, w_in, gla_w_dec, gla_b_dec, gla_norm_g, gla_w_out, lru_conv_w, lru_conv_b, lru_w_a, lru_b_a, lru_w_i, lru_b_i, lru_lambda, lru_w_out, mla_qa_g, mla_w_qb, mla_kva_g, mla_w_kvb, mla_qn_g, mla_kn_g, mla_w_out, w_o, norm_ffn_g, ffn_w1, ffn_w3, ffn_w2, moe_w_router, moe_b_router, moe_w1, moe_w3, moe_w2):
    mixer_params = (w_in, gla_w_dec, gla_b_dec, gla_norm_g, gla_w_out, lru_conv_w, lru_conv_b, lru_w_a, lru_b_a, lru_w_i, lru_b_i, lru_lambda, lru_w_out, mla_qa_g, mla_w_qb, mla_kva_g, mla_w_kvb, mla_qn_g, mla_kn_g, mla_w_out, w_o)
    y_prompt = encoder_trunk(x_prompt, norm_mix_g, mixer_params, norm_ffn_g, ffn_w1, ffn_w3, ffn_w2, moe_w_router, moe_b_router, moe_w1, moe_w3, moe_w2)
    y_sample = encoder_trunk(x_sample, norm_mix_g, mixer_params, norm_ffn_g, ffn_w1, ffn_w3, ffn_w2, moe_w_router, moe_b_router, moe_w1, moe_w3, moe_w2)
    return (y_prompt, y_sample)
```

```python
import functools

import jax
import jax.numpy as jnp
from jax import lax
from jax.experimental import pallas as pl
from jax.experimental.pallas import tpu as pltpu

F32 = jnp.float32
BF16 = jnp.bfloat16

D_MODEL = 1024
EPS = 1e-6
GLA_HEADS = 4
GLA_DK = 64
GLA_DV = 128
GLA_LOWRANK = 16
GLA_GATE_NORM = 16.0
GLA_CHUNK = 64
LRU_WIDTH = 512
LRU_BLOCKS = 8
LRU_BLOCK = LRU_WIDTH // LRU_BLOCKS
LRU_C = 8.0
MLA_HEADS = 4
MLA_Q_RANK = 256
MLA_KV_RANK = 128
MLA_NOPE = 128
MLA_ROPE = 64
MLA_V = 128
MLA_QK = MLA_NOPE + MLA_ROPE
MLA_QK_PAD = 256
ROPE_THETA = 10000.0
D_FF = 2816
N_EXPERTS = 8
EXPERT_FF = 1408
LANES = 128

COL_GATES = 0
COL_GV = 3072
COL_GOG = 3584
COL_RX = 4096
COL_RG = 4608
COL_GQ = 5120
COL_GK = 5376
COL_MQA = 5632
COL_MKVA = 5888
COL_MISC = 6016
Z_COLS = 6144

VMEM_LIMIT = 56 * 1024 * 1024


def _cparams(*sem):
    return pltpu.CompilerParams(dimension_semantics=sem, vmem_limit_bytes=VMEM_LIMIT)


def _sigmoid(x):
    return 1.0 / (1.0 + jnp.exp(-x))


def _dot(a, b):
    return jnp.dot(a, b, preferred_element_type=F32)


def _dot_nt(a, b):
    return lax.dot_general(a, b, (((1,), (1,)), ((), ())), preferred_element_type=F32)


def _dot_tn(a, b):
    return lax.dot_general(a, b, (((0,), (0,)), ((), ())), preferred_element_type=F32)


def _inproj_kernel(x_ref, g_ref, w_ref, z_ref, h_ref):
    @pl.when(pl.program_id(1) == 0)
    def _():
        x = x_ref[...]
        ms = jnp.mean(x * x, axis=-1, keepdims=True)
        h_ref[...] = (x * lax.rsqrt(ms + EPS) * g_ref[...]).astype(BF16)

    z_ref[...] = _dot(h_ref[...], w_ref[...]).astype(z_ref.dtype)


def in_proj(x2d, g, w_packed, tm=1024, tn=1536):
    t = x2d.shape[0]
    tm = min(tm, t)
    return pl.pallas_call(
        _inproj_kernel,
        out_shape=jax.ShapeDtypeStruct((t, Z_COLS), BF16),
        grid=(t // tm, Z_COLS // tn),
        in_specs=[
            pl.BlockSpec((tm, D_MODEL), lambda i, j: (i, 0)),
            pl.BlockSpec((1, D_MODEL), lambda i, j: (0, 0)),
            pl.BlockSpec((D_MODEL, tn), lambda i, j: (0, j)),
        ],
        out_specs=pl.BlockSpec((tm, tn), lambda i, j: (i, j)),
        scratch_shapes=[pltpu.VMEM((tm, D_MODEL), BF16)],
        compiler_params=_cparams("parallel", "arbitrary"),
        name="in_proj",
    )(x2d, g, w_packed)


def _gla_kernel(qf_ref, kf_ref, vf_ref, mf_ref, qb_ref, kb_ref, vb_ref, mb_ref, wdec_ref, bdec_ref,
                of_ref, ob_ref, s_ref, g_ref, *, tl):
    c = GLA_CHUNK
    nc = tl // c

    @pl.when(pl.program_id(1) == 0)
    def _():
        s_ref[...] = jnp.zeros_like(s_ref)

    for d, m_ref in enumerate((mf_ref, mb_ref)):
        x = _dot(m_ref[...], wdec_ref[d]) + bdec_ref[d]
        logsig = jnp.minimum(x, 0.0) - jnp.log(1.0 + jnp.exp(-jnp.abs(x)))
        g_ref[d] = logsig * (1.0 / GLA_GATE_NORM)

    row = lax.broadcasted_iota(jnp.int32, (c, c), 0)
    col = lax.broadcasted_iota(jnp.int32, (c, c), 1)
    tri = (row >= col, row <= col)
    lane = lax.broadcasted_iota(jnp.int32, (c, LANES), 1)
    head_mask = (lane < GLA_DK, lane >= GLA_DK)
    dirs = ((qf_ref, kf_ref, vf_ref, of_ref), (qb_ref, kb_ref, vb_ref, ob_ref))

    def one_chunk(d, cc):
        q_ref, k_ref, v_ref, o_ref = dirs[d]
        r0 = pl.multiple_of(cc * c, c)
        g = g_ref[d, pl.ds(r0, c), :]
        g_hi = g.astype(BF16)
        r1 = g - g_hi.astype(F32)
        g_mid = r1.astype(BF16)
        g_lo = (r1 - g_mid.astype(F32)).astype(BF16)
        ones_tri = jnp.where(tri[d], 1.0, 0.0).astype(BF16)
        b = _dot(ones_tri, g_hi) + _dot(ones_tri, g_mid) + _dot(ones_tri, g_lo)
        b_tot = b[c - 1:c, :] if d == 0 else b[0:1, :]
        qc = q_ref[pl.ds(r0, c), :].astype(F32) * (GLA_DK ** -0.5)
        kc = k_ref[pl.ds(r0, c), :].astype(F32)
        q_dec = qc * jnp.exp(b)
        k_inv = kc * jnp.exp(-b)
        k_end = kc * jnp.exp(b_tot - b)
        chunk_decay = jnp.exp(b_tot)
        for p in range(GLA_HEADS // 2):
            sl = slice(p * LANES, (p + 1) * LANES)
            qd_p, ke_p = q_dec[:, sl], k_end[:, sl]
            ki_p = k_inv[:, sl].astype(BF16)
            dec_p = chunk_decay[:, sl]
            for hh in range(2):
                h = 2 * p + hh
                qm = jnp.where(head_mask[hh], qd_p, 0.0).astype(BF16)
                km = jnp.where(head_mask[hh], ke_p, 0.0).astype(BF16)
                scores = jnp.where(tri[d], _dot_nt(qm, ki_p), 0.0).astype(BF16)
                v_h = v_ref[pl.ds(r0, c), h * GLA_DV:(h + 1) * GLA_DV]
                st = s_ref[d, h]
                o_h = _dot(scores, v_h) + _dot_nt(qm, st.astype(BF16))
                s_ref[d, h] = st * dec_p + _dot_tn(v_h, km)
                o_ref[pl.ds(r0, c), h * GLA_DV:(h + 1) * GLA_DV] = o_h.astype(o_ref.dtype)

    def body(ci, carry):
        one_chunk(0, ci)
        one_chunk(1, nc - 1 - ci)
        return carry

    lax.fori_loop(0, nc, body, 0)


def gla_mixer(z, wdec, bdec, batch, seq, tl=512):
    t = batch * seq
    tl = min(tl, seq)
    nb = seq // tl
    fwd = lambda b, i: b * nb + i
    bwd = lambda b, i: b * nb + (nb - 1 - i)

    def specs(rb):
        return [
            pl.BlockSpec((tl, 256), lambda b, i: (rb(b, i), COL_GQ // 256)),
            pl.BlockSpec((tl, 256), lambda b, i: (rb(b, i), COL_GK // 256)),
            pl.BlockSpec((tl, 512), lambda b, i: (rb(b, i), COL_GV // 512)),
            pl.BlockSpec((tl, LANES), lambda b, i: (rb(b, i), COL_MISC // LANES)),
        ]

    out_sd = jax.ShapeDtypeStruct((t, GLA_HEADS * GLA_DV), BF16)
    return pl.pallas_call(
        functools.partial(_gla_kernel, tl=tl),
        out_shape=(out_sd, out_sd),
        grid=(batch, nb),
        in_specs=specs(fwd) + specs(bwd) + [
            pl.BlockSpec((2, LANES, 256), lambda b, i: (0, 0, 0)),
            pl.BlockSpec((2, 1, 256), lambda b, i: (0, 0, 0)),
        ],
        out_specs=(
            pl.BlockSpec((tl, 512), lambda b, i: (fwd(b, i), 0)),
            pl.BlockSpec((tl, 512), lambda b, i: (bwd(b, i), 0)),
        ),
        scratch_shapes=[
            pltpu.VMEM((2, GLA_HEADS, GLA_DV, LANES), F32),
            pltpu.VMEM((2, tl, 256), F32),
        ],
        compiler_params=_cparams("parallel", "arbitrary"),
        name="gla_mixer",
    )(z, z, z, z, z, z, z, z, wdec, bdec)


HALO = 16
SCAN_TILE = 8


def _lru_kernel(xf_ref, xfp_ref, xfn_ref, xb_ref, xbp_ref, xbn_ref, cw_ref, cb_ref, wa_ref, ba_ref,
                wi_ref, bi_ref, lam_ref, hf_ref, hb_ref, a_sc, h_sc, carry_sc, *, tl, nb):
    i = pl.program_id(1)

    @pl.when(i == 0)
    def _():
        carry_sc[...] = jnp.zeros_like(carry_sc)

    row = lax.broadcasted_iota(jnp.int32, (tl, LRU_WIDTH), 0)
    in_tile = row & (SCAN_TILE - 1)
    blocks = ((xf_ref, xfp_ref, xfn_ref, i), (xb_ref, xbp_ref, xbn_ref, nb - 1 - i))

    for d, (x_ref, xp_ref, xn_ref, li) in enumerate(blocks):
        x = x_ref[...].astype(F32)
        prev = jnp.where(li > 0, xp_ref[...].astype(F32), 0.0)
        nxt = jnp.where(li < nb - 1, xn_ref[...].astype(F32), 0.0)
        x_m1 = jnp.where(row == 0, prev[HALO - 1:HALO, :], pltpu.roll(x, 1, 0))
        x_m2 = jnp.where(row == 0, prev[HALO - 2:HALO - 1, :],
                         jnp.where(row == 1, prev[HALO - 1:HALO, :], pltpu.roll(x, 2, 0)))
        x_p1 = jnp.where(row == tl - 1, nxt[0:1, :], pltpu.roll(x, tl - 1, 0))
        cw = cw_ref[...]
        u = cw[0:1, :] * x_m2 + cw[1:2, :] * x_m1 + cw[2:3, :] * x + cw[3:4, :] * x_p1 + cb_ref[...]
        ub = u.astype(BF16)
        r = _sigmoid(_dot(ub, wa_ref[d]) + ba_ref[d])
        ig = _sigmoid(_dot(ub, wi_ref[d]) + bi_ref[d])
        lam = lam_ref[d]
        softplus_neg = jnp.maximum(-lam, 0.0) + jnp.log(1.0 + jnp.exp(-jnp.abs(lam)))
        a = jnp.exp(r * (-LRU_C * softplus_neg))
        hh = jnp.sqrt(1.0 - a * a) * ig * u
        for s in (1, 2, 4):
            if d == 0:
                a_sh, h_sh = pltpu.roll(a, s, 0), pltpu.roll(hh, s, 0)
                ok = in_tile >= s
            else:
                a_sh, h_sh = pltpu.roll(a, tl - s, 0), pltpu.roll(hh, tl - s, 0)
                ok = in_tile < SCAN_TILE - s
            hh = jnp.where(ok, a * h_sh + hh, hh)
            a = jnp.where(ok, a * a_sh, a)
        a_sc[d] = a
        h_sc[d] = hh

    nt = tl // SCAN_TILE

    def body(j, carry):
        cf, cb = carry
        rf = pl.multiple_of(j * SCAN_TILE, SCAN_TILE)
        hf = a_sc[0, pl.ds(rf, SCAN_TILE), :] * cf + h_sc[0, pl.ds(rf, SCAN_TILE), :]
        h_sc[0, pl.ds(rf, SCAN_TILE), :] = hf
        rb = pl.multiple_of((nt - 1 - j) * SCAN_TILE, SCAN_TILE)
        hb = a_sc[1, pl.ds(rb, SCAN_TILE), :] * cb + h_sc[1, pl.ds(rb, SCAN_TILE), :]
        h_sc[1, pl.ds(rb, SCAN_TILE), :] = hb
        cf = jnp.broadcast_to(hf[SCAN_TILE - 1:SCAN_TILE, :], (SCAN_TILE, LRU_WIDTH))
        cb = jnp.broadcast_to(hb[0:1, :], (SCAN_TILE, LRU_WIDTH))
        return cf, cb

    cf, cb = lax.fori_loop(0, nt, body, (carry_sc[0], carry_sc[1]))
    carry_sc[0] = cf
    carry_sc[1] = cb
    hf_ref[...] = h_sc[0].astype(hf_ref.dtype)
    hb_ref[...] = h_sc[1].astype(hb_ref.dtype)


def lru_mixer(z, conv_w, conv_b, wa, ba, wi, bi, lam, batch, seq, tl=512):
    t = batch * seq
    tl = min(tl, seq)
    nb = seq // tl
    hpb = tl // HALO
    last_halo = t // HALO - 1
    fwd = lambda b, i: b * nb + i
    bwd = lambda b, i: b * nb + (nb - 1 - i)
    colx = COL_RX // LRU_WIDTH

    def specs(rb):
        return [
            pl.BlockSpec((tl, LRU_WIDTH), lambda b, i: (rb(b, i), colx)),
            pl.BlockSpec((HALO, LRU_WIDTH), lambda b, i: (jnp.maximum(rb(b, i) * hpb - 1, 0), colx)),
            pl.BlockSpec((HALO, LRU_WIDTH), lambda b, i: (jnp.minimum((rb(b, i) + 1) * hpb, last_halo), colx)),
        ]

    full = lambda shape: pl.BlockSpec(shape, lambda b, i: (0,) * len(shape))
    out_sd = jax.ShapeDtypeStruct((t, LRU_WIDTH), BF16)
    return pl.pallas_call(
        functools.partial(_lru_kernel, tl=tl, nb=nb),
        out_shape=(out_sd, out_sd),
        grid=(batch, nb),
        in_specs=specs(fwd) + specs(bwd) + [
            full((4, LRU_WIDTH)), full((1, LRU_WIDTH)),
            full((2, LRU_WIDTH, LRU_WIDTH)), full((2, 1, LRU_WIDTH)),
            full((2, LRU_WIDTH, LRU_WIDTH)), full((2, 1, LRU_WIDTH)),
            full((2, 1, LRU_WIDTH)),
        ],
        out_specs=(
            pl.BlockSpec((tl, LRU_WIDTH), lambda b, i: (fwd(b, i), 0)),
            pl.BlockSpec((tl, LRU_WIDTH), lambda b, i: (bwd(b, i), 0)),
        ),
        scratch_shapes=[
            pltpu.VMEM((2, tl, LRU_WIDTH), F32),
            pltpu.VMEM((2, tl, LRU_WIDTH), F32),
            pltpu.VMEM((2, SCAN_TILE, LRU_WIDTH), F32),
        ],
        compiler_params=_cparams("parallel", "arbitrary"),
        name="lru_mixer",
    )(z, z, z, z, z, z, conv_w, conv_b, wa, ba, wi, bi, lam)


def _rope(x, cos_t, sin_a, sin_b):
    return x * cos_t + pltpu.roll(x, LANES - MLA_ROPE // 2, 1) * sin_a + pltpu.roll(x, MLA_ROPE // 2, 1) * sin_b


def _mla_prep_kernel(qa_ref, kva_ref, misc_ref, cos_ref, sa_ref, sb_ref, qag_ref, wqb_ref, kvag_ref, wkvb_ref,
                     qng_ref, kng_ref, q_out, k_out, v_out):
    cos_t, sin_a, sin_b = cos_ref[...], sa_ref[...], sb_ref[...]
    scale = MLA_QK ** -0.5

    qa = qa_ref[...].astype(F32)
    qa_n = qa * lax.rsqrt(jnp.mean(qa * qa, axis=-1, keepdims=True) + EPS) * qag_ref[...]
    q = _dot(qa_n.astype(BF16), wqb_ref[...])
    qng = qng_ref[...]
    for h in range(MLA_HEADS):
        qh = q[:, h * MLA_QK_PAD:(h + 1) * MLA_QK_PAD]
        rstd = lax.rsqrt(jnp.sum(qh * qh, axis=-1, keepdims=True) * (1.0 / MLA_QK) + EPS)
        qh = qh * rstd * qng
        q_out[:, h * MLA_QK_PAD:h * MLA_QK_PAD + LANES] = (qh[:, :LANES] * scale).astype(q_out.dtype)
        q_out[:, h * MLA_QK_PAD + LANES:(h + 1) * MLA_QK_PAD] = (
            _rope(qh[:, LANES:], cos_t, sin_a, sin_b) * scale).astype(q_out.dtype)

    kva = kva_ref[...].astype(F32)
    kva_n = kva * lax.rsqrt(jnp.mean(kva * kva, axis=-1, keepdims=True) + EPS) * kvag_ref[...]
    kv = _dot(kva_n.astype(BF16), wkvb_ref[...])
    lane = lax.broadcasted_iota(jnp.int32, misc_ref.shape, 1)
    kr = jnp.where(lane < MLA_ROPE, misc_ref[...].astype(F32), 0.0)
    kr_ss = jnp.sum(kr * kr, axis=-1, keepdims=True)
    kng = kng_ref[...]
    kr_rot = _rope(kr * kng[:, LANES:], cos_t, sin_a, sin_b)
    for h in range(MLA_HEADS):
        k_nope = kv[:, h * 256:h * 256 + LANES]
        rstd = lax.rsqrt((jnp.sum(k_nope * k_nope, axis=-1, keepdims=True) + kr_ss) * (1.0 / MLA_QK) + EPS)
        k_out[:, h * MLA_QK_PAD:h * MLA_QK_PAD + LANES] = (k_nope * rstd * kng[:, :LANES]).astype(k_out.dtype)
        k_out[:, h * MLA_QK_PAD + LANES:(h + 1) * MLA_QK_PAD] = (kr_rot * rstd).astype(k_out.dtype)
        v_out[:, h * MLA_V:(h + 1) * MLA_V] = kv[:, h * 256 + LANES:(h + 1) * 256].astype(v_out.dtype)


def mla_prep(z, cos_t, sin_a, sin_b, qag, wqb, kvag, wkvb, qng, kng, batch, seq, tm=512):
    t = batch * seq
    tm = min(tm, seq)
    nb = seq // tm
    full = lambda shape: pl.BlockSpec(shape, lambda i: (0,) * len(shape))
    tab = pl.BlockSpec((tm, LANES), lambda i: (i % nb, 0))
    return pl.pallas_call(
        _mla_prep_kernel,
        out_shape=(
            jax.ShapeDtypeStruct((t, MLA_HEADS * MLA_QK_PAD), BF16),
            jax.ShapeDtypeStruct((t, MLA_HEADS * MLA_QK_PAD), BF16),
            jax.ShapeDtypeStruct((t, MLA_HEADS * MLA_V), BF16),
        ),
        grid=(t // tm,),
        in_specs=[
            pl.BlockSpec((tm, MLA_Q_RANK), lambda i: (i, COL_MQA // MLA_Q_RANK)),
            pl.BlockSpec((tm, MLA_KV_RANK), lambda i: (i, COL_MKVA // MLA_KV_RANK)),
            pl.BlockSpec((tm, LANES), lambda i: (i, COL_MISC // LANES)),
            tab, tab, tab,
            full((1, MLA_Q_RANK)), full((MLA_Q_RANK, MLA_HEADS * MLA_QK_PAD)),
            full((1, MLA_KV_RANK)), full((MLA_KV_RANK, MLA_HEADS * 256)),
            full((1, MLA_QK_PAD)), full((1, MLA_QK_PAD)),
        ],
        out_specs=(
            pl.BlockSpec((tm, MLA_HEADS * MLA_QK_PAD), lambda i: (i, 0)),
            pl.BlockSpec((tm, MLA_HEADS * MLA_QK_PAD), lambda i: (i, 0)),
            pl.BlockSpec((tm, MLA_HEADS * MLA_V), lambda i: (i, 0)),
        ),
        compiler_params=_cparams("parallel"),
        name="mla_prep",
    )(z, z, z, cos_t, sin_a, sin_b, qag, wqb, kvag, wkvb, qng, kng)


def _attn_kernel(q_ref, k_ref, v_ref, o_ref, m_sc, l_sc, acc_sc, *, tk):
    nk = k_ref.shape[0] // tk
    q = q_ref[...]
    m_sc[...] = jnp.full_like(m_sc, -jnp.inf)
    l_sc[...] = jnp.zeros_like(l_sc)
    acc_sc[...] = jnp.zeros_like(acc_sc)

    def body(j, carry):
        r0 = pl.multiple_of(j * tk, tk)
        s = _dot_nt(q, k_ref[pl.ds(r0, tk), :])
        m_prev = m_sc[...]
        m_new = jnp.maximum(m_prev, jnp.max(s, axis=-1, keepdims=True))
        p = jnp.exp(s - jnp.concatenate([m_new] * (tk // LANES), axis=1))
        alpha = jnp.exp(m_prev - m_new)
        l_sc[...] = alpha * l_sc[...] + jnp.sum(p, axis=-1, keepdims=True)
        acc_sc[...] = alpha * acc_sc[...] + _dot(p.astype(BF16), v_ref[pl.ds(r0, tk), :])
        m_sc[...] = m_new
        return carry

    lax.fori_loop(0, nk, body, 0)
    o_ref[...] = (acc_sc[...] / l_sc[...]).astype(o_ref.dtype)


def attention(q, k, v, batch, seq, tq=512, tk=512):
    t = batch * seq
    tq = min(tq, seq)
    tk = min(tk, seq)
    nq = seq // tq
    return pl.pallas_call(
        functools.partial(_attn_kernel, tk=tk),
        out_shape=jax.ShapeDtypeStruct((t, MLA_HEADS * MLA_V), BF16),
        grid=(batch, MLA_HEADS, nq),
        in_specs=[
            pl.BlockSpec((tq, MLA_QK_PAD), lambda b, h, i: (b * nq + i, h)),
            pl.BlockSpec((seq, MLA_QK_PAD), lambda b, h, i: (b, h)),
            pl.BlockSpec((seq, MLA_V), lambda b, h, i: (b, h)),
        ],
        out_specs=pl.BlockSpec((tq, MLA_V), lambda b, h, i: (b * nq + i, h)),
        scratch_shapes=[
            pltpu.VMEM((tq, LANES), F32),
            pltpu.VMEM((tq, LANES), F32),
            pltpu.VMEM((tq, MLA_V), F32),
        ],
        compiler_params=_cparams("parallel", "parallel", "arbitrary"),
        name="mla_attention",
    )(q, k, v)


def _split3(x):
    hi = x.astype(BF16)
    r1 = x - hi.astype(F32)
    mid = r1.astype(BF16)
    lo = (r1 - mid.astype(F32)).astype(BF16)
    return hi, mid, lo


def _merge_kernel(*refs, with_router):
    (x_ref, gates_ref, og_ref, rg_ref, gf_ref, gb_ref, lf_ref, lb_ref, am_ref,
     gng_ref, wga_ref, wlr_ref, wml_ref, wo_ref, fng_ref) = refs[:15]
    if with_router:
        wr_ref, br_ref, x_out, h_out, comb_out = refs[15:]
    else:
        x_out, h_out = refs[15:]

    o = gf_ref[...].astype(F32) + gb_ref[...].astype(F32)
    gng = gng_ref[...]
    parts = []
    for h in range(GLA_HEADS):
        oh = o[:, h * GLA_DV:(h + 1) * GLA_DV]
        rstd = lax.rsqrt(jnp.mean(oh * oh, axis=-1, keepdims=True) + EPS)
        parts.append(oh * rstd * gng[:, h * GLA_DV:(h + 1) * GLA_DV])
    og = og_ref[...].astype(F32)
    ya_in = jnp.concatenate(parts, axis=-1) * (og * _sigmoid(og))
    y_a = _dot(ya_in.astype(BF16), wga_ref[...])

    rg = rg_ref[...].astype(F32)
    gelu = 0.5 * rg * (1.0 + jnp.tanh(0.7978845608028654 * (rg + 0.044715 * rg * rg * rg)))
    yb_in = (lf_ref[...].astype(F32) + lb_ref[...].astype(F32)) * gelu
    y_b = _dot(yb_in.astype(BF16), wlr_ref[...])

    y_c = _dot(am_ref[...], wml_ref[...])

    d = D_MODEL
    merged = (_sigmoid(gates_ref[:, 0:d].astype(F32)) * y_a
              + _sigmoid(gates_ref[:, d:2 * d].astype(F32)) * y_b
              + _sigmoid(gates_ref[:, 2 * d:3 * d].astype(F32)) * y_c)
    x_new = x_ref[...] + _dot(merged.astype(BF16), wo_ref[...])
    x_out[...] = x_new
    h = x_new * lax.rsqrt(jnp.mean(x_new * x_new, axis=-1, keepdims=True) + EPS) * fng_ref[...]
    h_out[...] = h.astype(h_out.dtype)

    if with_router:
        h_hi, h_mid, h_lo = _split3(h)
        w_hi, w_mid, w_lo = _split3(wr_ref[...])
        logits = (_dot(h_hi, w_hi) + (_dot(h_hi, w_mid) + _dot(h_mid, w_hi))
                  + (_dot(h_hi, w_lo) + _dot(h_mid, w_mid) + _dot(h_lo, w_hi))) + br_ref[...]
        lane = lax.broadcasted_iota(jnp.int32, logits.shape, 1)
        logits = jnp.where(lane < N_EXPERTS, logits, -jnp.inf)
        v1 = jnp.max(logits, axis=-1, keepdims=True)
        i1 = jnp.min(jnp.where(logits == v1, lane, LANES), axis=-1, keepdims=True)
        rest = jnp.where(lane == i1, -jnp.inf, logits)
        v2 = jnp.max(rest, axis=-1, keepdims=True)
        i2 = jnp.min(jnp.where(rest == v2, lane, LANES), axis=-1, keepdims=True)
        e2 = jnp.exp(v2 - v1)
        w1 = 1.0 / (1.0 + e2)
        comb_out[...] = jnp.where(lane == i1, w1, 0.0) + jnp.where(lane == i2, e2 * w1, 0.0)


def merge(x2d, z, gla_f, gla_b, lru_f, lru_b, attn, gng, wga, wlr, wml, wo, fng, router=None, tm=512):
    t = x2d.shape[0]
    tm = min(tm, t)
    full = lambda shape: pl.BlockSpec(shape, lambda i: (0,) * len(shape))
    row512 = pl.BlockSpec((tm, 512), lambda i: (i, 0))
    in_specs = [
        pl.BlockSpec((tm, D_MODEL), lambda i: (i, 0)),
        pl.BlockSpec((tm, 3 * D_MODEL), lambda i: (i, COL_GATES // (3 * D_MODEL))),
        pl.BlockSpec((tm, 512), lambda i: (i, COL_GOG // 512)),
        pl.BlockSpec((tm, 512), lambda i: (i, COL_RG // 512)),
        row512, row512, row512, row512, row512,
        full((1, 512)), full((512, D_MODEL)), full((512, D_MODEL)), full((512, D_MODEL)),
        full((D_MODEL, D_MODEL)), full((1, D_MODEL)),
    ]
    args = [x2d, z, z, z, gla_f, gla_b, lru_f, lru_b, attn, gng, wga, wlr, wml, wo, fng]
    out_shape = [jax.ShapeDtypeStruct((t, D_MODEL), F32), jax.ShapeDtypeStruct((t, D_MODEL), BF16)]
    out_specs = [pl.BlockSpec((tm, D_MODEL), lambda i: (i, 0)), pl.BlockSpec((tm, D_MODEL), lambda i: (i, 0))]
    if router is not None:
        in_specs += [full((D_MODEL, LANES)), full((1, LANES))]
        args += list(router)
        out_shape.append(jax.ShapeDtypeStruct((t, LANES), F32))
        out_specs.append(pl.BlockSpec((tm, LANES), lambda i: (i, 0)))
    return pl.pallas_call(
        functools.partial(_merge_kernel, with_router=router is not None),
        out_shape=tuple(out_shape),
        grid=(t // tm,),
        in_specs=in_specs,
        out_specs=tuple(out_specs),
        compiler_params=_cparams("parallel"),
        name="merge_router" if router is not None else "merge",
    )(*args)


def _ffn_kernel(h_ref, x_ref, w1_ref, w3_ref, w2_ref, o_ref):
    h = h_ref[...]
    a = _dot(h, w1_ref[...])
    y = _dot((a * _sigmoid(a) * _dot(h, w3_ref[...])).astype(BF16), w2_ref[...])

    @pl.when(pl.program_id(1) == 0)
    def _():
        o_ref[...] = x_ref[...] + y

    @pl.when(pl.program_id(1) > 0)
    def _():
        o_ref[...] += y


def ffn(h, x2d, w1, w3, w2, tm=512, tf=1408):
    t = x2d.shape[0]
    tm = min(tm, t)
    return pl.pallas_call(
        _ffn_kernel,
        out_shape=jax.ShapeDtypeStruct((t, D_MODEL), F32),
        grid=(t // tm, D_FF // tf),
        in_specs=[
            pl.BlockSpec((tm, D_MODEL), lambda i, j: (i, 0)),
            pl.BlockSpec((tm, D_MODEL), lambda i, j: (i, 0)),
            pl.BlockSpec((D_MODEL, tf), lambda i, j: (0, j)),
            pl.BlockSpec((D_MODEL, tf), lambda i, j: (0, j)),
            pl.BlockSpec((tf, D_MODEL), lambda i, j: (j, 0)),
        ],
        out_specs=pl.BlockSpec((tm, D_MODEL), lambda i, j: (i, 0)),
        compiler_params=_cparams("parallel", "arbitrary"),
        name="ffn",
    )(h, x2d, w1, w3, w2)


def _moe_kernel(h_ref, x_ref, comb_ref, w1_ref, w3_ref, w2_ref, o_ref):
    e = pl.program_id(1)
    h = h_ref[...]
    a = _dot(h, w1_ref[...])
    y = _dot((a * _sigmoid(a) * _dot(h, w3_ref[...])).astype(BF16), w2_ref[...])
    comb = comb_ref[...]
    lane = lax.broadcasted_iota(jnp.int32, comb.shape, 1)
    c = jnp.sum(jnp.where(lane == e, comb, 0.0), axis=-1, keepdims=True)

    @pl.when(e == 0)
    def _():
        o_ref[...] = x_ref[...] + c * y

    @pl.when(e > 0)
    def _():
        o_ref[...] += c * y


def moe(h, x2d, comb, w1, w3, w2, tm=512):
    t = x2d.shape[0]
    tm = min(tm, t)
    return pl.pallas_call(
        _moe_kernel,
        out_shape=jax.ShapeDtypeStruct((t, D_MODEL), F32),
        grid=(t // tm, N_EXPERTS),
        in_specs=[
            pl.BlockSpec((tm, D_MODEL), lambda i, e: (i, 0)),
            pl.BlockSpec((tm, D_MODEL), lambda i, e: (i, 0)),
            pl.BlockSpec((tm, LANES), lambda i, e: (i, 0)),
            pl.BlockSpec((None, D_MODEL, EXPERT_FF), lambda i, e: (e, 0, 0)),
            pl.BlockSpec((None, D_MODEL, EXPERT_FF), lambda i, e: (e, 0, 0)),
            pl.BlockSpec((None, EXPERT_FF, D_MODEL), lambda i, e: (e, 0, 0)),
        ],
        out_specs=pl.BlockSpec((tm, D_MODEL), lambda i, e: (i, 0)),
        compiler_params=_cparams("parallel", "arbitrary"),
        name="moe",
    )(h, x2d, comb, w1, w3, w2)


def _pack_layer(layer, p):
    w_in = p["w_in"][layer]
    sizes = (256, 256, 512, 512, 32, 512, 512, 256, 128, 64, 3072)
    offs = [0]
    for s in sizes:
        offs.append(offs[-1] + s)
    seg = lambda n: w_in[:, offs[n]:offs[n + 1]]
    g_q, g_k, g_v, g_og, g_dec, r_x, r_gate, m_qa, m_kva, m_kr, gates = (seg(n) for n in range(11))
    w_packed = jnp.concatenate(
        [gates, g_v, g_og, r_x, r_gate, g_q, g_k, m_qa, m_kva, m_kr, g_dec, jnp.zeros((D_MODEL, 32), F32)],
        axis=1).astype(BF16)

    wdec = jnp.zeros((2, LANES, 256), F32)
    for d in range(2):
        lo = MLA_ROPE + d * GLA_LOWRANK
        wdec = wdec.at[d, lo:lo + GLA_LOWRANK, :].set(p["gla_w_dec"][layer, d])
    bdec = p["gla_b_dec"][layer].reshape(2, 1, 256)

    def block_diag(w):
        out = jnp.zeros((2, LRU_WIDTH, LRU_WIDTH), F32)
        for n in range(LRU_BLOCKS):
            sl = slice(n * LRU_BLOCK, (n + 1) * LRU_BLOCK)
            out = out.at[:, sl, sl].set(w[:, n])
        return out.astype(BF16)

    wqb = p["mla_w_qb"][layer].reshape(MLA_Q_RANK, MLA_HEADS, MLA_QK)
    wqb = jnp.pad(wqb, ((0, 0), (0, 0), (0, MLA_QK_PAD - MLA_QK))).reshape(MLA_Q_RANK, MLA_HEADS * MLA_QK_PAD)
    pad_qk = lambda g: jnp.pad(g, (0, MLA_QK_PAD - MLA_QK)).reshape(1, MLA_QK_PAD)
    return dict(
        w_in=w_packed,
        norm_mix_g=p["norm_mix_g"][layer].reshape(1, D_MODEL),
        wdec=wdec.astype(BF16), bdec=bdec,
        conv_w=p["lru_conv_w"][layer], conv_b=p["lru_conv_b"][layer].reshape(1, LRU_WIDTH),
        wa=block_diag(p["lru_w_a"][layer]), ba=p["lru_b_a"][layer].reshape(2, 1, LRU_WIDTH),
        wi=block_diag(p["lru_w_i"][layer]), bi=p["lru_b_i"][layer].reshape(2, 1, LRU_WIDTH),
        lam=p["lru_lambda"][layer].reshape(2, 1, LRU_WIDTH),
        qag=p["mla_qa_g"][layer].reshape(1, MLA_Q_RANK), wqb=wqb.astype(BF16),
        kvag=p["mla_kva_g"][layer].reshape(1, MLA_KV_RANK), wkvb=p["mla_w_kvb"][layer].astype(BF16),
        qng=pad_qk(p["mla_qn_g"][layer]), kng=pad_qk(p["mla_kn_g"][layer]),
        gng=p["gla_norm_g"][layer].reshape(1, GLA_HEADS * GLA_DV),
        wga=p["gla_w_out"][layer].astype(BF16), wlr=p["lru_w_out"][layer].astype(BF16),
        wml=p["mla_w_out"][layer].astype(BF16), wo=p["w_o"][layer].astype(BF16),
        fng=p["norm_ffn_g"][layer].reshape(1, D_MODEL),
    )


def _rope_tables(length):
    pos = jnp.arange(length, dtype=F32)
    inv = ROPE_THETA ** (-jnp.arange(0, MLA_ROPE, 2, dtype=F32) / MLA_ROPE)
    ang = pos[:, None] * inv[None, :]
    cos, sin = jnp.cos(ang), jnp.sin(ang)
    zero = jnp.zeros_like(cos)
    cos_t = jnp.concatenate([cos, cos, zero, zero], axis=1)
    sin_a = jnp.concatenate([-sin, zero, zero, zero], axis=1)
    sin_b = jnp.concatenate([zero, sin, zero, zero], axis=1)
    return cos_t, sin_a, sin_b


def _trunk(x, layers, ffn_params, moe_params):
    batch, seq, _ = x.shape
    x2d = x.reshape(batch * seq, D_MODEL)
    tables = _rope_tables(seq)
    for layer, lp in enumerate(layers):
        z = in_proj(x2d, lp["norm_mix_g"], lp["w_in"])
        gla_f, gla_b = gla_mixer(z, lp["wdec"], lp["bdec"], batch, seq)
        lru_f, lru_b = lru_mixer(z, lp["conv_w"], lp["conv_b"], lp["wa"], lp["ba"], lp["wi"], lp["bi"], lp["lam"],
                                 batch, seq)
        q, k, v = mla_prep(z, *tables, lp["qag"], lp["wqb"], lp["kvag"], lp["wkvb"], lp["qng"], lp["kng"], batch, seq)
        attn = attention(q, k, v, batch, seq)
        j = layer // 2
        branch = (gla_f, gla_b, lru_f, lru_b, attn, lp["gng"], lp["wga"], lp["wlr"], lp["wml"], lp["wo"], lp["fng"])
        if layer % 2 == 0:
            x2d, h = merge(x2d, z, *branch)
            w1, w3, w2 = ffn_params[j]
            x2d = ffn(h, x2d, w1, w3, w2)
        else:
            x2d, h, comb = merge(x2d, z, *branch, router=moe_params[j][:2])
            _, _, w1, w3, w2 = moe_params[j]
            x2d = moe(h, x2d, comb, w1, w3, w2)
    return x2d.reshape(batch, seq, D_MODEL)


def kernel(x_prompt, x_sample, norm_mix_g, w_in, gla_w_dec, gla_b_dec, gla_norm_g, gla_w_out, lru_conv_w, lru_conv_b, lru_w_a, lru_b_a, lru_w_i, lru_b_i, lru_lambda, lru_w_out, mla_qa_g, mla_w_qb, mla_kva_g, mla_w_kvb, mla_qn_g, mla_kn_g, mla_w_out, w_o, norm_ffn_g, ffn_w1, ffn_w3, ffn_w2, moe_w_router, moe_b_router, moe_w1, moe_w3, moe_w2):
    p = dict(norm_mix_g=norm_mix_g, w_in=w_in, gla_w_dec=gla_w_dec, gla_b_dec=gla_b_dec, gla_norm_g=gla_norm_g,
             gla_w_out=gla_w_out, lru_conv_w=lru_conv_w, lru_conv_b=lru_conv_b, lru_w_a=lru_w_a, lru_b_a=lru_b_a,
             lru_w_i=lru_w_i, lru_b_i=lru_b_i, lru_lambda=lru_lambda, lru_w_out=lru_w_out, mla_qa_g=mla_qa_g,
             mla_w_qb=mla_w_qb, mla_kva_g=mla_kva_g, mla_w_kvb=mla_w_kvb, mla_qn_g=mla_qn_g, mla_kn_g=mla_kn_g,
             mla_w_out=mla_w_out, w_o=w_o, norm_ffn_g=norm_ffn_g)
    depth = w_in.shape[0]
    layers = [_pack_layer(layer, p) for layer in range(depth)]
    ffn_params = [(ffn_w1[j].astype(BF16), ffn_w3[j].astype(BF16), ffn_w2[j].astype(BF16))
                  for j in range(ffn_w1.shape[0])]
    moe_params = []
    for j in range(moe_w1.shape[0]):
        w_router = jnp.pad(moe_w_router[j], ((0, 0), (0, LANES - N_EXPERTS)))
        b_router = jnp.pad(moe_b_router[j], (0, LANES - N_EXPERTS)).reshape(1, LANES)
        moe_params.append((w_router, b_router, moe_w1[j].astype(BF16), moe_w3[j].astype(BF16),
                           moe_w2[j].astype(BF16)))
    y_prompt = _trunk(x_prompt, layers, ffn_params, moe_params)
    y_sample = _trunk(x_sample, layers, ffn_params, moe_params)
    return (y_prompt, y_sample)
```

```python
import functools

import jax
import jax.numpy as jnp
from jax import lax
from jax.experimental import pallas as pl
from jax.experimental.pallas import tpu as pltpu

F32 = jnp.float32
BF16 = jnp.bfloat16

D_MODEL = 1024
EPS = 1e-6
GLA_HEADS = 4
GLA_DK = 64
GLA_DV = 128
GLA_LOWRANK = 16
GLA_GATE_NORM = 16.0
GLA_CHUNK = 64
LRU_WIDTH = 512
LRU_BLOCKS = 8
LRU_BLOCK = LRU_WIDTH // LRU_BLOCKS
LRU_C = 8.0
MLA_HEADS = 4
MLA_Q_RANK = 256
MLA_KV_RANK = 128
MLA_NOPE = 128
MLA_ROPE = 64
MLA_V = 128
MLA_QK = MLA_NOPE + MLA_ROPE
MLA_QK_PAD = 256
ROPE_THETA = 10000.0
D_FF = 2816
N_EXPERTS = 8
EXPERT_FF = 1408
LANES = 128
LOG2_E = 1.4426950408889634

COL_GATES = 0
COL_GV = 3072
COL_GOG = 3584
COL_RX = 4096
COL_RG = 4608
COL_GQ = 5120
COL_GK = 5376
COL_MQA = 5632
COL_MKVA = 5888
COL_MISC = 6016
Z_COLS = 6144

VMEM_LIMIT = 56 * 1024 * 1024


def _cparams(*sem):
    return pltpu.CompilerParams(dimension_semantics=sem, vmem_limit_bytes=VMEM_LIMIT)


def _sigmoid(x):
    return 1.0 / (1.0 + jnp.exp(-x))


def _dot(a, b):
    return jnp.dot(a, b, preferred_element_type=F32)


def _dot_nt(a, b):
    return lax.dot_general(a, b, (((1,), (1,)), ((), ())), preferred_element_type=F32)


def _dot_tn(a, b):
    return lax.dot_general(a, b, (((0,), (0,)), ((), ())), preferred_element_type=F32)


def _inproj_kernel(x_ref, g_ref, w_ref, z_ref, h_ref):
    @pl.when(pl.program_id(1) == 0)
    def _():
        x = x_ref[...]
        ms = jnp.mean(x * x, axis=-1, keepdims=True)
        h_ref[...] = (x * lax.rsqrt(ms + EPS) * g_ref[...]).astype(BF16)

    z_ref[...] = _dot(h_ref[...], w_ref[...]).astype(z_ref.dtype)


def in_proj(x2d, g, w_packed, tm=1024, tn=1536):
    t = x2d.shape[0]
    tm = min(tm, t)
    return pl.pallas_call(
        _inproj_kernel,
        out_shape=jax.ShapeDtypeStruct((t, Z_COLS), BF16),
        grid=(t // tm, Z_COLS // tn),
        in_specs=[
            pl.BlockSpec((tm, D_MODEL), lambda i, j: (i, 0)),
            pl.BlockSpec((1, D_MODEL), lambda i, j: (0, 0)),
            pl.BlockSpec((D_MODEL, tn), lambda i, j: (0, j)),
        ],
        out_specs=pl.BlockSpec((tm, tn), lambda i, j: (i, j)),
        scratch_shapes=[pltpu.VMEM((tm, D_MODEL), BF16)],
        compiler_params=_cparams("parallel", "arbitrary"),
        name="in_proj",
    )(x2d, g, w_packed)


def _gla_kernel(qf_ref, kf_ref, vf_ref, mf_ref, qb_ref, kb_ref, vb_ref, mb_ref, wdec_ref, bdec_ref,
                of_ref, ob_ref, s_ref, g_ref, *, tl):
    c = GLA_CHUNK
    nc = tl // c

    @pl.when(pl.program_id(1) == 0)
    def _():
        s_ref[...] = jnp.zeros_like(s_ref)

    for d, m_ref in enumerate((mf_ref, mb_ref)):
        x = _dot(m_ref[...], wdec_ref[d]) + bdec_ref[d]
        logsig = jnp.minimum(x, 0.0) - jnp.log(1.0 + jnp.exp(-jnp.abs(x)))
        g_ref[d] = logsig * (1.0 / GLA_GATE_NORM)

    row = lax.broadcasted_iota(jnp.int32, (c, c), 0)
    col = lax.broadcasted_iota(jnp.int32, (c, c), 1)
    tri = (row >= col, row <= col)
    lane = lax.broadcasted_iota(jnp.int32, (c, LANES), 1)
    head_mask = (lane < GLA_DK, lane >= GLA_DK)
    dirs = ((qf_ref, kf_ref, vf_ref, of_ref), (qb_ref, kb_ref, vb_ref, ob_ref))

    def one_chunk(d, cc):
        q_ref, k_ref, v_ref, o_ref = dirs[d]
        r0 = cc * c
        g = g_ref[d, pl.ds(r0, c), :]
        g_hi = g.astype(BF16)
        r1 = g - g_hi.astype(F32)
        g_mid = r1.astype(BF16)
        g_lo = (r1 - g_mid.astype(F32)).astype(BF16)
        ones_tri = jnp.where(tri[d], 1.0, 0.0).astype(BF16)
        b = _dot(ones_tri, g_hi) + _dot(ones_tri, g_mid) + _dot(ones_tri, g_lo)
        b_tot = b[c - 1:c, :] if d == 0 else b[0:1, :]
        qc = q_ref[pl.ds(r0, c), :].astype(F32) * (GLA_DK ** -0.5)
        kc = k_ref[pl.ds(r0, c), :].astype(F32)
        q_dec = qc * jnp.exp(b)
        k_inv = kc * jnp.exp(-b)
        k_end = kc * jnp.exp(b_tot - b)
        chunk_decay = jnp.exp(b_tot)
        for p in range(GLA_HEADS // 2):
            sl = slice(p * LANES, (p + 1) * LANES)
            qd_p, ke_p = q_dec[:, sl], k_end[:, sl]
            ki_p = k_inv[:, sl].astype(BF16)
            dec_p = chunk_decay[:, sl]
            for hh in range(2):
                h = 2 * p + hh
                qm = jnp.where(head_mask[hh], qd_p, 0.0).astype(BF16)
                km = jnp.where(head_mask[hh], ke_p, 0.0).astype(BF16)
                scores = jnp.where(tri[d], _dot_nt(qm, ki_p), 0.0).astype(BF16)
                v_h = v_ref[pl.ds(r0, c), h * GLA_DV:(h + 1) * GLA_DV]
                st = s_ref[d, h]
                o_h = _dot(scores, v_h) + _dot_nt(qm, st.astype(BF16))
                s_ref[d, h] = st * dec_p + _dot_tn(v_h, km)
                o_ref[pl.ds(r0, c), h * GLA_DV:(h + 1) * GLA_DV] = o_h.astype(o_ref.dtype)

    for ci in range(nc):
        one_chunk(0, ci)
        one_chunk(1, nc - 1 - ci)


def gla_mixer(z, wdec, bdec, batch, seq, tl=512):
    t = batch * seq
    tl = min(tl, seq)
    nb = seq // tl
    fwd = lambda b, i: b * nb + i
    bwd = lambda b, i: b * nb + (nb - 1 - i)

    def specs(rb):
        return [
            pl.BlockSpec((tl, 256), lambda b, i: (rb(b, i), COL_GQ // 256)),
            pl.BlockSpec((tl, 256), lambda b, i: (rb(b, i), COL_GK // 256)),
            pl.BlockSpec((tl, 512), lambda b, i: (rb(b, i), COL_GV // 512)),
            pl.BlockSpec((tl, LANES), lambda b, i: (rb(b, i), COL_MISC // LANES)),
        ]

    out_sd = jax.ShapeDtypeStruct((t, GLA_HEADS * GLA_DV), BF16)
    return pl.pallas_call(
        functools.partial(_gla_kernel, tl=tl),
        out_shape=(out_sd, out_sd),
        grid=(batch, nb),
        in_specs=specs(fwd) + specs(bwd) + [
            pl.BlockSpec((2, LANES, 256), lambda b, i: (0, 0, 0)),
            pl.BlockSpec((2, 1, 256), lambda b, i: (0, 0, 0)),
        ],
        out_specs=(
            pl.BlockSpec((tl, 512), lambda b, i: (fwd(b, i), 0)),
            pl.BlockSpec((tl, 512), lambda b, i: (bwd(b, i), 0)),
        ),
        scratch_shapes=[
            pltpu.VMEM((2, GLA_HEADS, GLA_DV, LANES), F32),
            pltpu.VMEM((2, tl, 256), F32),
        ],
        compiler_params=_cparams("parallel", "arbitrary"),
        name="gla_mixer",
    )(z, z, z, z, z, z, z, z, wdec, bdec)


HALO = 16
SCAN_TILE = 8


def _lru_kernel(xf_ref, xfp_ref, xfn_ref, xb_ref, xbp_ref, xbn_ref, cw_ref, cb_ref, wa_ref, ba_ref,
                wi_ref, bi_ref, lam_ref, hf_ref, hb_ref, h_sc, ac_sc, hc_sc, cin_sc, carry_sc, *, tl, nb):
    i = pl.program_id(1)
    st = SCAN_TILE
    nt = tl // st
    ng = LRU_WIDTH // LANES

    @pl.when(i == 0)
    def _():
        carry_sc[...] = jnp.zeros_like(carry_sc)

    blocks = ((xf_ref, xfp_ref, xfn_ref, i), (xb_ref, xbp_ref, xbn_ref, nb - 1 - i))

    tile_row = lax.broadcasted_iota(jnp.int32, (nt, LRU_WIDTH), 0)
    cw = cw_ref[...]
    conv_bias = cb_ref[...]

    for d, (x_ref, xp_ref, xn_ref, li) in enumerate(blocks):
        x = x_ref[...].astype(F32)
        for g in range(ng):
            h_sc[d, g] = x[:, g * LANES:(g + 1) * LANES]
        xr = [jnp.concatenate([h_sc[d, g, pl.ds(r, nt, stride=st), :] for g in range(ng)], axis=1)
              for r in range(st)]
        prev = jnp.where(li > 0, xp_ref[...].astype(F32), 0.0)
        nxt = jnp.where(li < nb - 1, xn_ref[...].astype(F32), 0.0)

        def from_prev_tile(a, first):
            return jnp.where(tile_row == 0, first, pltpu.roll(a, 1, 0))

        def from_next_tile(a, last):
            return jnp.where(tile_row == nt - 1, last, pltpu.roll(a, nt - 1, 0))

        neighbours = {-2: from_prev_tile(xr[st - 2], prev[HALO - 2:HALO - 1, :]),
                      -1: from_prev_tile(xr[st - 1], prev[HALO - 1:HALO, :]),
                      st: from_next_tile(xr[0], nxt[0:1, :])}
        at = lambda r: xr[r] if 0 <= r < st else neighbours[r]
        u = jnp.concatenate(
            [cw[0:1, :] * at(r - 2) + cw[1:2, :] * at(r - 1) + cw[2:3, :] * at(r) + cw[3:4, :] * at(r + 1) + conv_bias
             for r in range(st)], axis=0)
        ub = u.astype(BF16)
        r_tanh = jnp.tanh(_dot(ub, wa_ref[d]) + ba_ref[d])
        i_tanh = jnp.tanh(_dot(ub, wi_ref[d]) + bi_ref[d])
        lam = lam_ref[d]
        softplus_neg = jnp.maximum(-lam, 0.0) + jnp.log(1.0 + jnp.exp(-jnp.abs(lam)))
        half_rate = (-0.5 * LRU_C * LOG2_E) * softplus_neg
        a = jnp.exp2(r_tanh * half_rate + half_rate)
        y = 1.0 - a * a
        root = y * lax.rsqrt(jnp.maximum(y, 1e-30))
        hh = (root * u) * (0.5 * i_tanh + 0.5)

        order = range(st) if d == 0 else range(st - 1, -1, -1)
        a_run = h_run = None
        for r in order:
            a_r, x_r = a[r * nt:(r + 1) * nt, :], hh[r * nt:(r + 1) * nt, :]
            if a_run is None:
                a_run, h_run = a_r, x_r
            else:
                h_run = a_r * h_run + x_r
                a_run = a_r * a_run
            ac_sc[d, r] = a_run
            hc_sc[d, r] = h_run

    carry = [carry_sc[0], carry_sc[1]]
    ends = (st - 1, 0)
    for step in range(nt):
        for d in range(2):
            j = step if d == 0 else nt - 1 - step
            cin_sc[d, j:j + 1, :] = carry[d]
            carry[d] = ac_sc[d, ends[d], j:j + 1, :] * carry[d] + hc_sc[d, ends[d], j:j + 1, :]
    carry_sc[0] = carry[0]
    carry_sc[1] = carry[1]

    for d, o_ref in enumerate((hf_ref, hb_ref)):
        cin = cin_sc[d]
        for r in range(st):
            h_r = ac_sc[d, r] * cin + hc_sc[d, r]
            for g in range(ng):
                h_sc[d, g, pl.ds(r, nt, stride=st), :] = h_r[:, g * LANES:(g + 1) * LANES]
        for g in range(ng):
            o_ref[:, g * LANES:(g + 1) * LANES] = h_sc[d, g].astype(o_ref.dtype)


def lru_mixer(z, conv_w, conv_b, wa, ba, wi, bi, lam, batch, seq, tl=512):
    t = batch * seq
    tl = min(tl, seq)
    nb = seq // tl
    hpb = tl // HALO
    last_halo = t // HALO - 1
    fwd = lambda b, i: b * nb + i
    bwd = lambda b, i: b * nb + (nb - 1 - i)
    colx = COL_RX // LRU_WIDTH

    def specs(rb):
        return [
            pl.BlockSpec((tl, LRU_WIDTH), lambda b, i: (rb(b, i), colx)),
            pl.BlockSpec((HALO, LRU_WIDTH), lambda b, i: (jnp.maximum(rb(b, i) * hpb - 1, 0), colx)),
            pl.BlockSpec((HALO, LRU_WIDTH), lambda b, i: (jnp.minimum((rb(b, i) + 1) * hpb, last_halo), colx)),
        ]

    full = lambda shape: pl.BlockSpec(shape, lambda b, i: (0,) * len(shape))
    out_sd = jax.ShapeDtypeStruct((t, LRU_WIDTH), BF16)
    return pl.pallas_call(
        functools.partial(_lru_kernel, tl=tl, nb=nb),
        out_shape=(out_sd, out_sd),
        grid=(batch, nb),
        in_specs=specs(fwd) + specs(bwd) + [
            full((4, LRU_WIDTH)), full((1, LRU_WIDTH)),
            full((2, LRU_WIDTH, LRU_WIDTH)), full((2, 1, LRU_WIDTH)),
            full((2, LRU_WIDTH, LRU_WIDTH)), full((2, 1, LRU_WIDTH)),
            full((2, 1, LRU_WIDTH)),
        ],
        out_specs=(
            pl.BlockSpec((tl, LRU_WIDTH), lambda b, i: (fwd(b, i), 0)),
            pl.BlockSpec((tl, LRU_WIDTH), lambda b, i: (bwd(b, i), 0)),
        ),
        scratch_shapes=[
            pltpu.VMEM((2, LRU_WIDTH // LANES, tl, LANES), F32),
            pltpu.VMEM((2, SCAN_TILE, tl // SCAN_TILE, LRU_WIDTH), F32),
            pltpu.VMEM((2, SCAN_TILE, tl // SCAN_TILE, LRU_WIDTH), F32),
            pltpu.VMEM((2, tl // SCAN_TILE, LRU_WIDTH), F32),
            pltpu.VMEM((2, 1, LRU_WIDTH), F32),
        ],
        compiler_params=_cparams("parallel", "arbitrary"),
        name="lru_mixer",
    )(z, z, z, z, z, z, conv_w, conv_b, wa, ba, wi, bi, lam)


def _rope(x, cos_t, sin_a, sin_b):
    return x * cos_t + pltpu.roll(x, LANES - MLA_ROPE // 2, 1) * sin_a + pltpu.roll(x, MLA_ROPE // 2, 1) * sin_b


def _mla_prep_kernel(qa_ref, kva_ref, misc_ref, cos_ref, sa_ref, sb_ref, qag_ref, wqb_ref, kvag_ref, wkvb_ref,
                     qng_ref, kng_ref, q_out, k_out, v_out):
    cos_t, sin_a, sin_b = cos_ref[...], sa_ref[...], sb_ref[...]
    scale = LOG2_E * MLA_QK ** -0.5

    qa = qa_ref[...].astype(F32)
    qa_n = qa * lax.rsqrt(jnp.mean(qa * qa, axis=-1, keepdims=True) + EPS) * qag_ref[...]
    q = _dot(qa_n.astype(BF16), wqb_ref[...])
    qng = qng_ref[...]
    for h in range(MLA_HEADS):
        qh = q[:, h * MLA_QK_PAD:(h + 1) * MLA_QK_PAD]
        rstd = lax.rsqrt(jnp.sum(qh * qh, axis=-1, keepdims=True) * (1.0 / MLA_QK) + EPS)
        qh = qh * rstd * qng
        q_out[:, h * MLA_QK_PAD:h * MLA_QK_PAD + LANES] = (qh[:, :LANES] * scale).astype(q_out.dtype)
        q_out[:, h * MLA_QK_PAD + LANES:(h + 1) * MLA_QK_PAD] = (
            _rope(qh[:, LANES:], cos_t, sin_a, sin_b) * scale).astype(q_out.dtype)

    kva = kva_ref[...].astype(F32)
    kva_n = kva * lax.rsqrt(jnp.mean(kva * kva, axis=-1, keepdims=True) + EPS) * kvag_ref[...]
    kv = _dot(kva_n.astype(BF16), wkvb_ref[...])
    lane = lax.broadcasted_iota(jnp.int32, misc_ref.shape, 1)
    kr = jnp.where(lane < MLA_ROPE, misc_ref[...].astype(F32), 0.0)
    kr_ss = jnp.sum(kr * kr, axis=-1, keepdims=True)
    kng = kng_ref[...]
    kr_rot = _rope(kr * kng[:, LANES:], cos_t, sin_a, sin_b)
    for h in range(MLA_HEADS):
        k_nope = kv[:, h * 256:h * 256 + LANES]
        rstd = lax.rsqrt((jnp.sum(k_nope * k_nope, axis=-1, keepdims=True) + kr_ss) * (1.0 / MLA_QK) + EPS)
        k_out[:, h * MLA_QK_PAD:h * MLA_QK_PAD + LANES] = (k_nope * rstd * kng[:, :LANES]).astype(k_out.dtype)
        k_out[:, h * MLA_QK_PAD + LANES:(h + 1) * MLA_QK_PAD] = (kr_rot * rstd).astype(k_out.dtype)
        v_out[:, h * MLA_V:(h + 1) * MLA_V] = kv[:, h * 256 + LANES:(h + 1) * 256].astype(v_out.dtype)


def mla_prep(z, cos_t, sin_a, sin_b, qag, wqb, kvag, wkvb, qng, kng, batch, seq, tm=512):
    t = batch * seq
    tm = min(tm, seq)
    nb = seq // tm
    full = lambda shape: pl.BlockSpec(shape, lambda i: (0,) * len(shape))
    tab = pl.BlockSpec((tm, LANES), lambda i: (i % nb, 0))
    return pl.pallas_call(
        _mla_prep_kernel,
        out_shape=(
            jax.ShapeDtypeStruct((t, MLA_HEADS * MLA_QK_PAD), BF16),
            jax.ShapeDtypeStruct((t, MLA_HEADS * MLA_QK_PAD), BF16),
            jax.ShapeDtypeStruct((t, MLA_HEADS * MLA_V), BF16),
        ),
        grid=(t // tm,),
        in_specs=[
            pl.BlockSpec((tm, MLA_Q_RANK), lambda i: (i, COL_MQA // MLA_Q_RANK)),
            pl.BlockSpec((tm, MLA_KV_RANK), lambda i: (i, COL_MKVA // MLA_KV_RANK)),
            pl.BlockSpec((tm, LANES), lambda i: (i, COL_MISC // LANES)),
            tab, tab, tab,
            full((1, MLA_Q_RANK)), full((MLA_Q_RANK, MLA_HEADS * MLA_QK_PAD)),
            full((1, MLA_KV_RANK)), full((MLA_KV_RANK, MLA_HEADS * 256)),
            full((1, MLA_QK_PAD)), full((1, MLA_QK_PAD)),
        ],
        out_specs=(
            pl.BlockSpec((tm, MLA_HEADS * MLA_QK_PAD), lambda i: (i, 0)),
            pl.BlockSpec((tm, MLA_HEADS * MLA_QK_PAD), lambda i: (i, 0)),
            pl.BlockSpec((tm, MLA_HEADS * MLA_V), lambda i: (i, 0)),
        ),
        compiler_params=_cparams("parallel"),
        name="mla_prep",
    )(z, z, z, cos_t, sin_a, sin_b, qag, wqb, kvag, wkvb, qng, kng)


def _attn_kernel(q_ref, k_ref, v_ref, o_ref, vp_sc, m_sc, acc_sc, s_sc, *, tk):
    nk = k_ref.shape[0] // tk

    @pl.when(pl.program_id(2) == 0)
    def _():
        vp_sc[:, :MLA_V] = v_ref[...]
        vp_sc[:, MLA_V:] = jnp.ones((vp_sc.shape[0], MLA_V), vp_sc.dtype)

    q = q_ref[...]
    m_sc[...] = jnp.full_like(m_sc, -jnp.inf)
    acc_sc[...] = jnp.zeros_like(acc_sc)

    def scores(j):
        return _dot_nt(q, k_ref[j * tk:(j + 1) * tk, :])

    def consume(slot, j):
        s = s_sc[slot]
        m_prev = m_sc[...]
        m_new = jnp.maximum(m_prev, jnp.max(s, axis=-1, keepdims=True))
        p = jnp.exp2(s - jnp.concatenate([m_new] * (tk // LANES), axis=1))
        alpha = jnp.exp2(m_prev - m_new)
        pv = _dot(p.astype(BF16), vp_sc[j * tk:(j + 1) * tk, :])
        acc_sc[...] = jnp.concatenate([alpha, alpha], axis=1) * acc_sc[...] + pv
        m_sc[...] = m_new

    s_sc[0] = scores(0)
    for j in range(nk):
        if j + 1 < nk:
            s_sc[(j + 1) % 2] = scores(j + 1)
        consume(j % 2, j)
    acc = acc_sc[...]
    o_ref[...] = (acc[:, :MLA_V] / acc[:, MLA_V:]).astype(o_ref.dtype)


def attention(q, k, v, batch, seq, tq=512, tk=512):
    t = batch * seq
    tq = min(tq, seq)
    tk = min(tk, seq // 2)
    nq = seq // tq
    return pl.pallas_call(
        functools.partial(_attn_kernel, tk=tk),
        out_shape=jax.ShapeDtypeStruct((t, MLA_HEADS * MLA_V), BF16),
        grid=(batch, MLA_HEADS, nq),
        in_specs=[
            pl.BlockSpec((tq, MLA_QK_PAD), lambda b, h, i: (b * nq + i, h)),
            pl.BlockSpec((seq, MLA_QK_PAD), lambda b, h, i: (b, h)),
            pl.BlockSpec((seq, MLA_V), lambda b, h, i: (b, h)),
        ],
        out_specs=pl.BlockSpec((tq, MLA_V), lambda b, h, i: (b * nq + i, h)),
        scratch_shapes=[
            pltpu.VMEM((seq, 2 * MLA_V), BF16),
            pltpu.VMEM((tq, LANES), F32),
            pltpu.VMEM((tq, 2 * MLA_V), F32),
            pltpu.VMEM((2, tq, tk), F32),
        ],
        compiler_params=_cparams("parallel", "parallel", "arbitrary"),
        name="mla_attention",
    )(q, k, v)


def _split3(x):
    hi = x.astype(BF16)
    r1 = x - hi.astype(F32)
    mid = r1.astype(BF16)
    lo = (r1 - mid.astype(F32)).astype(BF16)
    return hi, mid, lo


def _merge_kernel(*refs, with_router):
    (x_ref, gates_ref, og_ref, rg_ref, gf_ref, gb_ref, lf_ref, lb_ref, am_ref,
     gng_ref, wga_ref, wlr_ref, wml_ref, wo_ref, fng_ref) = refs[:15]
    if with_router:
        wr_ref, br_ref, x_out, h_out, comb_out = refs[15:]
    else:
        x_out, h_out = refs[15:]

    o = gf_ref[...].astype(F32) + gb_ref[...].astype(F32)
    gng = gng_ref[...]
    parts = []
    for h in range(GLA_HEADS):
        oh = o[:, h * GLA_DV:(h + 1) * GLA_DV]
        rstd = lax.rsqrt(jnp.mean(oh * oh, axis=-1, keepdims=True) + EPS)
        parts.append(oh * rstd * gng[:, h * GLA_DV:(h + 1) * GLA_DV])
    og = og_ref[...].astype(F32)
    ya_in = jnp.concatenate(parts, axis=-1) * (og * _sigmoid(og))
    y_a = _dot(ya_in.astype(BF16), wga_ref[...])

    rg = rg_ref[...].astype(F32)
    gelu = 0.5 * rg * (1.0 + jnp.tanh(0.7978845608028654 * (rg + 0.044715 * rg * rg * rg)))
    yb_in = (lf_ref[...].astype(F32) + lb_ref[...].astype(F32)) * gelu
    y_b = _dot(yb_in.astype(BF16), wlr_ref[...])

    y_c = _dot(am_ref[...], wml_ref[...])

    d = D_MODEL
    merged = (_sigmoid(gates_ref[:, 0:d].astype(F32)) * y_a
              + _sigmoid(gates_ref[:, d:2 * d].astype(F32)) * y_b
              + _sigmoid(gates_ref[:, 2 * d:3 * d].astype(F32)) * y_c)
    x_new = x_ref[...] + _dot(merged.astype(BF16), wo_ref[...])
    x_out[...] = x_new
    h = x_new * lax.rsqrt(jnp.mean(x_new * x_new, axis=-1, keepdims=True) + EPS) * fng_ref[...]
    h_out[...] = h.astype(h_out.dtype)

    if with_router:
        h_hi, h_mid, h_lo = _split3(h)
        w_hi, w_mid, w_lo = _split3(wr_ref[...])
        logits = (_dot(h_hi, w_hi) + (_dot(h_hi, w_mid) + _dot(h_mid, w_hi))
                  + (_dot(h_hi, w_lo) + _dot(h_mid, w_mid) + _dot(h_lo, w_hi))) + br_ref[...]
        lane = lax.broadcasted_iota(jnp.int32, logits.shape, 1)
        logits = jnp.where(lane < N_EXPERTS, logits, -jnp.inf)
        v1 = jnp.max(logits, axis=-1, keepdims=True)
        i1 = jnp.min(jnp.where(logits == v1, lane, LANES), axis=-1, keepdims=True)
        rest = jnp.where(lane == i1, -jnp.inf, logits)
        v2 = jnp.max(rest, axis=-1, keepdims=True)
        i2 = jnp.min(jnp.where(rest == v2, lane, LANES), axis=-1, keepdims=True)
        e2 = jnp.exp(v2 - v1)
        w1 = 1.0 / (1.0 + e2)
        comb_out[...] = jnp.where(lane == i1, w1, 0.0) + jnp.where(lane == i2, e2 * w1, 0.0)


def merge(x2d, z, gla_f, gla_b, lru_f, lru_b, attn, gng, wga, wlr, wml, wo, fng, router=None, tm=512):
    t = x2d.shape[0]
    tm = min(tm, t)
    full = lambda shape: pl.BlockSpec(shape, lambda i: (0,) * len(shape))
    row512 = pl.BlockSpec((tm, 512), lambda i: (i, 0))
    in_specs = [
        pl.BlockSpec((tm, D_MODEL), lambda i: (i, 0)),
        pl.BlockSpec((tm, 3 * D_MODEL), lambda i: (i, COL_GATES // (3 * D_MODEL))),
        pl.BlockSpec((tm, 512), lambda i: (i, COL_GOG // 512)),
        pl.BlockSpec((tm, 512), lambda i: (i, COL_RG // 512)),
        row512, row512, row512, row512, row512,
        full((1, 512)), full((512, D_MODEL)), full((512, D_MODEL)), full((512, D_MODEL)),
        full((D_MODEL, D_MODEL)), full((1, D_MODEL)),
    ]
    args = [x2d, z, z, z, gla_f, gla_b, lru_f, lru_b, attn, gng, wga, wlr, wml, wo, fng]
    out_shape = [jax.ShapeDtypeStruct((t, D_MODEL), F32), jax.ShapeDtypeStruct((t, D_MODEL), BF16)]
    out_specs = [pl.BlockSpec((tm, D_MODEL), lambda i: (i, 0)), pl.BlockSpec((tm, D_MODEL), lambda i: (i, 0))]
    if router is not None:
        in_specs += [full((D_MODEL, LANES)), full((1, LANES))]
        args += list(router)
        out_shape.append(jax.ShapeDtypeStruct((t, LANES), F32))
        out_specs.append(pl.BlockSpec((tm, LANES), lambda i: (i, 0)))
    return pl.pallas_call(
        functools.partial(_merge_kernel, with_router=router is not None),
        out_shape=tuple(out_shape),
        grid=(t // tm,),
        in_specs=in_specs,
        out_specs=tuple(out_specs),
        compiler_params=_cparams("parallel"),
        name="merge_router" if router is not None else "merge",
    )(*args)


def _ffn_kernel(h_ref, x_ref, w1_ref, w3_ref, w2_ref, o_ref):
    h = h_ref[...]
    a = _dot(h, w1_ref[...])
    y = _dot((a * _sigmoid(a) * _dot(h, w3_ref[...])).astype(BF16), w2_ref[...])

    @pl.when(pl.program_id(1) == 0)
    def _():
        o_ref[...] = x_ref[...] + y

    @pl.when(pl.program_id(1) > 0)
    def _():
        o_ref[...] += y


def ffn(h, x2d, w1, w3, w2, tm=512, tf=1408):
    t = x2d.shape[0]
    tm = min(tm, t)
    return pl.pallas_call(
        _ffn_kernel,
        out_shape=jax.ShapeDtypeStruct((t, D_MODEL), F32),
        grid=(t // tm, D_FF // tf),
        in_specs=[
            pl.BlockSpec((tm, D_MODEL), lambda i, j: (i, 0)),
            pl.BlockSpec((tm, D_MODEL), lambda i, j: (i, 0)),
            pl.BlockSpec((D_MODEL, tf), lambda i, j: (0, j)),
            pl.BlockSpec((D_MODEL, tf), lambda i, j: (0, j)),
            pl.BlockSpec((tf, D_MODEL), lambda i, j: (j, 0)),
        ],
        out_specs=pl.BlockSpec((tm, D_MODEL), lambda i, j: (i, 0)),
        compiler_params=_cparams("parallel", "arbitrary"),
        name="ffn",
    )(h, x2d, w1, w3, w2)


def _moe_kernel(h_ref, x_ref, comb_ref, w1_ref, w3_ref, w2_ref, o_ref):
    e = pl.program_id(1)
    h = h_ref[...]
    a = _dot(h, w1_ref[...])
    y = _dot((a * _sigmoid(a) * _dot(h, w3_ref[...])).astype(BF16), w2_ref[...])
    comb = comb_ref[...]
    lane = lax.broadcasted_iota(jnp.int32, comb.shape, 1)
    c = jnp.sum(jnp.where(lane == e, comb, 0.0), axis=-1, keepdims=True)

    @pl.when(e == 0)
    def _():
        o_ref[...] = x_ref[...] + c * y

    @pl.when(e > 0)
    def _():
        o_ref[...] += c * y


def moe(h, x2d, comb, w1, w3, w2, tm=512):
    t = x2d.shape[0]
    tm = min(tm, t)
    return pl.pallas_call(
        _moe_kernel,
        out_shape=jax.ShapeDtypeStruct((t, D_MODEL), F32),
        grid=(t // tm, N_EXPERTS),
        in_specs=[
            pl.BlockSpec((tm, D_MODEL), lambda i, e: (i, 0)),
            pl.BlockSpec((tm, D_MODEL), lambda i, e: (i, 0)),
            pl.BlockSpec((tm, LANES), lambda i, e: (i, 0)),
            pl.BlockSpec((None, D_MODEL, EXPERT_FF), lambda i, e: (e, 0, 0)),
            pl.BlockSpec((None, D_MODEL, EXPERT_FF), lambda i, e: (e, 0, 0)),
            pl.BlockSpec((None, EXPERT_FF, D_MODEL), lambda i, e: (e, 0, 0)),
        ],
        out_specs=pl.BlockSpec((tm, D_MODEL), lambda i, e: (i, 0)),
        compiler_params=_cparams("parallel", "arbitrary"),
        name="moe",
    )(h, x2d, comb, w1, w3, w2)


def _pack_layer(layer, p):
    w_in = p["w_in"][layer]
    sizes = (256, 256, 512, 512, 32, 512, 512, 256, 128, 64, 3072)
    offs = [0]
    for s in sizes:
        offs.append(offs[-1] + s)
    seg = lambda n: w_in[:, offs[n]:offs[n + 1]]
    g_q, g_k, g_v, g_og, g_dec, r_x, r_gate, m_qa, m_kva, m_kr, gates = (seg(n) for n in range(11))
    w_packed = jnp.concatenate(
        [gates, g_v, g_og, r_x, r_gate, g_q, g_k, m_qa, m_kva, m_kr, g_dec, jnp.zeros((D_MODEL, 32), F32)],
        axis=1).astype(BF16)

    wdec = jnp.zeros((2, LANES, 256), F32)
    for d in range(2):
        lo = MLA_ROPE + d * GLA_LOWRANK
        wdec = wdec.at[d, lo:lo + GLA_LOWRANK, :].set(p["gla_w_dec"][layer, d])
    bdec = p["gla_b_dec"][layer].reshape(2, 1, 256)

    def block_diag(w):
        out = jnp.zeros((2, LRU_WIDTH, LRU_WIDTH), F32)
        for n in range(LRU_BLOCKS):
            sl = slice(n * LRU_BLOCK, (n + 1) * LRU_BLOCK)
            out = out.at[:, sl, sl].set(w[:, n])
        return out.astype(BF16)

    wqb = p["mla_w_qb"][layer].reshape(MLA_Q_RANK, MLA_HEADS, MLA_QK)
    wqb = jnp.pad(wqb, ((0, 0), (0, 0), (0, MLA_QK_PAD - MLA_QK))).reshape(MLA_Q_RANK, MLA_HEADS * MLA_QK_PAD)
    pad_qk = lambda g: jnp.pad(g, (0, MLA_QK_PAD - MLA_QK)).reshape(1, MLA_QK_PAD)
    return dict(
        w_in=w_packed,
        norm_mix_g=p["norm_mix_g"][layer].reshape(1, D_MODEL),
        wdec=wdec.astype(BF16), bdec=bdec,
        conv_w=p["lru_conv_w"][layer], conv_b=p["lru_conv_b"][layer].reshape(1, LRU_WIDTH),
        wa=block_diag(0.5 * p["lru_w_a"][layer]), ba=0.5 * p["lru_b_a"][layer].reshape(2, 1, LRU_WIDTH),
        wi=block_diag(0.5 * p["lru_w_i"][layer]), bi=0.5 * p["lru_b_i"][layer].reshape(2, 1, LRU_WIDTH),
        lam=p["lru_lambda"][layer].reshape(2, 1, LRU_WIDTH),
        qag=p["mla_qa_g"][layer].reshape(1, MLA_Q_RANK), wqb=wqb.astype(BF16),
        kvag=p["mla_kva_g"][layer].reshape(1, MLA_KV_RANK), wkvb=p["mla_w_kvb"][layer].astype(BF16),
        qng=pad_qk(p["mla_qn_g"][layer]), kng=pad_qk(p["mla_kn_g"][layer]),
        gng=p["gla_norm_g"][layer].reshape(1, GLA_HEADS * GLA_DV),
        wga=p["gla_w_out"][layer].astype(BF16), wlr=p["lru_w_out"][layer].astype(BF16),
        wml=p["mla_w_out"][layer].astype(BF16), wo=p["w_o"][layer].astype(BF16),
        fng=p["norm_ffn_g"][layer].reshape(1, D_MODEL),
    )


def _rope_tables(length):
    pos = jnp.arange(length, dtype=F32)
    inv = ROPE_THETA ** (-jnp.arange(0, MLA_ROPE, 2, dtype=F32) / MLA_ROPE)
    ang = pos[:, None] * inv[None, :]
    cos, sin = jnp.cos(ang), jnp.sin(ang)
    zero = jnp.zeros_like(cos)
    cos_t = jnp.concatenate([cos, cos, zero, zero], axis=1)
    sin_a = jnp.concatenate([-sin, zero, zero, zero], axis=1)
    sin_b = jnp.concatenate([zero, sin, zero, zero], axis=1)
    return cos_t, sin_a, sin_b


def _trunk(x, layers, ffn_params, moe_params):
    batch, seq, _ = x.shape
    x2d = x.reshape(batch * seq, D_MODEL)
    tables = _rope_tables(seq)
    for layer, lp in enumerate(layers):
        z = in_proj(x2d, lp["norm_mix_g"], lp["w_in"])
        gla_f, gla_b = gla_mixer(z, lp["wdec"], lp["bdec"], batch, seq)
        lru_f, lru_b = lru_mixer(z, lp["conv_w"], lp["conv_b"], lp["wa"], lp["ba"], lp["wi"], lp["bi"], lp["lam"],
                                 batch, seq)
        q, k, v = mla_prep(z, *tables, lp["qag"], lp["wqb"], lp["kvag"], lp["wkvb"], lp["qng"], lp["kng"], batch, seq)
        attn = attention(q, k, v, batch, seq)
        j = layer // 2
        branch = (gla_f, gla_b, lru_f, lru_b, attn, lp["gng"], lp["wga"], lp["wlr"], lp["wml"], lp["wo"], lp["fng"])
        if layer % 2 == 0:
            x2d, h = merge(x2d, z, *branch)
            w1, w3, w2 = ffn_params[j]
            x2d = ffn(h, x2d, w1, w3, w2)
        else:
            x2d, h, comb = merge(x2d, z, *branch, router=moe_params[j][:2])
            _, _, w1, w3, w2 = moe_params[j]
            x2d = moe(h, x2d, comb, w1, w3, w2)
    return x2d.reshape(batch, seq, D_MODEL)


def kernel(x_prompt, x_sample, norm_mix_g, w_in, gla_w_dec, gla_b_dec, gla_norm_g, gla_w_out, lru_conv_w, lru_conv_b, lru_w_a, lru_b_a, lru_w_i, lru_b_i, lru_lambda, lru_w_out, mla_qa_g, mla_w_qb, mla_kva_g, mla_w_kvb, mla_qn_g, mla_kn_g, mla_w_out, w_o, norm_ffn_g, ffn_w1, ffn_w3, ffn_w2, moe_w_router, moe_b_router, moe_w1, moe_w3, moe_w2):
    p = dict(norm_mix_g=norm_mix_g, w_in=w_in, gla_w_dec=gla_w_dec, gla_b_dec=gla_b_dec, gla_norm_g=gla_norm_g,
             gla_w_out=gla_w_out, lru_conv_w=lru_conv_w, lru_conv_b=lru_conv_b, lru_w_a=lru_w_a, lru_b_a=lru_b_a,
             lru_w_i=lru_w_i, lru_b_i=lru_b_i, lru_lambda=lru_lambda, lru_w_out=lru_w_out, mla_qa_g=mla_qa_g,
             mla_w_qb=mla_w_qb, mla_kva_g=mla_kva_g, mla_w_kvb=mla_w_kvb, mla_qn_g=mla_qn_g, mla_kn_g=mla_kn_g,
             mla_w_out=mla_w_out, w_o=w_o, norm_ffn_g=norm_ffn_g)
    depth = w_in.shape[0]
    layers = [_pack_layer(layer, p) for layer in range(depth)]
    ffn_params = [(ffn_w1[j].astype(BF16), ffn_w3[j].astype(BF16), ffn_w2[j].astype(BF16))
                  for j in range(ffn_w1.shape[0])]
    moe_params = []
    for j in range(moe_w1.shape[0]):
        w_router = jnp.pad(moe_w_router[j], ((0, 0), (0, LANES - N_EXPERTS)))
        b_router = jnp.pad(moe_b_router[j], (0, LANES - N_EXPERTS)).reshape(1, LANES)
        moe_params.append((w_router, b_router, moe_w1[j].astype(BF16), moe_w3[j].astype(BF16),
                           moe_w2[j].astype(BF16)))
    y_prompt = _trunk(x_prompt, layers, ffn_params, moe_params)
    y_sample = _trunk(x_sample, layers, ffn_params, moe_params)
    return (y_prompt, y_sample)
```

```python
import functools

import jax
import jax.numpy as jnp
from jax import lax
from jax.experimental import pallas as pl
from jax.experimental.pallas import tpu as pltpu

F32 = jnp.float32
BF16 = jnp.bfloat16

D_MODEL = 1024
EPS = 1e-6
GLA_HEADS = 4
GLA_DK = 64
GLA_DV = 128
GLA_LOWRANK = 16
GLA_GATE_NORM = 16.0
GLA_CHUNK = 64
LRU_WIDTH = 512
LRU_BLOCKS = 8
LRU_BLOCK = LRU_WIDTH // LRU_BLOCKS
LRU_C = 8.0
MLA_HEADS = 4
MLA_Q_RANK = 256
MLA_KV_RANK = 128
MLA_NOPE = 128
MLA_ROPE = 64
MLA_V = 128
MLA_QK = MLA_NOPE + MLA_ROPE
MLA_QK_PAD = 256
ROPE_THETA = 10000.0
D_FF = 2816
N_EXPERTS = 8
EXPERT_FF = 1408
LANES = 128
LOG2_E = 1.4426950408889634

COL_GATES = 0
COL_GV = 3072
COL_GOG = 3584
COL_RX = 4096
COL_RG = 4608
COL_GQ = 5120
COL_GK = 5376
COL_MQA = 5632
COL_MKVA = 5888
COL_MISC = 6016
Z_COLS = 6144

ROUTE_TOP1 = 8
ROUTE_TOP2 = 16

VMEM_LIMIT = 56 * 1024 * 1024
MOE_VMEM_LIMIT = 62 * 1024 * 1024


def _cparams(*sem):
    return pltpu.CompilerParams(dimension_semantics=sem, vmem_limit_bytes=VMEM_LIMIT)


def _sigmoid(x):
    return 1.0 / (1.0 + jnp.exp(-x))


def _dot(a, b):
    return jnp.dot(a, b, preferred_element_type=F32)


def _dot_nt(a, b):
    return lax.dot_general(a, b, (((1,), (1,)), ((), ())), preferred_element_type=F32)


def _dot_tn(a, b):
    return lax.dot_general(a, b, (((0,), (0,)), ((), ())), preferred_element_type=F32)


def _inproj_kernel(x_ref, g_ref, w_ref, z_ref, h_ref):
    @pl.when(pl.program_id(1) == 0)
    def _():
        x = x_ref[...]
        ms = jnp.mean(x * x, axis=-1, keepdims=True)
        h_ref[...] = (x * lax.rsqrt(ms + EPS) * g_ref[...]).astype(BF16)

    z_ref[...] = _dot(h_ref[...], w_ref[...]).astype(z_ref.dtype)


def in_proj(x2d, g, w_packed, tm=1024, tn=1536):
    t = x2d.shape[0]
    tm = min(tm, t)
    return pl.pallas_call(
        _inproj_kernel,
        out_shape=jax.ShapeDtypeStruct((t, Z_COLS), BF16),
        grid=(t // tm, Z_COLS // tn),
        in_specs=[
            pl.BlockSpec((tm, D_MODEL), lambda i, j: (i, 0)),
            pl.BlockSpec((1, D_MODEL), lambda i, j: (0, 0)),
            pl.BlockSpec((D_MODEL, tn), lambda i, j: (0, j)),
        ],
        out_specs=pl.BlockSpec((tm, tn), lambda i, j: (i, j)),
        scratch_shapes=[pltpu.VMEM((tm, D_MODEL), BF16)],
        compiler_params=_cparams("parallel", "arbitrary"),
        name="in_proj",
    )(x2d, g, w_packed)


def _gla_kernel(qf_ref, kf_ref, vf_ref, mf_ref, qb_ref, kb_ref, vb_ref, mb_ref, wdec_ref, bdec_ref,
                of_ref, ob_ref, s_ref, g_ref, *, tl):
    c = GLA_CHUNK
    nc = tl // c

    @pl.when(pl.program_id(1) == 0)
    def _():
        s_ref[...] = jnp.zeros_like(s_ref)

    for d, m_ref in enumerate((mf_ref, mb_ref)):
        x = _dot(m_ref[...], wdec_ref[d]) + bdec_ref[d]
        logsig = jnp.minimum(x, 0.0) - jnp.log(1.0 + jnp.exp(-jnp.abs(x)))
        g_ref[d] = logsig * (1.0 / GLA_GATE_NORM)

    row = lax.broadcasted_iota(jnp.int32, (c, c), 0)
    col = lax.broadcasted_iota(jnp.int32, (c, c), 1)
    tri = (row >= col, row <= col)
    lane = lax.broadcasted_iota(jnp.int32, (c, LANES), 1)
    head_mask = (lane < GLA_DK, lane >= GLA_DK)
    dirs = ((qf_ref, kf_ref, vf_ref, of_ref), (qb_ref, kb_ref, vb_ref, ob_ref))

    def one_chunk(d, cc):
        q_ref, k_ref, v_ref, o_ref = dirs[d]
        r0 = cc * c
        g = g_ref[d, pl.ds(r0, c), :]
        g_hi = g.astype(BF16)
        r1 = g - g_hi.astype(F32)
        g_mid = r1.astype(BF16)
        g_lo = (r1 - g_mid.astype(F32)).astype(BF16)
        ones_tri = jnp.where(tri[d], 1.0, 0.0).astype(BF16)
        b = _dot(ones_tri, g_hi) + _dot(ones_tri, g_mid) + _dot(ones_tri, g_lo)
        b_tot = b[c - 1:c, :] if d == 0 else b[0:1, :]
        qc = q_ref[pl.ds(r0, c), :].astype(F32) * (GLA_DK ** -0.5)
        kc = k_ref[pl.ds(r0, c), :].astype(F32)
        q_dec = qc * jnp.exp(b)
        k_inv = kc * jnp.exp(-b)
        k_end = kc * jnp.exp(b_tot - b)
        chunk_decay = jnp.exp(b_tot)
        for p in range(GLA_HEADS // 2):
            sl = slice(p * LANES, (p + 1) * LANES)
            qd_p, ke_p = q_dec[:, sl], k_end[:, sl]
            ki_p = k_inv[:, sl].astype(BF16)
            dec_p = chunk_decay[:, sl]
            for hh in range(2):
                h = 2 * p + hh
                qm = jnp.where(head_mask[hh], qd_p, 0.0).astype(BF16)
                km = jnp.where(head_mask[hh], ke_p, 0.0).astype(BF16)
                scores = jnp.where(tri[d], _dot_nt(qm, ki_p), 0.0).astype(BF16)
                v_h = v_ref[pl.ds(r0, c), h * GLA_DV:(h + 1) * GLA_DV]
                st = s_ref[d, h]
                o_h = _dot(scores, v_h) + _dot_nt(qm, st.astype(BF16))
                s_ref[d, h] = st * dec_p + _dot_tn(v_h, km)
                o_ref[pl.ds(r0, c), h * GLA_DV:(h + 1) * GLA_DV] = o_h.astype(o_ref.dtype)

    for ci in range(nc):
        one_chunk(0, ci)
        one_chunk(1, nc - 1 - ci)


def gla_mixer(z, wdec, bdec, batch, seq, tl=512):
    t = batch * seq
    tl = min(tl, seq)
    nb = seq // tl
    fwd = lambda b, i: b * nb + i
    bwd = lambda b, i: b * nb + (nb - 1 - i)

    def specs(rb):
        return [
            pl.BlockSpec((tl, 256), lambda b, i: (rb(b, i), COL_GQ // 256)),
            pl.BlockSpec((tl, 256), lambda b, i: (rb(b, i), COL_GK // 256)),
            pl.BlockSpec((tl, 512), lambda b, i: (rb(b, i), COL_GV // 512)),
            pl.BlockSpec((tl, LANES), lambda b, i: (rb(b, i), COL_MISC // LANES)),
        ]

    out_sd = jax.ShapeDtypeStruct((t, GLA_HEADS * GLA_DV), BF16)
    return pl.pallas_call(
        functools.partial(_gla_kernel, tl=tl),
        out_shape=(out_sd, out_sd),
        grid=(batch, nb),
        in_specs=specs(fwd) + specs(bwd) + [
            pl.BlockSpec((2, LANES, 256), lambda b, i: (0, 0, 0)),
            pl.BlockSpec((2, 1, 256), lambda b, i: (0, 0, 0)),
        ],
        out_specs=(
            pl.BlockSpec((tl, 512), lambda b, i: (fwd(b, i), 0)),
            pl.BlockSpec((tl, 512), lambda b, i: (bwd(b, i), 0)),
        ),
        scratch_shapes=[
            pltpu.VMEM((2, GLA_HEADS, GLA_DV, LANES), F32),
            pltpu.VMEM((2, tl, 256), F32),
        ],
        compiler_params=_cparams("parallel", "arbitrary"),
        name="gla_mixer",
    )(z, z, z, z, z, z, z, z, wdec, bdec)


HALO = 16
SCAN_TILE = 8


def _lru_kernel(xf_ref, xfp_ref, xfn_ref, xb_ref, xbp_ref, xbn_ref, cw_ref, cb_ref, wa_ref, ba_ref,
                wi_ref, bi_ref, lam_ref, hf_ref, hb_ref, h_sc, ac_sc, hc_sc, cin_sc, carry_sc, *, tl, nb):
    i = pl.program_id(1)
    st = SCAN_TILE
    nt = tl // st
    ng = LRU_WIDTH // LANES

    @pl.when(i == 0)
    def _():
        carry_sc[...] = jnp.zeros_like(carry_sc)

    blocks = ((xf_ref, xfp_ref, xfn_ref, i), (xb_ref, xbp_ref, xbn_ref, nb - 1 - i))

    tile_row = lax.broadcasted_iota(jnp.int32, (nt, LRU_WIDTH), 0)
    cw = cw_ref[...]
    conv_bias = cb_ref[...]

    for d, (x_ref, xp_ref, xn_ref, li) in enumerate(blocks):
        x = x_ref[...].astype(F32)
        for g in range(ng):
            h_sc[d, g] = x[:, g * LANES:(g + 1) * LANES]
        xr = [jnp.concatenate([h_sc[d, g, pl.ds(r, nt, stride=st), :] for g in range(ng)], axis=1)
              for r in range(st)]
        prev = jnp.where(li > 0, xp_ref[...].astype(F32), 0.0)
        nxt = jnp.where(li < nb - 1, xn_ref[...].astype(F32), 0.0)

        def from_prev_tile(a, first):
            return jnp.where(tile_row == 0, first, pltpu.roll(a, 1, 0))

        def from_next_tile(a, last):
            return jnp.where(tile_row == nt - 1, last, pltpu.roll(a, nt - 1, 0))

        neighbours = {-2: from_prev_tile(xr[st - 2], prev[HALO - 2:HALO - 1, :]),
                      -1: from_prev_tile(xr[st - 1], prev[HALO - 1:HALO, :]),
                      st: from_next_tile(xr[0], nxt[0:1, :])}
        at = lambda r: xr[r] if 0 <= r < st else neighbours[r]
        u = jnp.concatenate(
            [cw[0:1, :] * at(r - 2) + cw[1:2, :] * at(r - 1) + cw[2:3, :] * at(r) + cw[3:4, :] * at(r + 1) + conv_bias
             for r in range(st)], axis=0)
        ub = u.astype(BF16)
        r_tanh = jnp.tanh(_dot(ub, wa_ref[d]) + ba_ref[d])
        i_tanh = jnp.tanh(_dot(ub, wi_ref[d]) + bi_ref[d])
        lam = lam_ref[d]
        softplus_neg = jnp.maximum(-lam, 0.0) + jnp.log(1.0 + jnp.exp(-jnp.abs(lam)))
        half_rate = (-0.5 * LRU_C * LOG2_E) * softplus_neg
        a = jnp.exp2(r_tanh * half_rate + half_rate)
        y = 1.0 - a * a
        root = y * lax.rsqrt(jnp.maximum(y, 1e-30))
        hh = (root * u) * (0.5 * i_tanh + 0.5)

        order = range(st) if d == 0 else range(st - 1, -1, -1)
        a_run = h_run = None
        for r in order:
            a_r, x_r = a[r * nt:(r + 1) * nt, :], hh[r * nt:(r + 1) * nt, :]
            if a_run is None:
                a_run, h_run = a_r, x_r
            else:
                h_run = a_r * h_run + x_r
                a_run = a_r * a_run
            ac_sc[d, r] = a_run
            hc_sc[d, r] = h_run

    carry = [carry_sc[0], carry_sc[1]]
    ends = (st - 1, 0)
    for step in range(nt):
        for d in range(2):
            j = step if d == 0 else nt - 1 - step
            cin_sc[d, j:j + 1, :] = carry[d]
            carry[d] = ac_sc[d, ends[d], j:j + 1, :] * carry[d] + hc_sc[d, ends[d], j:j + 1, :]
    carry_sc[0] = carry[0]
    carry_sc[1] = carry[1]

    for d, o_ref in enumerate((hf_ref, hb_ref)):
        cin = cin_sc[d]
        for r in range(st):
            h_r = ac_sc[d, r] * cin + hc_sc[d, r]
            for g in range(ng):
                h_sc[d, g, pl.ds(r, nt, stride=st), :] = h_r[:, g * LANES:(g + 1) * LANES]
        for g in range(ng):
            o_ref[:, g * LANES:(g + 1) * LANES] = h_sc[d, g].astype(o_ref.dtype)


def lru_mixer(z, conv_w, conv_b, wa, ba, wi, bi, lam, batch, seq, tl=512):
    t = batch * seq
    tl = min(tl, seq)
    nb = seq // tl
    hpb = tl // HALO
    last_halo = t // HALO - 1
    fwd = lambda b, i: b * nb + i
    bwd = lambda b, i: b * nb + (nb - 1 - i)
    colx = COL_RX // LRU_WIDTH

    def specs(rb):
        return [
            pl.BlockSpec((tl, LRU_WIDTH), lambda b, i: (rb(b, i), colx)),
            pl.BlockSpec((HALO, LRU_WIDTH), lambda b, i: (jnp.maximum(rb(b, i) * hpb - 1, 0), colx)),
            pl.BlockSpec((HALO, LRU_WIDTH), lambda b, i: (jnp.minimum((rb(b, i) + 1) * hpb, last_halo), colx)),
        ]

    full = lambda shape: pl.BlockSpec(shape, lambda b, i: (0,) * len(shape))
    out_sd = jax.ShapeDtypeStruct((t, LRU_WIDTH), BF16)
    return pl.pallas_call(
        functools.partial(_lru_kernel, tl=tl, nb=nb),
        out_shape=(out_sd, out_sd),
        grid=(batch, nb),
        in_specs=specs(fwd) + specs(bwd) + [
            full((4, LRU_WIDTH)), full((1, LRU_WIDTH)),
            full((2, LRU_WIDTH, LRU_WIDTH)), full((2, 1, LRU_WIDTH)),
            full((2, LRU_WIDTH, LRU_WIDTH)), full((2, 1, LRU_WIDTH)),
            full((2, 1, LRU_WIDTH)),
        ],
        out_specs=(
            pl.BlockSpec((tl, LRU_WIDTH), lambda b, i: (fwd(b, i), 0)),
            pl.BlockSpec((tl, LRU_WIDTH), lambda b, i: (bwd(b, i), 0)),
        ),
        scratch_shapes=[
            pltpu.VMEM((2, LRU_WIDTH // LANES, tl, LANES), F32),
            pltpu.VMEM((2, SCAN_TILE, tl // SCAN_TILE, LRU_WIDTH), F32),
            pltpu.VMEM((2, SCAN_TILE, tl // SCAN_TILE, LRU_WIDTH), F32),
            pltpu.VMEM((2, tl // SCAN_TILE, LRU_WIDTH), F32),
            pltpu.VMEM((2, 1, LRU_WIDTH), F32),
        ],
        compiler_params=_cparams("parallel", "arbitrary"),
        name="lru_mixer",
    )(z, z, z, z, z, z, conv_w, conv_b, wa, ba, wi, bi, lam)


def _rope(x, cos_t, sin_a, sin_b):
    return x * cos_t + pltpu.roll(x, LANES - MLA_ROPE // 2, 1) * sin_a + pltpu.roll(x, MLA_ROPE // 2, 1) * sin_b


def _mla_prep_kernel(qa_ref, kva_ref, misc_ref, cos_ref, sa_ref, sb_ref, qag_ref, wqb_ref, kvag_ref, wkvb_ref,
                     qng_ref, kng_ref, q_out, k_out, v_out):
    cos_t, sin_a, sin_b = cos_ref[...], sa_ref[...], sb_ref[...]
    scale = LOG2_E * MLA_QK ** -0.5

    qa = qa_ref[...].astype(F32)
    qa_n = qa * lax.rsqrt(jnp.mean(qa * qa, axis=-1, keepdims=True) + EPS) * qag_ref[...]
    q = _dot(qa_n.astype(BF16), wqb_ref[...])
    qng = qng_ref[...]
    for h in range(MLA_HEADS):
        qh = q[:, h * MLA_QK_PAD:(h + 1) * MLA_QK_PAD]
        rstd = lax.rsqrt(jnp.sum(qh * qh, axis=-1, keepdims=True) * (1.0 / MLA_QK) + EPS)
        qh = qh * rstd * qng
        q_out[:, h * MLA_QK_PAD:h * MLA_QK_PAD + LANES] = (qh[:, :LANES] * scale).astype(q_out.dtype)
        q_out[:, h * MLA_QK_PAD + LANES:(h + 1) * MLA_QK_PAD] = (
            _rope(qh[:, LANES:], cos_t, sin_a, sin_b) * scale).astype(q_out.dtype)

    kva = kva_ref[...].astype(F32)
    kva_n = kva * lax.rsqrt(jnp.mean(kva * kva, axis=-1, keepdims=True) + EPS) * kvag_ref[...]
    kv = _dot(kva_n.astype(BF16), wkvb_ref[...])
    lane = lax.broadcasted_iota(jnp.int32, misc_ref.shape, 1)
    kr = jnp.where(lane < MLA_ROPE, misc_ref[...].astype(F32), 0.0)
    kr_ss = jnp.sum(kr * kr, axis=-1, keepdims=True)
    kng = kng_ref[...]
    kr_rot = _rope(kr * kng[:, LANES:], cos_t, sin_a, sin_b)
    for h in range(MLA_HEADS):
        k_nope = kv[:, h * 256:h * 256 + LANES]
        rstd = lax.rsqrt((jnp.sum(k_nope * k_nope, axis=-1, keepdims=True) + kr_ss) * (1.0 / MLA_QK) + EPS)
        k_out[:, h * MLA_QK_PAD:h * MLA_QK_PAD + LANES] = (k_nope * rstd * kng[:, :LANES]).astype(k_out.dtype)
        k_out[:, h * MLA_QK_PAD + LANES:(h + 1) * MLA_QK_PAD] = (kr_rot * rstd).astype(k_out.dtype)
        v_out[:, h * MLA_V:(h + 1) * MLA_V] = kv[:, h * 256 + LANES:(h + 1) * 256].astype(v_out.dtype)


def mla_prep(z, cos_t, sin_a, sin_b, qag, wqb, kvag, wkvb, qng, kng, batch, seq, tm=512):
    t = batch * seq
    tm = min(tm, seq)
    nb = seq // tm
    full = lambda shape: pl.BlockSpec(shape, lambda i: (0,) * len(shape))
    tab = pl.BlockSpec((tm, LANES), lambda i: (i % nb, 0))
    return pl.pallas_call(
        _mla_prep_kernel,
        out_shape=(
            jax.ShapeDtypeStruct((t, MLA_HEADS * MLA_QK_PAD), BF16),
            jax.ShapeDtypeStruct((t, MLA_HEADS * MLA_QK_PAD), BF16),
            jax.ShapeDtypeStruct((t, MLA_HEADS * MLA_V), BF16),
        ),
        grid=(t // tm,),
        in_specs=[
            pl.BlockSpec((tm, MLA_Q_RANK), lambda i: (i, COL_MQA // MLA_Q_RANK)),
            pl.BlockSpec((tm, MLA_KV_RANK), lambda i: (i, COL_MKVA // MLA_KV_RANK)),
            pl.BlockSpec((tm, LANES), lambda i: (i, COL_MISC // LANES)),
            tab, tab, tab,
            full((1, MLA_Q_RANK)), full((MLA_Q_RANK, MLA_HEADS * MLA_QK_PAD)),
            full((1, MLA_KV_RANK)), full((MLA_KV_RANK, MLA_HEADS * 256)),
            full((1, MLA_QK_PAD)), full((1, MLA_QK_PAD)),
        ],
        out_specs=(
            pl.BlockSpec((tm, MLA_HEADS * MLA_QK_PAD), lambda i: (i, 0)),
            pl.BlockSpec((tm, MLA_HEADS * MLA_QK_PAD), lambda i: (i, 0)),
            pl.BlockSpec((tm, MLA_HEADS * MLA_V), lambda i: (i, 0)),
        ),
        compiler_params=_cparams("parallel"),
        name="mla_prep",
    )(z, z, z, cos_t, sin_a, sin_b, qag, wqb, kvag, wkvb, qng, kng)


def _attn_kernel(q_ref, k_ref, v_ref, o_ref, vp_sc, m_sc, acc_sc, s_sc, *, tk):
    nk = k_ref.shape[0] // tk

    @pl.when(pl.program_id(2) == 0)
    def _():
        vp_sc[:, :MLA_V] = v_ref[...]
        vp_sc[:, MLA_V:] = jnp.ones((vp_sc.shape[0], MLA_V), vp_sc.dtype)

    q = q_ref[...]
    m_sc[...] = jnp.full_like(m_sc, -jnp.inf)
    acc_sc[...] = jnp.zeros_like(acc_sc)

    def scores(j):
        return _dot_nt(q, k_ref[j * tk:(j + 1) * tk, :])

    def consume(slot, j):
        s = s_sc[slot]
        m_prev = m_sc[...]
        m_new = jnp.maximum(m_prev, jnp.max(s, axis=-1, keepdims=True))
        p = jnp.exp2(s - jnp.concatenate([m_new] * (tk // LANES), axis=1))
        alpha = jnp.exp2(m_prev - m_new)
        pv = _dot(p.astype(BF16), vp_sc[j * tk:(j + 1) * tk, :])
        acc_sc[...] = jnp.concatenate([alpha, alpha], axis=1) * acc_sc[...] + pv
        m_sc[...] = m_new

    s_sc[0] = scores(0)
    for j in range(nk):
        if j + 1 < nk:
            s_sc[(j + 1) % 2] = scores(j + 1)
        consume(j % 2, j)
    acc = acc_sc[...]
    o_ref[...] = (acc[:, :MLA_V] / acc[:, MLA_V:]).astype(o_ref.dtype)


def attention(q, k, v, batch, seq, tq=512, tk=512):
    t = batch * seq
    tq = min(tq, seq)
    tk = min(tk, seq // 2)
    nq = seq // tq
    return pl.pallas_call(
        functools.partial(_attn_kernel, tk=tk),
        out_shape=jax.ShapeDtypeStruct((t, MLA_HEADS * MLA_V), BF16),
        grid=(batch, MLA_HEADS, nq),
        in_specs=[
            pl.BlockSpec((tq, MLA_QK_PAD), lambda b, h, i: (b * nq + i, h)),
            pl.BlockSpec((seq, MLA_QK_PAD), lambda b, h, i: (b, h)),
            pl.BlockSpec((seq, MLA_V), lambda b, h, i: (b, h)),
        ],
        out_specs=pl.BlockSpec((tq, MLA_V), lambda b, h, i: (b * nq + i, h)),
        scratch_shapes=[
            pltpu.VMEM((seq, 2 * MLA_V), BF16),
            pltpu.VMEM((tq, LANES), F32),
            pltpu.VMEM((tq, 2 * MLA_V), F32),
            pltpu.VMEM((2, tq, tk), F32),
        ],
        compiler_params=_cparams("parallel", "parallel", "arbitrary"),
        name="mla_attention",
    )(q, k, v)


def _split3(x):
    hi = x.astype(BF16)
    r1 = x - hi.astype(F32)
    mid = r1.astype(BF16)
    lo = (r1 - mid.astype(F32)).astype(BF16)
    return hi, mid, lo


def _merge_kernel(*refs, with_router):
    (x_ref, gates_ref, og_ref, rg_ref, gf_ref, gb_ref, lf_ref, lb_ref, am_ref,
     gng_ref, wga_ref, wlr_ref, wml_ref, wo_ref, fng_ref) = refs[:15]
    if with_router:
        wr_ref, br_ref, x_out, h_out, comb_out = refs[15:]
    else:
        x_out, h_out = refs[15:]

    o = gf_ref[...].astype(F32) + gb_ref[...].astype(F32)
    gng = gng_ref[...]
    parts = []
    for h in range(GLA_HEADS):
        oh = o[:, h * GLA_DV:(h + 1) * GLA_DV]
        rstd = lax.rsqrt(jnp.mean(oh * oh, axis=-1, keepdims=True) + EPS)
        parts.append(oh * rstd * gng[:, h * GLA_DV:(h + 1) * GLA_DV])
    og = og_ref[...].astype(F32)
    ya_in = jnp.concatenate(parts, axis=-1) * (og * _sigmoid(og))
    y_a = _dot(ya_in.astype(BF16), wga_ref[...])

    rg = rg_ref[...].astype(F32)
    gelu = 0.5 * rg * (1.0 + jnp.tanh(0.7978845608028654 * (rg + 0.044715 * rg * rg * rg)))
    yb_in = (lf_ref[...].astype(F32) + lb_ref[...].astype(F32)) * gelu
    y_b = _dot(yb_in.astype(BF16), wlr_ref[...])

    y_c = _dot(am_ref[...], wml_ref[...])

    d = D_MODEL
    merged = (_sigmoid(gates_ref[:, 0:d].astype(F32)) * y_a
              + _sigmoid(gates_ref[:, d:2 * d].astype(F32)) * y_b
              + _sigmoid(gates_ref[:, 2 * d:3 * d].astype(F32)) * y_c)
    x_new = x_ref[...] + _dot(merged.astype(BF16), wo_ref[...])
    x_out[...] = x_new
    h = x_new * lax.rsqrt(jnp.mean(x_new * x_new, axis=-1, keepdims=True) + EPS) * fng_ref[...]
    h_out[...] = h.astype(h_out.dtype)

    if with_router:
        h_hi, h_mid, h_lo = _split3(h)
        w_hi, w_mid, w_lo = _split3(wr_ref[...])
        logits = (_dot(h_hi, w_hi) + (_dot(h_hi, w_mid) + _dot(h_mid, w_hi))
                  + (_dot(h_hi, w_lo) + _dot(h_mid, w_mid) + _dot(h_lo, w_hi))) + br_ref[...]
        lane = lax.broadcasted_iota(jnp.int32, logits.shape, 1)
        logits = jnp.where(lane < N_EXPERTS, logits, -jnp.inf)
        v1 = jnp.max(logits, axis=-1, keepdims=True)
        i1 = jnp.min(jnp.where(logits == v1, lane, LANES), axis=-1, keepdims=True)
        rest = jnp.where(lane == i1, -jnp.inf, logits)
        v2 = jnp.max(rest, axis=-1, keepdims=True)
        i2 = jnp.min(jnp.where(rest == v2, lane, LANES), axis=-1, keepdims=True)
        e2 = jnp.exp(v2 - v1)
        w1 = 1.0 / (1.0 + e2)
        comb_out[...] = (jnp.where(lane == i1, w1, 0.0) + jnp.where(lane == i2, e2 * w1, 0.0)
                         + jnp.where(lane == i1 + ROUTE_TOP1, 1.0, 0.0) + jnp.where(lane == i2 + ROUTE_TOP2, 1.0, 0.0))


def merge(x2d, z, gla_f, gla_b, lru_f, lru_b, attn, gng, wga, wlr, wml, wo, fng, router=None, tm=512):
    t = x2d.shape[0]
    tm = min(tm, t)
    full = lambda shape: pl.BlockSpec(shape, lambda i: (0,) * len(shape))
    row512 = pl.BlockSpec((tm, 512), lambda i: (i, 0))
    in_specs = [
        pl.BlockSpec((tm, D_MODEL), lambda i: (i, 0)),
        pl.BlockSpec((tm, 3 * D_MODEL), lambda i: (i, COL_GATES // (3 * D_MODEL))),
        pl.BlockSpec((tm, 512), lambda i: (i, COL_GOG // 512)),
        pl.BlockSpec((tm, 512), lambda i: (i, COL_RG // 512)),
        row512, row512, row512, row512, row512,
        full((1, 512)), full((512, D_MODEL)), full((512, D_MODEL)), full((512, D_MODEL)),
        full((D_MODEL, D_MODEL)), full((1, D_MODEL)),
    ]
    args = [x2d, z, z, z, gla_f, gla_b, lru_f, lru_b, attn, gng, wga, wlr, wml, wo, fng]
    out_shape = [jax.ShapeDtypeStruct((t, D_MODEL), F32), jax.ShapeDtypeStruct((t, D_MODEL), BF16)]
    out_specs = [pl.BlockSpec((tm, D_MODEL), lambda i: (i, 0)), pl.BlockSpec((tm, D_MODEL), lambda i: (i, 0))]
    if router is not None:
        in_specs += [full((D_MODEL, LANES)), full((1, LANES))]
        args += list(router)
        out_shape.append(jax.ShapeDtypeStruct((t, LANES), F32))
        out_specs.append(pl.BlockSpec((tm, LANES), lambda i: (i, 0)))
    return pl.pallas_call(
        functools.partial(_merge_kernel, with_router=router is not None),
        out_shape=tuple(out_shape),
        grid=(t // tm,),
        in_specs=in_specs,
        out_specs=tuple(out_specs),
        compiler_params=_cparams("parallel"),
        name="merge_router" if router is not None else "merge",
    )(*args)


def _ffn_kernel(h_ref, x_ref, w1_ref, w3_ref, w2_ref, o_ref):
    h = h_ref[...]
    a = _dot(h, w1_ref[...])
    y = _dot((a * _sigmoid(a) * _dot(h, w3_ref[...])).astype(BF16), w2_ref[...])

    @pl.when(pl.program_id(1) == 0)
    def _():
        o_ref[...] = x_ref[...] + y

    @pl.when(pl.program_id(1) > 0)
    def _():
        o_ref[...] += y


def ffn(h, x2d, w1, w3, w2, tm=512, tf=1408):
    t = x2d.shape[0]
    tm = min(tm, t)
    return pl.pallas_call(
        _ffn_kernel,
        out_shape=jax.ShapeDtypeStruct((t, D_MODEL), F32),
        grid=(t // tm, D_FF // tf),
        in_specs=[
            pl.BlockSpec((tm, D_MODEL), lambda i, j: (i, 0)),
            pl.BlockSpec((tm, D_MODEL), lambda i, j: (i, 0)),
            pl.BlockSpec((D_MODEL, tf), lambda i, j: (0, j)),
            pl.BlockSpec((D_MODEL, tf), lambda i, j: (0, j)),
            pl.BlockSpec((tf, D_MODEL), lambda i, j: (j, 0)),
        ],
        out_specs=pl.BlockSpec((tm, D_MODEL), lambda i, j: (i, 0)),
        compiler_params=_cparams("parallel", "arbitrary"),
        name="ffn",
    )(h, x2d, w1, w3, w2)


MOE_CHUNK = 128
MOE_PERM_ROWS = 512


def _moe_kernel(h_ref, x_ref, route_ref, w1_ref, w3_ref, w2_ref, o_ref,
                xs_sc, ys_sc, ws_sc, pos_sc, meta_sc, *, tb, rows):
    e = pl.program_id(1)
    n_perm = rows // MOE_PERM_ROWS

    def one_hot_rows(c):
        r = (lax.broadcasted_iota(jnp.int32, (MOE_PERM_ROWS, tb), 0) + c * MOE_PERM_ROWS).astype(F32)
        eq_a = r == pos_sc[0:1, :]
        eq_b = r == pos_sc[1:2, :]
        return eq_a, eq_b

    @pl.when(e == 0)
    def _():
        route_t = route_ref[...].T
        comb_t = route_t[0:N_EXPERTS, :]
        top1_t = route_t[ROUTE_TOP1:ROUTE_TOP1 + N_EXPERTS, :]
        top2_t = route_t[ROUTE_TOP2:ROUTE_TOP2 + N_EXPERTS, :]
        sel_t = top1_t + top2_t
        count = jnp.sum(sel_t, axis=1, keepdims=True)
        n_chunk = jnp.floor((count + (MOE_CHUNK - 1)) * (1.0 / MOE_CHUNK))
        base = []
        run = jnp.zeros((1, 1), F32)
        for ex in range(N_EXPERTS):
            base.append(run)
            run = run + n_chunk[ex:ex + 1, :]
        base_chunk = jnp.concatenate(base, axis=0)
        meta_sc[...] = jnp.concatenate(
            [jnp.broadcast_to(base_chunk, (N_EXPERTS, LANES)), jnp.broadcast_to(n_chunk, (N_EXPERTS, LANES))], axis=0)
        t_row = lax.broadcasted_iota(jnp.int32, (tb, tb), 0)
        t_col = lax.broadcasted_iota(jnp.int32, (tb, tb), 1)
        earlier = jnp.where(t_row < t_col, 1.0, 0.0).astype(BF16)
        rank_t = _dot(sel_t.astype(BF16), earlier)
        pos_t = base_chunk * float(MOE_CHUNK) + rank_t
        pos_sc[0:1, :] = jnp.sum(top1_t * pos_t, axis=0, keepdims=True)
        pos_sc[1:2, :] = jnp.sum(top2_t * pos_t, axis=0, keepdims=True)
        w_a = jnp.sum(top1_t * comb_t, axis=0, keepdims=True)
        w_b = jnp.sum(top2_t * comb_t, axis=0, keepdims=True)
        h = h_ref[...]
        for c in range(n_perm):
            eq_a, eq_b = one_hot_rows(c)
            sl = slice(c * MOE_PERM_ROWS, (c + 1) * MOE_PERM_ROWS)
            perm = jnp.where(eq_a, 1.0, jnp.where(eq_b, 1.0, 0.0)).astype(BF16)
            xs_sc[sl, :] = _dot(perm, h).astype(xs_sc.dtype)
            w_rows = jnp.sum(jnp.where(eq_a, w_a, 0.0) + jnp.where(eq_b, w_b, 0.0), axis=1, keepdims=True)
            ws_sc[sl, :] = jnp.broadcast_to(w_rows, (MOE_PERM_ROWS, LANES))
        ys_sc[...] = jnp.zeros_like(ys_sc)

    meta = meta_sc[...]
    sub = lax.broadcasted_iota(jnp.int32, meta.shape, 0)
    base_e = jnp.sum(jnp.where(sub == e, meta, 0.0)[:, 0:1]).astype(jnp.int32)
    n_e = jnp.sum(jnp.where(sub == e + N_EXPERTS, meta, 0.0)[:, 0:1]).astype(jnp.int32)

    def expert_rows(row0, n_rows):
        sl = pl.ds(row0, n_rows)
        xs = xs_sc[sl, :]
        a = _dot(xs, w1_ref[...])
        hidden = a * _sigmoid(a) * _dot(xs, w3_ref[...]) * ws_sc[sl, 0:1]
        ys_sc[sl, :] = _dot(hidden.astype(BF16), w2_ref[...]).astype(ys_sc.dtype)

    def pair(c, carry):
        expert_rows(pl.multiple_of((base_e + 2 * c) * MOE_CHUNK, MOE_CHUNK), 2 * MOE_CHUNK)
        return carry

    lax.fori_loop(0, n_e // 2, pair, 0)

    @pl.when(n_e % 2 == 1)
    def _():
        expert_rows(pl.multiple_of((base_e + n_e - 1) * MOE_CHUNK, MOE_CHUNK), MOE_CHUNK)

    @pl.when(e == N_EXPERTS - 1)
    def _():
        o_ref[...] = x_ref[...]
        for c in range(n_perm):
            eq_a, eq_b = one_hot_rows(c)
            perm = jnp.where(eq_a, 1.0, jnp.where(eq_b, 1.0, 0.0)).astype(BF16)
            o_ref[...] += _dot_tn(perm, ys_sc[c * MOE_PERM_ROWS:(c + 1) * MOE_PERM_ROWS, :])


def moe(h, x2d, route, w1, w3, w2, tb=1024):
    t = x2d.shape[0]
    tb = min(tb, t)
    rows = -(-(2 * tb + N_EXPERTS * (MOE_CHUNK - 1)) // MOE_PERM_ROWS) * MOE_PERM_ROWS
    once = pl.Buffered(1)
    return pl.pallas_call(
        functools.partial(_moe_kernel, tb=tb, rows=rows),
        out_shape=jax.ShapeDtypeStruct((t, D_MODEL), F32),
        grid=(t // tb, N_EXPERTS),
        in_specs=[
            pl.BlockSpec((tb, D_MODEL), lambda i, e: (i, 0), pipeline_mode=once),
            pl.BlockSpec((tb, D_MODEL), lambda i, e: (i, 0), pipeline_mode=once),
            pl.BlockSpec((tb, LANES), lambda i, e: (i, 0)),
            pl.BlockSpec((None, D_MODEL, EXPERT_FF), lambda i, e: (e, 0, 0)),
            pl.BlockSpec((None, D_MODEL, EXPERT_FF), lambda i, e: (e, 0, 0)),
            pl.BlockSpec((None, EXPERT_FF, D_MODEL), lambda i, e: (e, 0, 0)),
        ],
        out_specs=pl.BlockSpec((tb, D_MODEL), lambda i, e: (i, 0)),
        scratch_shapes=[
            pltpu.VMEM((rows, D_MODEL), BF16),
            pltpu.VMEM((rows, D_MODEL), BF16),
            pltpu.VMEM((rows, LANES), F32),
            pltpu.VMEM((8, tb), F32),
            pltpu.VMEM((2 * N_EXPERTS, LANES), F32),
        ],
        compiler_params=pltpu.CompilerParams(dimension_semantics=("parallel", "arbitrary"),
                                             vmem_limit_bytes=MOE_VMEM_LIMIT),
        name="moe",
    )(h, x2d, route, w1, w3, w2)


def _pack_layer(layer, p):
    w_in = p["w_in"][layer]
    sizes = (256, 256, 512, 512, 32, 512, 512, 256, 128, 64, 3072)
    offs = [0]
    for s in sizes:
        offs.append(offs[-1] + s)
    seg = lambda n: w_in[:, offs[n]:offs[n + 1]]
    g_q, g_k, g_v, g_og, g_dec, r_x, r_gate, m_qa, m_kva, m_kr, gates = (seg(n) for n in range(11))
    w_packed = jnp.concatenate(
        [gates, g_v, g_og, r_x, r_gate, g_q, g_k, m_qa, m_kva, m_kr, g_dec, jnp.zeros((D_MODEL, 32), F32)],
        axis=1).astype(BF16)

    wdec = jnp.zeros((2, LANES, 256), F32)
    for d in range(2):
        lo = MLA_ROPE + d * GLA_LOWRANK
        wdec = wdec.at[d, lo:lo + GLA_LOWRANK, :].set(p["gla_w_dec"][layer, d])
    bdec = p["gla_b_dec"][layer].reshape(2, 1, 256)

    def block_diag(w):
        out = jnp.zeros((2, LRU_WIDTH, LRU_WIDTH), F32)
        for n in range(LRU_BLOCKS):
            sl = slice(n * LRU_BLOCK, (n + 1) * LRU_BLOCK)
            out = out.at[:, sl, sl].set(w[:, n])
        return out.astype(BF16)

    wqb = p["mla_w_qb"][layer].reshape(MLA_Q_RANK, MLA_HEADS, MLA_QK)
    wqb = jnp.pad(wqb, ((0, 0), (0, 0), (0, MLA_QK_PAD - MLA_QK))).reshape(MLA_Q_RANK, MLA_HEADS * MLA_QK_PAD)
    pad_qk = lambda g: jnp.pad(g, (0, MLA_QK_PAD - MLA_QK)).reshape(1, MLA_QK_PAD)
    return dict(
        w_in=w_packed,
        norm_mix_g=p["norm_mix_g"][layer].reshape(1, D_MODEL),
        wdec=wdec.astype(BF16), bdec=bdec,
        conv_w=p["lru_conv_w"][layer], conv_b=p["lru_conv_b"][layer].reshape(1, LRU_WIDTH),
        wa=block_diag(0.5 * p["lru_w_a"][layer]), ba=0.5 * p["lru_b_a"][layer].reshape(2, 1, LRU_WIDTH),
        wi=block_diag(0.5 * p["lru_w_i"][layer]), bi=0.5 * p["lru_b_i"][layer].reshape(2, 1, LRU_WIDTH),
        lam=p["lru_lambda"][layer].reshape(2, 1, LRU_WIDTH),
        qag=p["mla_qa_g"][layer].reshape(1, MLA_Q_RANK), wqb=wqb.astype(BF16),
        kvag=p["mla_kva_g"][layer].reshape(1, MLA_KV_RANK), wkvb=p["mla_w_kvb"][layer].astype(BF16),
        qng=pad_qk(p["mla_qn_g"][layer]), kng=pad_qk(p["mla_kn_g"][layer]),
        gng=p["gla_norm_g"][layer].reshape(1, GLA_HEADS * GLA_DV),
        wga=p["gla_w_out"][layer].astype(BF16), wlr=p["lru_w_out"][layer].astype(BF16),
        wml=p["mla_w_out"][layer].astype(BF16), wo=p["w_o"][layer].astype(BF16),
        fng=p["norm_ffn_g"][layer].reshape(1, D_MODEL),
    )


def _rope_tables(length):
    pos = jnp.arange(length, dtype=F32)
    inv = ROPE_THETA ** (-jnp.arange(0, MLA_ROPE, 2, dtype=F32) / MLA_ROPE)
    ang = pos[:, None] * inv[None, :]
    cos, sin = jnp.cos(ang), jnp.sin(ang)
    zero = jnp.zeros_like(cos)
    cos_t = jnp.concatenate([cos, cos, zero, zero], axis=1)
    sin_a = jnp.concatenate([-sin, zero, zero, zero], axis=1)
    sin_b = jnp.concatenate([zero, sin, zero, zero], axis=1)
    return cos_t, sin_a, sin_b


def _trunk(x, layers, ffn_params, moe_params):
    batch, seq, _ = x.shape
    x2d = x.reshape(batch * seq, D_MODEL)
    tables = _rope_tables(seq)
    for layer, lp in enumerate(layers):
        z = in_proj(x2d, lp["norm_mix_g"], lp["w_in"])
        gla_f, gla_b = gla_mixer(z, lp["wdec"], lp["bdec"], batch, seq)
        lru_f, lru_b = lru_mixer(z, lp["conv_w"], lp["conv_b"], lp["wa"], lp["ba"], lp["wi"], lp["bi"], lp["lam"],
                                 batch, seq)
        q, k, v = mla_prep(z, *tables, lp["qag"], lp["wqb"], lp["kvag"], lp["wkvb"], lp["qng"], lp["kng"], batch, seq)
        attn = attention(q, k, v, batch, seq)
        j = layer // 2
        branch = (gla_f, gla_b, lru_f, lru_b, attn, lp["gng"], lp["wga"], lp["wlr"], lp["wml"], lp["wo"], lp["fng"])
        if layer % 2 == 0:
            x2d, h = merge(x2d, z, *branch)
            w1, w3, w2 = ffn_params[j]
            x2d = ffn(h, x2d, w1, w3, w2)
        else:
            x2d, h, comb = merge(x2d, z, *branch, router=moe_params[j][:2])
            _, _, w1, w3, w2 = moe_params[j]
            x2d = moe(h, x2d, comb, w1, w3, w2)
    return x2d.reshape(batch, seq, D_MODEL)


def kernel(x_prompt, x_sample, norm_mix_g, w_in, gla_w_dec, gla_b_dec, gla_norm_g, gla_w_out, lru_conv_w, lru_conv_b, lru_w_a, lru_b_a, lru_w_i, lru_b_i, lru_lambda, lru_w_out, mla_qa_g, mla_w_qb, mla_kva_g, mla_w_kvb, mla_qn_g, mla_kn_g, mla_w_out, w_o, norm_ffn_g, ffn_w1, ffn_w3, ffn_w2, moe_w_router, moe_b_router, moe_w1, moe_w3, moe_w2):
    p = dict(norm_mix_g=norm_mix_g, w_in=w_in, gla_w_dec=gla_w_dec, gla_b_dec=gla_b_dec, gla_norm_g=gla_norm_g,
             gla_w_out=gla_w_out, lru_conv_w=lru_conv_w, lru_conv_b=lru_conv_b, lru_w_a=lru_w_a, lru_b_a=lru_b_a,
             lru_w_i=lru_w_i, lru_b_i=lru_b_i, lru_lambda=lru_lambda, lru_w_out=lru_w_out, mla_qa_g=mla_qa_g,
             mla_w_qb=mla_w_qb, mla_kva_g=mla_kva_g, mla_w_kvb=mla_w_kvb, mla_qn_g=mla_qn_g, mla_kn_g=mla_kn_g,
             mla_w_out=mla_w_out, w_o=w_o, norm_ffn_g=norm_ffn_g)
    depth = w_in.shape[0]
    layers = [_pack_layer(layer, p) for layer in range(depth)]
    ffn_params = [(ffn_w1[j].astype(BF16), ffn_w3[j].astype(BF16), ffn_w2[j].astype(BF16))
                  for j in range(ffn_w1.shape[0])]
    moe_params = []
    for j in range(moe_w1.shape[0]):
        w_router = jnp.pad(moe_w_router[j], ((0, 0), (0, LANES - N_EXPERTS)))
        b_router = jnp.pad(moe_b_router[j], (0, LANES - N_EXPERTS)).reshape(1, LANES)
        moe_params.append((w_router, b_router, moe_w1[j].astype(BF16), moe_w3[j].astype(BF16),
                           moe_w2[j].astype(BF16)))
    y_prompt = _trunk(x_prompt, layers, ffn_params, moe_params)
    y_sample = _trunk(x_sample, layers, ffn_params, moe_params)
    return (y_prompt, y_sample)
```

```python
import functools

import jax
import jax.numpy as jnp
from jax import lax
from jax.experimental import pallas as pl
from jax.experimental.pallas import tpu as pltpu

F32 = jnp.float32
BF16 = jnp.bfloat16

D_MODEL = 1024
EPS = 1e-6
GLA_HEADS = 4
GLA_DK = 64
GLA_DV = 128
GLA_LOWRANK = 16
GLA_GATE_NORM = 16.0
GLA_CHUNK = 64
LRU_WIDTH = 512
LRU_BLOCKS = 8
LRU_BLOCK = LRU_WIDTH // LRU_BLOCKS
LRU_C = 8.0
MLA_HEADS = 4
MLA_Q_RANK = 256
MLA_KV_RANK = 128
MLA_NOPE = 128
MLA_ROPE = 64
MLA_V = 128
MLA_QK = MLA_NOPE + MLA_ROPE
MLA_QK_PAD = 256
ROPE_THETA = 10000.0
D_FF = 2816
N_EXPERTS = 8
EXPERT_FF = 1408
LANES = 128
LOG2_E = 1.4426950408889634

COL_GATES = 0
COL_GV = 3072
COL_GOG = 3584
COL_RX = 4096
COL_RG = 4608
COL_GQ = 5120
COL_GK = 5376
COL_MQA = 5632
COL_MKVA = 5888
COL_MISC = 6016
Z_COLS = 6144

ROUTE_TOP1 = 8
ROUTE_TOP2 = 16

VMEM_LIMIT = 56 * 1024 * 1024
MOE_VMEM_LIMIT = 62 * 1024 * 1024


def _cparams(*sem):
    return pltpu.CompilerParams(dimension_semantics=sem, vmem_limit_bytes=VMEM_LIMIT)


def _sigmoid(x):
    return 1.0 / (1.0 + jnp.exp(-x))


def _dot(a, b):
    return jnp.dot(a, b, preferred_element_type=F32)


def _dot_nt(a, b):
    return lax.dot_general(a, b, (((1,), (1,)), ((), ())), preferred_element_type=F32)


def _dot_tn(a, b):
    return lax.dot_general(a, b, (((0,), (0,)), ((), ())), preferred_element_type=F32)


def _inproj_kernel(x_ref, g_ref, w_ref, z_ref, h_ref):
    @pl.when(pl.program_id(1) == 0)
    def _():
        x = x_ref[...]
        ms = jnp.mean(x * x, axis=-1, keepdims=True)
        h_ref[...] = (x * lax.rsqrt(ms + EPS) * g_ref[...]).astype(BF16)

    z_ref[...] = _dot(h_ref[...], w_ref[...]).astype(z_ref.dtype)


def in_proj(x2d, g, w_packed, tm=1024, tn=1536):
    t = x2d.shape[0]
    tm = min(tm, t)
    return pl.pallas_call(
        _inproj_kernel,
        out_shape=jax.ShapeDtypeStruct((t, Z_COLS), BF16),
        grid=(t // tm, Z_COLS // tn),
        in_specs=[
            pl.BlockSpec((tm, D_MODEL), lambda i, j: (i, 0)),
            pl.BlockSpec((1, D_MODEL), lambda i, j: (0, 0)),
            pl.BlockSpec((D_MODEL, tn), lambda i, j: (0, j)),
        ],
        out_specs=pl.BlockSpec((tm, tn), lambda i, j: (i, j)),
        scratch_shapes=[pltpu.VMEM((tm, D_MODEL), BF16)],
        compiler_params=_cparams("parallel", "arbitrary"),
        name="in_proj",
    )(x2d, g, w_packed)


def _gla_kernel(qf_ref, kf_ref, vf_ref, mf_ref, qb_ref, kb_ref, vb_ref, mb_ref, wdec_ref, bdec_ref,
                of_ref, ob_ref, s_ref, g_ref, qm_sc, km_sc, ki_sc, dec_sc, p_sc, u_sc, sb_sc, *, tl):
    c = GLA_CHUNK
    nc = tl // c

    @pl.when(pl.program_id(1) == 0)
    def _():
        s_ref[...] = jnp.zeros_like(s_ref)

    for d, m_ref in enumerate((mf_ref, mb_ref)):
        x = _dot(m_ref[...], wdec_ref[d]) + bdec_ref[d]
        logsig = jnp.minimum(x, 0.0) - jnp.log(1.0 + jnp.exp(-jnp.abs(x)))
        g_ref[d] = logsig * (1.0 / GLA_GATE_NORM)

    row = lax.broadcasted_iota(jnp.int32, (c, c), 0)
    col = lax.broadcasted_iota(jnp.int32, (c, c), 1)
    tri = (row >= col, row <= col)
    lane = lax.broadcasted_iota(jnp.int32, (c, LANES), 1)
    head_mask = (lane < GLA_DK, lane >= GLA_DK)
    dirs = ((qf_ref, kf_ref, vf_ref, of_ref), (qb_ref, kb_ref, vb_ref, ob_ref))

    ones_tri = [jnp.where(t, 1.0, 0.0).astype(BF16) for t in tri]
    work = [(d, cc) for cc in range(nc) for d in range(2)]
    rows = lambda cc: slice(cc * c, (cc + 1) * c)
    cols = lambda h: slice(h * GLA_DV, (h + 1) * GLA_DV)


    for d, cc in work:
        q_ref, k_ref, _, _ = dirs[d]
        g = g_ref[d, rows(cc), :]
        g_hi = g.astype(BF16)
        r1 = g - g_hi.astype(F32)
        g_mid = r1.astype(BF16)
        g_lo = (r1 - g_mid.astype(F32)).astype(BF16)
        b = _dot(ones_tri[d], g_hi) + _dot(ones_tri[d], g_mid) + _dot(ones_tri[d], g_lo)
        b_tot = b[c - 1:c, :] if d == 0 else b[0:1, :]
        qc = q_ref[rows(cc), :].astype(F32) * (GLA_DK ** -0.5)
        kc = k_ref[rows(cc), :].astype(F32)
        q_dec = qc * jnp.exp(b)
        k_inv = kc * jnp.exp(-b)
        k_end = kc * jnp.exp(b_tot - b)
        dec_sc[d, cc] = jnp.exp(b_tot)
        for p in range(GLA_HEADS // 2):
            sl = slice(p * LANES, (p + 1) * LANES)
            ki_sc[d, p, rows(cc), :] = k_inv[:, sl].astype(BF16)
            for hh in range(2):
                qm_sc[d, 2 * p + hh, rows(cc), :] = jnp.where(head_mask[hh], q_dec[:, sl], 0.0).astype(BF16)
                km_sc[d, 2 * p + hh, rows(cc), :] = jnp.where(head_mask[hh], k_end[:, sl], 0.0).astype(BF16)

    for d, cc in work:
        v_ref = dirs[d][2]
        for h in range(GLA_HEADS):
            scores = _dot_nt(qm_sc[d, h, rows(cc), :], ki_sc[d, h // 2, rows(cc), :])
            p_sc[d, h, rows(cc), :] = jnp.where(tri[d], scores, 0.0).astype(BF16)
            u_sc[d, h, cc] = _dot_tn(v_ref[rows(cc), cols(h)], km_sc[d, h, rows(cc), :])

    for h in range(GLA_HEADS):
        sl = slice((h // 2) * LANES, (h // 2 + 1) * LANES)
        for d in range(2):
            st = s_ref[d, h]
            for ci in range(nc):
                cc = ci if d == 0 else nc - 1 - ci
                sb_sc[d, h, cc] = st.astype(BF16)
                st = st * dec_sc[d, cc][:, sl] + u_sc[d, h, cc]
            s_ref[d, h] = st

    for d, cc in work:
        _, _, v_ref, o_ref = dirs[d]
        for h in range(GLA_HEADS):
            o_h = (_dot(p_sc[d, h, rows(cc), :], v_ref[rows(cc), cols(h)])
                   + _dot_nt(qm_sc[d, h, rows(cc), :], sb_sc[d, h, cc]))
            o_ref[rows(cc), cols(h)] = o_h.astype(o_ref.dtype)


def gla_mixer(z, wdec, bdec, batch, seq, tl=512):
    t = batch * seq
    tl = min(tl, seq)
    nb = seq // tl
    fwd = lambda b, i: b * nb + i
    bwd = lambda b, i: b * nb + (nb - 1 - i)

    def specs(rb):
        return [
            pl.BlockSpec((tl, 256), lambda b, i: (rb(b, i), COL_GQ // 256)),
            pl.BlockSpec((tl, 256), lambda b, i: (rb(b, i), COL_GK // 256)),
            pl.BlockSpec((tl, 512), lambda b, i: (rb(b, i), COL_GV // 512)),
            pl.BlockSpec((tl, LANES), lambda b, i: (rb(b, i), COL_MISC // LANES)),
        ]

    out_sd = jax.ShapeDtypeStruct((t, GLA_HEADS * GLA_DV), BF16)
    return pl.pallas_call(
        functools.partial(_gla_kernel, tl=tl),
        out_shape=(out_sd, out_sd),
        grid=(batch, nb),
        in_specs=specs(fwd) + specs(bwd) + [
            pl.BlockSpec((2, LANES, 256), lambda b, i: (0, 0, 0)),
            pl.BlockSpec((2, 1, 256), lambda b, i: (0, 0, 0)),
        ],
        out_specs=(
            pl.BlockSpec((tl, 512), lambda b, i: (fwd(b, i), 0)),
            pl.BlockSpec((tl, 512), lambda b, i: (bwd(b, i), 0)),
        ),
        scratch_shapes=[
            pltpu.VMEM((2, GLA_HEADS, GLA_DV, LANES), F32),
            pltpu.VMEM((2, tl, 256), F32),
            pltpu.VMEM((2, GLA_HEADS, tl, LANES), BF16),
            pltpu.VMEM((2, GLA_HEADS, tl, LANES), BF16),
            pltpu.VMEM((2, GLA_HEADS // 2, tl, LANES), BF16),
            pltpu.VMEM((2, tl // GLA_CHUNK, 1, 256), F32),
            pltpu.VMEM((2, GLA_HEADS, tl, GLA_CHUNK), BF16),
            pltpu.VMEM((2, GLA_HEADS, tl // GLA_CHUNK, GLA_DV, LANES), F32),
            pltpu.VMEM((2, GLA_HEADS, tl // GLA_CHUNK, GLA_DV, LANES), BF16),
        ],
        compiler_params=_cparams("parallel", "arbitrary"),
        name="gla_mixer",
    )(z, z, z, z, z, z, z, z, wdec, bdec)


HALO = 16
SCAN_TILE = 8


def _lru_kernel(xf_ref, xfp_ref, xfn_ref, xb_ref, xbp_ref, xbn_ref, cw_ref, cb_ref, wa_ref, ba_ref,
                wi_ref, bi_ref, lam_ref, hf_ref, hb_ref, h_sc, ac_sc, hc_sc, cin_sc, carry_sc, *, tl, nb):
    i = pl.program_id(1)
    st = SCAN_TILE
    nt = tl // st
    ng = LRU_WIDTH // LANES

    @pl.when(i == 0)
    def _():
        carry_sc[...] = jnp.zeros_like(carry_sc)

    blocks = ((xf_ref, xfp_ref, xfn_ref, i), (xb_ref, xbp_ref, xbn_ref, nb - 1 - i))

    tile_row = lax.broadcasted_iota(jnp.int32, (nt, LRU_WIDTH), 0)
    cw = cw_ref[...]
    conv_bias = cb_ref[...]

    for d, (x_ref, xp_ref, xn_ref, li) in enumerate(blocks):
        x = x_ref[...].astype(F32)
        for g in range(ng):
            h_sc[d, g] = x[:, g * LANES:(g + 1) * LANES]
        xr = [jnp.concatenate([h_sc[d, g, pl.ds(r, nt, stride=st), :] for g in range(ng)], axis=1)
              for r in range(st)]
        prev = jnp.where(li > 0, xp_ref[...].astype(F32), 0.0)
        nxt = jnp.where(li < nb - 1, xn_ref[...].astype(F32), 0.0)

        def from_prev_tile(a, first):
            return jnp.where(tile_row == 0, first, pltpu.roll(a, 1, 0))

        def from_next_tile(a, last):
            return jnp.where(tile_row == nt - 1, last, pltpu.roll(a, nt - 1, 0))

        neighbours = {-2: from_prev_tile(xr[st - 2], prev[HALO - 2:HALO - 1, :]),
                      -1: from_prev_tile(xr[st - 1], prev[HALO - 1:HALO, :]),
                      st: from_next_tile(xr[0], nxt[0:1, :])}
        at = lambda r: xr[r] if 0 <= r < st else neighbours[r]
        u = jnp.concatenate(
            [cw[0:1, :] * at(r - 2) + cw[1:2, :] * at(r - 1) + cw[2:3, :] * at(r) + cw[3:4, :] * at(r + 1) + conv_bias
             for r in range(st)], axis=0)
        ub = u.astype(BF16)
        r_tanh = jnp.tanh(_dot(ub, wa_ref[d]) + ba_ref[d])
        i_tanh = jnp.tanh(_dot(ub, wi_ref[d]) + bi_ref[d])
        lam = lam_ref[d]
        softplus_neg = jnp.maximum(-lam, 0.0) + jnp.log(1.0 + jnp.exp(-jnp.abs(lam)))
        half_rate = (-0.5 * LRU_C * LOG2_E) * softplus_neg
        a = jnp.exp2(r_tanh * half_rate + half_rate)
        y = 1.0 - a * a
        root = y * lax.rsqrt(jnp.maximum(y, 1e-30))
        hh = (root * u) * (0.5 * i_tanh + 0.5)

        order = range(st) if d == 0 else range(st - 1, -1, -1)
        a_run = h_run = None
        for r in order:
            a_r, x_r = a[r * nt:(r + 1) * nt, :], hh[r * nt:(r + 1) * nt, :]
            if a_run is None:
                a_run, h_run = a_r, x_r
            else:
                h_run = a_r * h_run + x_r
                a_run = a_r * a_run
            ac_sc[d, r] = a_run
            hc_sc[d, r] = h_run

    carry = [carry_sc[0], carry_sc[1]]
    ends = (st - 1, 0)
    for step in range(nt):
        for d in range(2):
            j = step if d == 0 else nt - 1 - step
            cin_sc[d, j:j + 1, :] = carry[d]
            carry[d] = ac_sc[d, ends[d], j:j + 1, :] * carry[d] + hc_sc[d, ends[d], j:j + 1, :]
    carry_sc[0] = carry[0]
    carry_sc[1] = carry[1]

    for d, o_ref in enumerate((hf_ref, hb_ref)):
        cin = cin_sc[d]
        for r in range(st):
            h_r = ac_sc[d, r] * cin + hc_sc[d, r]
            for g in range(ng):
                h_sc[d, g, pl.ds(r, nt, stride=st), :] = h_r[:, g * LANES:(g + 1) * LANES]
        for g in range(ng):
            o_ref[:, g * LANES:(g + 1) * LANES] = h_sc[d, g].astype(o_ref.dtype)


def lru_mixer(z, conv_w, conv_b, wa, ba, wi, bi, lam, batch, seq, tl=512):
    t = batch * seq
    tl = min(tl, seq)
    nb = seq // tl
    hpb = tl // HALO
    last_halo = t // HALO - 1
    fwd = lambda b, i: b * nb + i
    bwd = lambda b, i: b * nb + (nb - 1 - i)
    colx = COL_RX // LRU_WIDTH

    def specs(rb):
        return [
            pl.BlockSpec((tl, LRU_WIDTH), lambda b, i: (rb(b, i), colx)),
            pl.BlockSpec((HALO, LRU_WIDTH), lambda b, i: (jnp.maximum(rb(b, i) * hpb - 1, 0), colx)),
            pl.BlockSpec((HALO, LRU_WIDTH), lambda b, i: (jnp.minimum((rb(b, i) + 1) * hpb, last_halo), colx)),
        ]

    full = lambda shape: pl.BlockSpec(shape, lambda b, i: (0,) * len(shape))
    out_sd = jax.ShapeDtypeStruct((t, LRU_WIDTH), BF16)
    return pl.pallas_call(
        functools.partial(_lru_kernel, tl=tl, nb=nb),
        out_shape=(out_sd, out_sd),
        grid=(batch, nb),
        in_specs=specs(fwd) + specs(bwd) + [
            full((4, LRU_WIDTH)), full((1, LRU_WIDTH)),
            full((2, LRU_WIDTH, LRU_WIDTH)), full((2, 1, LRU_WIDTH)),
            full((2, LRU_WIDTH, LRU_WIDTH)), full((2, 1, LRU_WIDTH)),
            full((2, 1, LRU_WIDTH)),
        ],
        out_specs=(
            pl.BlockSpec((tl, LRU_WIDTH), lambda b, i: (fwd(b, i), 0)),
            pl.BlockSpec((tl, LRU_WIDTH), lambda b, i: (bwd(b, i), 0)),
        ),
        scratch_shapes=[
            pltpu.VMEM((2, LRU_WIDTH // LANES, tl, LANES), F32),
            pltpu.VMEM((2, SCAN_TILE, tl // SCAN_TILE, LRU_WIDTH), F32),
            pltpu.VMEM((2, SCAN_TILE, tl // SCAN_TILE, LRU_WIDTH), F32),
            pltpu.VMEM((2, tl // SCAN_TILE, LRU_WIDTH), F32),
            pltpu.VMEM((2, 1, LRU_WIDTH), F32),
        ],
        compiler_params=_cparams("parallel", "arbitrary"),
        name="lru_mixer",
    )(z, z, z, z, z, z, conv_w, conv_b, wa, ba, wi, bi, lam)


def _rope(x, cos_t, sin_a, sin_b):
    return x * cos_t + pltpu.roll(x, LANES - MLA_ROPE // 2, 1) * sin_a + pltpu.roll(x, MLA_ROPE // 2, 1) * sin_b


def _mla_prep_kernel(qa_ref, kva_ref, misc_ref, cos_ref, sa_ref, sb_ref, qag_ref, wqb_ref, kvag_ref, wkvb_ref,
                     qng_ref, kng_ref, q_out, k_out, v_out):
    cos_t, sin_a, sin_b = cos_ref[...], sa_ref[...], sb_ref[...]
    scale = LOG2_E * MLA_QK ** -0.5

    qa = qa_ref[...].astype(F32)
    qa_n = qa * lax.rsqrt(jnp.mean(qa * qa, axis=-1, keepdims=True) + EPS) * qag_ref[...]
    q = _dot(qa_n.astype(BF16), wqb_ref[...])
    qng = qng_ref[...]
    for h in range(MLA_HEADS):
        qh = q[:, h * MLA_QK_PAD:(h + 1) * MLA_QK_PAD]
        rstd = lax.rsqrt(jnp.sum(qh * qh, axis=-1, keepdims=True) * (1.0 / MLA_QK) + EPS)
        qh = qh * rstd * qng
        q_out[:, h * MLA_QK_PAD:h * MLA_QK_PAD + LANES] = (qh[:, :LANES] * scale).astype(q_out.dtype)
        q_out[:, h * MLA_QK_PAD + LANES:(h + 1) * MLA_QK_PAD] = (
            _rope(qh[:, LANES:], cos_t, sin_a, sin_b) * scale).astype(q_out.dtype)

    kva = kva_ref[...].astype(F32)
    kva_n = kva * lax.rsqrt(jnp.mean(kva * kva, axis=-1, keepdims=True) + EPS) * kvag_ref[...]
    kv = _dot(kva_n.astype(BF16), wkvb_ref[...])
    lane = lax.broadcasted_iota(jnp.int32, misc_ref.shape, 1)
    kr = jnp.where(lane < MLA_ROPE, misc_ref[...].astype(F32), 0.0)
    kr_ss = jnp.sum(kr * kr, axis=-1, keepdims=True)
    kng = kng_ref[...]
    kr_rot = _rope(kr * kng[:, LANES:], cos_t, sin_a, sin_b)
    for h in range(MLA_HEADS):
        k_nope = kv[:, h * 256:h * 256 + LANES]
        rstd = lax.rsqrt((jnp.sum(k_nope * k_nope, axis=-1, keepdims=True) + kr_ss) * (1.0 / MLA_QK) + EPS)
        k_out[:, h * MLA_QK_PAD:h * MLA_QK_PAD + LANES] = (k_nope * rstd * kng[:, :LANES]).astype(k_out.dtype)
        k_out[:, h * MLA_QK_PAD + LANES:(h + 1) * MLA_QK_PAD] = (kr_rot * rstd).astype(k_out.dtype)
        v_out[:, h * MLA_V:(h + 1) * MLA_V] = kv[:, h * 256 + LANES:(h + 1) * 256].astype(v_out.dtype)


def mla_prep(z, cos_t, sin_a, sin_b, qag, wqb, kvag, wkvb, qng, kng, batch, seq, tm=512):
    t = batch * seq
    tm = min(tm, seq)
    nb = seq // tm
    full = lambda shape: pl.BlockSpec(shape, lambda i: (0,) * len(shape))
    tab = pl.BlockSpec((tm, LANES), lambda i: (i % nb, 0))
    return pl.pallas_call(
        _mla_prep_kernel,
        out_shape=(
            jax.ShapeDtypeStruct((t, MLA_HEADS * MLA_QK_PAD), BF16),
            jax.ShapeDtypeStruct((t, MLA_HEADS * MLA_QK_PAD), BF16),
            jax.ShapeDtypeStruct((t, MLA_HEADS * MLA_V), BF16),
        ),
        grid=(t // tm,),
        in_specs=[
            pl.BlockSpec((tm, MLA_Q_RANK), lambda i: (i, COL_MQA // MLA_Q_RANK)),
            pl.BlockSpec((tm, MLA_KV_RANK), lambda i: (i, COL_MKVA // MLA_KV_RANK)),
            pl.BlockSpec((tm, LANES), lambda i: (i, COL_MISC // LANES)),
            tab, tab, tab,
            full((1, MLA_Q_RANK)), full((MLA_Q_RANK, MLA_HEADS * MLA_QK_PAD)),
            full((1, MLA_KV_RANK)), full((MLA_KV_RANK, MLA_HEADS * 256)),
            full((1, MLA_QK_PAD)), full((1, MLA_QK_PAD)),
        ],
        out_specs=(
            pl.BlockSpec((tm, MLA_HEADS * MLA_QK_PAD), lambda i: (i, 0)),
            pl.BlockSpec((tm, MLA_HEADS * MLA_QK_PAD), lambda i: (i, 0)),
            pl.BlockSpec((tm, MLA_HEADS * MLA_V), lambda i: (i, 0)),
        ),
        compiler_params=_cparams("parallel"),
        name="mla_prep",
    )(z, z, z, cos_t, sin_a, sin_b, qag, wqb, kvag, wkvb, qng, kng)


def _attn_kernel(q_ref, k_ref, v_ref, o_ref, vp_sc, m_sc, acc_sc, s_sc, *, tk):
    nk = k_ref.shape[0] // tk

    @pl.when(pl.program_id(2) == 0)
    def _():
        vp_sc[:, :MLA_V] = v_ref[...]
        vp_sc[:, MLA_V:] = jnp.ones((vp_sc.shape[0], MLA_V), vp_sc.dtype)

    q = q_ref[...]
    m_sc[...] = jnp.full_like(m_sc, -jnp.inf)
    acc_sc[...] = jnp.zeros_like(acc_sc)

    def scores(j):
        return _dot_nt(q, k_ref[j * tk:(j + 1) * tk, :])

    def consume(slot, j):
        s = s_sc[slot]
        m_prev = m_sc[...]
        m_new = jnp.maximum(m_prev, jnp.max(s, axis=-1, keepdims=True))
        p = jnp.exp2(s - jnp.concatenate([m_new] * (tk // LANES), axis=1))
        alpha = jnp.exp2(m_prev - m_new)
        pv = _dot(p.astype(BF16), vp_sc[j * tk:(j + 1) * tk, :])
        acc_sc[...] = jnp.concatenate([alpha, alpha], axis=1) * acc_sc[...] + pv
        m_sc[...] = m_new

    s_sc[0] = scores(0)
    for j in range(nk):
        if j + 1 < nk:
            s_sc[(j + 1) % 2] = scores(j + 1)
        consume(j % 2, j)
    acc = acc_sc[...]
    o_ref[...] = (acc[:, :MLA_V] / acc[:, MLA_V:]).astype(o_ref.dtype)


def attention(q, k, v, batch, seq, tq=1024, tk=512):
    t = batch * seq
    tq = min(tq, seq)
    tk = min(tk, seq // 2)
    nq = seq // tq
    return pl.pallas_call(
        functools.partial(_attn_kernel, tk=tk),
        out_shape=jax.ShapeDtypeStruct((t, MLA_HEADS * MLA_V), BF16),
        grid=(batch, MLA_HEADS, nq),
        in_specs=[
            pl.BlockSpec((tq, MLA_QK_PAD), lambda b, h, i: (b * nq + i, h)),
            pl.BlockSpec((seq, MLA_QK_PAD), lambda b, h, i: (b, h)),
            pl.BlockSpec((seq, MLA_V), lambda b, h, i: (b, h)),
        ],
        out_specs=pl.BlockSpec((tq, MLA_V), lambda b, h, i: (b * nq + i, h)),
        scratch_shapes=[
            pltpu.VMEM((seq, 2 * MLA_V), BF16),
            pltpu.VMEM((tq, LANES), F32),
            pltpu.VMEM((tq, 2 * MLA_V), F32),
            pltpu.VMEM((2, tq, tk), F32),
        ],
        compiler_params=_cparams("parallel", "parallel", "arbitrary"),
        name="mla_attention",
    )(q, k, v)


def _split2(x):
    hi = x.astype(BF16)
    return hi, (x - hi.astype(F32)).astype(BF16)


def _merge_kernel(*refs, with_router):
    (x_ref, gates_ref, og_ref, rg_ref, gf_ref, gb_ref, lf_ref, lb_ref, am_ref,
     gng_ref, wga_ref, wlr_ref, wml_ref, wo_ref, fng_ref) = refs[:15]
    if with_router:
        wr_ref, br_ref, x_out, h_out, comb_out = refs[15:]
    else:
        x_out, h_out = refs[15:]

    o = gf_ref[...].astype(F32) + gb_ref[...].astype(F32)
    gng = gng_ref[...]
    parts = []
    for h in range(GLA_HEADS):
        oh = o[:, h * GLA_DV:(h + 1) * GLA_DV]
        rstd = lax.rsqrt(jnp.mean(oh * oh, axis=-1, keepdims=True) + EPS)
        parts.append(oh * rstd * gng[:, h * GLA_DV:(h + 1) * GLA_DV])
    og = og_ref[...].astype(F32)
    ya_in = jnp.concatenate(parts, axis=-1) * (og * _sigmoid(og))
    y_a = _dot(ya_in.astype(BF16), wga_ref[...])

    rg = rg_ref[...].astype(F32)
    gelu = 0.5 * rg * (1.0 + jnp.tanh(0.7978845608028654 * (rg + 0.044715 * rg * rg * rg)))
    yb_in = (lf_ref[...].astype(F32) + lb_ref[...].astype(F32)) * gelu
    y_b = _dot(yb_in.astype(BF16), wlr_ref[...])

    y_c = _dot(am_ref[...], wml_ref[...])

    d = D_MODEL
    merged = 0.5 * ((jnp.tanh(gates_ref[:, 0:d].astype(F32)) * y_a
                     + jnp.tanh(gates_ref[:, d:2 * d].astype(F32)) * y_b
                     + jnp.tanh(gates_ref[:, 2 * d:3 * d].astype(F32)) * y_c)
                    + (y_a + y_b + y_c))
    x_new = x_ref[...] + _dot(merged.astype(BF16), wo_ref[...])
    x_out[...] = x_new
    h = x_new * lax.rsqrt(jnp.mean(x_new * x_new, axis=-1, keepdims=True) + EPS) * fng_ref[...]
    h_out[...] = h.astype(h_out.dtype)

    if with_router:
        h_hi, h_mid = _split2(h)
        w_parts = jnp.concatenate(_split2(wr_ref[...]), axis=1)
        part_a = _dot(h_hi, w_parts)
        part_b = _dot(h_mid, w_parts)
        logits = ((part_a[:, :LANES] + part_a[:, LANES:]) + (part_b[:, :LANES] + part_b[:, LANES:])) + br_ref[...]
        lane = lax.broadcasted_iota(jnp.int32, logits.shape, 1)
        logits = jnp.where(lane < N_EXPERTS, logits, -jnp.inf)
        v1 = jnp.max(logits, axis=-1, keepdims=True)
        i1 = jnp.min(jnp.where(logits == v1, lane, LANES), axis=-1, keepdims=True)
        rest = jnp.where(lane == i1, -jnp.inf, logits)
        v2 = jnp.max(rest, axis=-1, keepdims=True)
        i2 = jnp.min(jnp.where(rest == v2, lane, LANES), axis=-1, keepdims=True)
        e2 = jnp.exp(v2 - v1)
        w1 = 1.0 / (1.0 + e2)
        comb_out[...] = (jnp.where(lane == i1, w1, 0.0) + jnp.where(lane == i2, e2 * w1, 0.0)
                         + jnp.where(lane == i1 + ROUTE_TOP1, 1.0, 0.0) + jnp.where(lane == i2 + ROUTE_TOP2, 1.0, 0.0))


def merge(x2d, z, gla_f, gla_b, lru_f, lru_b, attn, gng, wga, wlr, wml, wo, fng, router=None, tm=512):
    t = x2d.shape[0]
    tm = min(tm, t)
    full = lambda shape: pl.BlockSpec(shape, lambda i: (0,) * len(shape))
    row512 = pl.BlockSpec((tm, 512), lambda i: (i, 0))
    in_specs = [
        pl.BlockSpec((tm, D_MODEL), lambda i: (i, 0)),
        pl.BlockSpec((tm, 3 * D_MODEL), lambda i: (i, COL_GATES // (3 * D_MODEL))),
        pl.BlockSpec((tm, 512), lambda i: (i, COL_GOG // 512)),
        pl.BlockSpec((tm, 512), lambda i: (i, COL_RG // 512)),
        row512, row512, row512, row512, row512,
        full((1, 512)), full((512, D_MODEL)), full((512, D_MODEL)), full((512, D_MODEL)),
        full((D_MODEL, D_MODEL)), full((1, D_MODEL)),
    ]
    args = [x2d, z, z, z, gla_f, gla_b, lru_f, lru_b, attn, gng, wga, wlr, wml, wo, fng]
    out_shape = [jax.ShapeDtypeStruct((t, D_MODEL), F32), jax.ShapeDtypeStruct((t, D_MODEL), BF16)]
    out_specs = [pl.BlockSpec((tm, D_MODEL), lambda i: (i, 0)), pl.BlockSpec((tm, D_MODEL), lambda i: (i, 0))]
    if router is not None:
        in_specs += [full((D_MODEL, LANES)), full((1, LANES))]
        args += list(router)
        out_shape.append(jax.ShapeDtypeStruct((t, LANES), F32))
        out_specs.append(pl.BlockSpec((tm, LANES), lambda i: (i, 0)))
    return pl.pallas_call(
        functools.partial(_merge_kernel, with_router=router is not None),
        out_shape=tuple(out_shape),
        grid=(t // tm,),
        in_specs=in_specs,
        out_specs=tuple(out_specs),
        compiler_params=_cparams("parallel"),
        name="merge_router" if router is not None else "merge",
    )(*args)


def _ffn_kernel(h_ref, x_ref, w1_ref, w3_ref, w2_ref, o_ref):
    h = h_ref[...]
    a = _dot(h, w1_ref[...])
    y = _dot((a * _sigmoid(a) * _dot(h, w3_ref[...])).astype(BF16), w2_ref[...])

    @pl.when(pl.program_id(1) == 0)
    def _():
        o_ref[...] = x_ref[...] + y

    @pl.when(pl.program_id(1) > 0)
    def _():
        o_ref[...] += y


def ffn(h, x2d, w1, w3, w2, tm=512, tf=1408):
    t = x2d.shape[0]
    tm = min(tm, t)
    return pl.pallas_call(
        _ffn_kernel,
        out_shape=jax.ShapeDtypeStruct((t, D_MODEL), F32),
        grid=(t // tm, D_FF // tf),
        in_specs=[
            pl.BlockSpec((tm, D_MODEL), lambda i, j: (i, 0)),
            pl.BlockSpec((tm, D_MODEL), lambda i, j: (i, 0)),
            pl.BlockSpec((D_MODEL, tf), lambda i, j: (0, j)),
            pl.BlockSpec((D_MODEL, tf), lambda i, j: (0, j)),
            pl.BlockSpec((tf, D_MODEL), lambda i, j: (j, 0)),
        ],
        out_specs=pl.BlockSpec((tm, D_MODEL), lambda i, j: (i, 0)),
        compiler_params=_cparams("parallel", "arbitrary"),
        name="ffn",
    )(h, x2d, w1, w3, w2)


MOE_CHUNK = 128
MOE_PERM_ROWS = 512


def _moe_kernel(h_ref, x_ref, route_ref, w1_ref, w3_ref, w2_ref, o_ref,
                xs_sc, ys_sc, ws_sc, pos_sc, meta_sc, *, tb, rows):
    e = pl.program_id(1)
    n_perm = rows // MOE_PERM_ROWS

    def one_hot_rows(c):
        r = (lax.broadcasted_iota(jnp.int32, (MOE_PERM_ROWS, tb), 0) + c * MOE_PERM_ROWS).astype(F32)
        eq_a = r == pos_sc[0:1, :]
        eq_b = r == pos_sc[1:2, :]
        return eq_a, eq_b

    @pl.when(e == 0)
    def _():
        route_t = route_ref[...].T
        comb_t = route_t[0:N_EXPERTS, :]
        top1_t = route_t[ROUTE_TOP1:ROUTE_TOP1 + N_EXPERTS, :]
        top2_t = route_t[ROUTE_TOP2:ROUTE_TOP2 + N_EXPERTS, :]
        sel_t = top1_t + top2_t
        count = jnp.sum(sel_t, axis=1, keepdims=True)
        n_chunk = jnp.floor((count + (MOE_CHUNK - 1)) * (1.0 / MOE_CHUNK))
        base = []
        run = jnp.zeros((1, 1), F32)
        for ex in range(N_EXPERTS):
            base.append(run)
            run = run + n_chunk[ex:ex + 1, :]
        base_chunk = jnp.concatenate(base, axis=0)
        meta_sc[...] = jnp.concatenate(
            [jnp.broadcast_to(base_chunk, (N_EXPERTS, LANES)), jnp.broadcast_to(n_chunk, (N_EXPERTS, LANES))], axis=0)
        t_row = lax.broadcasted_iota(jnp.int32, (tb, tb), 0)
        t_col = lax.broadcasted_iota(jnp.int32, (tb, tb), 1)
        earlier = jnp.where(t_row < t_col, 1.0, 0.0).astype(BF16)
        rank_t = _dot(sel_t.astype(BF16), earlier)
        pos_t = base_chunk * float(MOE_CHUNK) + rank_t
        pos_sc[0:1, :] = jnp.sum(top1_t * pos_t, axis=0, keepdims=True)
        pos_sc[1:2, :] = jnp.sum(top2_t * pos_t, axis=0, keepdims=True)
        w_a = jnp.sum(top1_t * comb_t, axis=0, keepdims=True)
        w_b = jnp.sum(top2_t * comb_t, axis=0, keepdims=True)
        h = h_ref[...]
        for c in range(n_perm):
            eq_a, eq_b = one_hot_rows(c)
            sl = slice(c * MOE_PERM_ROWS, (c + 1) * MOE_PERM_ROWS)
            perm = jnp.where(eq_a, 1.0, jnp.where(eq_b, 1.0, 0.0)).astype(BF16)
            xs_sc[sl, :] = _dot(perm, h).astype(xs_sc.dtype)
            w_rows = jnp.sum(jnp.where(eq_a, w_a, 0.0) + jnp.where(eq_b, w_b, 0.0), axis=1, keepdims=True)
            ws_sc[sl, :] = jnp.broadcast_to(w_rows, (MOE_PERM_ROWS, LANES))
        ys_sc[...] = jnp.zeros_like(ys_sc)

    meta = meta_sc[...]
    sub = lax.broadcasted_iota(jnp.int32, meta.shape, 0)
    base_e = jnp.sum(jnp.where(sub == e, meta, 0.0)[:, 0:1]).astype(jnp.int32)
    n_e = jnp.sum(jnp.where(sub == e + N_EXPERTS, meta, 0.0)[:, 0:1]).astype(jnp.int32)

    def expert_rows(row0, n_rows):
        sl = pl.ds(row0, n_rows)
        xs = xs_sc[sl, :]
        a = _dot(xs, w1_ref[...])
        hidden = a * _sigmoid(a) * _dot(xs, w3_ref[...]) * ws_sc[sl, 0:1]
        ys_sc[sl, :] = _dot(hidden.astype(BF16), w2_ref[...]).astype(ys_sc.dtype)

    def pair(c, carry):
        expert_rows(pl.multiple_of((base_e + 2 * c) * MOE_CHUNK, MOE_CHUNK), 2 * MOE_CHUNK)
        return carry

    lax.fori_loop(0, n_e // 2, pair, 0)

    @pl.when(n_e % 2 == 1)
    def _():
        expert_rows(pl.multiple_of((base_e + n_e - 1) * MOE_CHUNK, MOE_CHUNK), MOE_CHUNK)

    @pl.when(e == N_EXPERTS - 1)
    def _():
        o_ref[...] = x_ref[...]
        for c in range(n_perm):
            eq_a, eq_b = one_hot_rows(c)
            perm = jnp.where(eq_a, 1.0, jnp.where(eq_b, 1.0, 0.0)).astype(BF16)
            o_ref[...] += _dot_tn(perm, ys_sc[c * MOE_PERM_ROWS:(c + 1) * MOE_PERM_ROWS, :])


def moe(h, x2d, route, w1, w3, w2, tb=1024):
    t = x2d.shape[0]
    tb = min(tb, t)
    rows = -(-(2 * tb + N_EXPERTS * (MOE_CHUNK - 1)) // MOE_PERM_ROWS) * MOE_PERM_ROWS
    once = pl.Buffered(1)
    return pl.pallas_call(
        functools.partial(_moe_kernel, tb=tb, rows=rows),
        out_shape=jax.ShapeDtypeStruct((t, D_MODEL), F32),
        grid=(t // tb, N_EXPERTS),
        in_specs=[
            pl.BlockSpec((tb, D_MODEL), lambda i, e: (i, 0), pipeline_mode=once),
            pl.BlockSpec((tb, D_MODEL), lambda i, e: (i, 0), pipeline_mode=once),
            pl.BlockSpec((tb, LANES), lambda i, e: (i, 0)),
            pl.BlockSpec((None, D_MODEL, EXPERT_FF), lambda i, e: (e, 0, 0)),
            pl.BlockSpec((None, D_MODEL, EXPERT_FF), lambda i, e: (e, 0, 0)),
            pl.BlockSpec((None, EXPERT_FF, D_MODEL), lambda i, e: (e, 0, 0)),
        ],
        out_specs=pl.BlockSpec((tb, D_MODEL), lambda i, e: (i, 0)),
        scratch_shapes=[
            pltpu.VMEM((rows, D_MODEL), BF16),
            pltpu.VMEM((rows, D_MODEL), BF16),
            pltpu.VMEM((rows, LANES), F32),
            pltpu.VMEM((8, tb), F32),
            pltpu.VMEM((2 * N_EXPERTS, LANES), F32),
        ],
        compiler_params=pltpu.CompilerParams(dimension_semantics=("parallel", "arbitrary"),
                                             vmem_limit_bytes=MOE_VMEM_LIMIT),
        name="moe",
    )(h, x2d, route, w1, w3, w2)


def _pack_layer(layer, p):
    w_in = p["w_in"][layer]
    sizes = (256, 256, 512, 512, 32, 512, 512, 256, 128, 64, 3072)
    offs = [0]
    for s in sizes:
        offs.append(offs[-1] + s)
    seg = lambda n: w_in[:, offs[n]:offs[n + 1]]
    g_q, g_k, g_v, g_og, g_dec, r_x, r_gate, m_qa, m_kva, m_kr, gates = (seg(n) for n in range(11))
    w_packed = jnp.concatenate(
        [0.5 * gates, g_v, g_og, r_x, r_gate, g_q, g_k, m_qa, m_kva, m_kr, g_dec, jnp.zeros((D_MODEL, 32), F32)],
        axis=1).astype(BF16)

    wdec = jnp.zeros((2, LANES, 256), F32)
    for d in range(2):
        lo = MLA_ROPE + d * GLA_LOWRANK
        wdec = wdec.at[d, lo:lo + GLA_LOWRANK, :].set(p["gla_w_dec"][layer, d])
    bdec = p["gla_b_dec"][layer].reshape(2, 1, 256)

    def block_diag(w):
        out = jnp.zeros((2, LRU_WIDTH, LRU_WIDTH), F32)
        for n in range(LRU_BLOCKS):
            sl = slice(n * LRU_BLOCK, (n + 1) * LRU_BLOCK)
            out = out.at[:, sl, sl].set(w[:, n])
        return out.astype(BF16)

    wqb = p["mla_w_qb"][layer].reshape(MLA_Q_RANK, MLA_HEADS, MLA_QK)
    wqb = jnp.pad(wqb, ((0, 0), (0, 0), (0, MLA_QK_PAD - MLA_QK))).reshape(MLA_Q_RANK, MLA_HEADS * MLA_QK_PAD)
    pad_qk = lambda g: jnp.pad(g, (0, MLA_QK_PAD - MLA_QK)).reshape(1, MLA_QK_PAD)
    return dict(
        w_in=w_packed,
        norm_mix_g=p["norm_mix_g"][layer].reshape(1, D_MODEL),
        wdec=wdec.astype(BF16), bdec=bdec,
        conv_w=p["lru_conv_w"][layer], conv_b=p["lru_conv_b"][layer].reshape(1, LRU_WIDTH),
        wa=block_diag(0.5 * p["lru_w_a"][layer]), ba=0.5 * p["lru_b_a"][layer].reshape(2, 1, LRU_WIDTH),
        wi=block_diag(0.5 * p["lru_w_i"][layer]), bi=0.5 * p["lru_b_i"][layer].reshape(2, 1, LRU_WIDTH),
        lam=p["lru_lambda"][layer].reshape(2, 1, LRU_WIDTH),
        qag=p["mla_qa_g"][layer].reshape(1, MLA_Q_RANK), wqb=wqb.astype(BF16),
        kvag=p["mla_kva_g"][layer].reshape(1, MLA_KV_RANK), wkvb=p["mla_w_kvb"][layer].astype(BF16),
        qng=pad_qk(p["mla_qn_g"][layer]), kng=pad_qk(p["mla_kn_g"][layer]),
        gng=p["gla_norm_g"][layer].reshape(1, GLA_HEADS * GLA_DV),
        wga=p["gla_w_out"][layer].astype(BF16), wlr=p["lru_w_out"][layer].astype(BF16),
        wml=p["mla_w_out"][layer].astype(BF16), wo=p["w_o"][layer].astype(BF16),
        fng=p["norm_ffn_g"][layer].reshape(1, D_MODEL),
    )


def _rope_tables(length):
    pos = jnp.arange(length, dtype=F32)
    inv = ROPE_THETA ** (-jnp.arange(0, MLA_ROPE, 2, dtype=F32) / MLA_ROPE)
    ang = pos[:, None] * inv[None, :]
    cos, sin = jnp.cos(ang), jnp.sin(ang)
    zero = jnp.zeros_like(cos)
    cos_t = jnp.concatenate([cos, cos, zero, zero], axis=1)
    sin_a = jnp.concatenate([-sin, zero, zero, zero], axis=1)
    sin_b = jnp.concatenate([zero, sin, zero, zero], axis=1)
    return cos_t, sin_a, sin_b


def _trunk(x, layers, ffn_params, moe_params):
    batch, seq, _ = x.shape
    x2d = x.reshape(batch * seq, D_MODEL)
    tables = _rope_tables(seq)
    for layer, lp in enumerate(layers):
        z = in_proj(x2d, lp["norm_mix_g"], lp["w_in"])
        gla_f, gla_b = gla_mixer(z, lp["wdec"], lp["bdec"], batch, seq)
        lru_f, lru_b = lru_mixer(z, lp["conv_w"], lp["conv_b"], lp["wa"], lp["ba"], lp["wi"], lp["bi"], lp["lam"],
                                 batch, seq)
        q, k, v = mla_prep(z, *tables, lp["qag"], lp["wqb"], lp["kvag"], lp["wkvb"], lp["qng"], lp["kng"], batch, seq)
        attn = attention(q, k, v, batch, seq)
        j = layer // 2
        branch = (gla_f, gla_b, lru_f, lru_b, attn, lp["gng"], lp["wga"], lp["wlr"], lp["wml"], lp["wo"], lp["fng"])
        if layer % 2 == 0:
            x2d, h = merge(x2d, z, *branch)
            w1, w3, w2 = ffn_params[j]
            x2d = ffn(h, x2d, w1, w3, w2)
        else:
            x2d, h, comb = merge(x2d, z, *branch, router=moe_params[j][:2])
            _, _, w1, w3, w2 = moe_params[j]
            x2d = moe(h, x2d, comb, w1, w3, w2)
    return x2d.reshape(batch, seq, D_MODEL)


def kernel(x_prompt, x_sample, norm_mix_g, w_in, gla_w_dec, gla_b_dec, gla_norm_g, gla_w_out, lru_conv_w, lru_conv_b, lru_w_a, lru_b_a, lru_w_i, lru_b_i, lru_lambda, lru_w_out, mla_qa_g, mla_w_qb, mla_kva_g, mla_w_kvb, mla_qn_g, mla_kn_g, mla_w_out, w_o, norm_ffn_g, ffn_w1, ffn_w3, ffn_w2, moe_w_router, moe_b_router, moe_w1, moe_w3, moe_w2):
    p = dict(norm_mix_g=norm_mix_g, w_in=w_in, gla_w_dec=gla_w_dec, gla_b_dec=gla_b_dec, gla_norm_g=gla_norm_g,
             gla_w_out=gla_w_out, lru_conv_w=lru_conv_w, lru_conv_b=lru_conv_b, lru_w_a=lru_w_a, lru_b_a=lru_b_a,
             lru_w_i=lru_w_i, lru_b_i=lru_b_i, lru_lambda=lru_lambda, lru_w_out=lru_w_out, mla_qa_g=mla_qa_g,
             mla_w_qb=mla_w_qb, mla_kva_g=mla_kva_g, mla_w_kvb=mla_w_kvb, mla_qn_g=mla_qn_g, mla_kn_g=mla_kn_g,
             mla_w_out=mla_w_out, w_o=w_o, norm_ffn_g=norm_ffn_g)
    depth = w_in.shape[0]
    layers = [_pack_layer(layer, p) for layer in range(depth)]
    ffn_params = [(ffn_w1[j].astype(BF16), ffn_w3[j].astype(BF16), ffn_w2[j].astype(BF16))
                  for j in range(ffn_w1.shape[0])]
    moe_params = []
    for j in range(moe_w1.shape[0]):
        w_router = jnp.pad(moe_w_router[j], ((0, 0), (0, LANES - N_EXPERTS)))
        b_router = jnp.pad(moe_b_router[j], (0, LANES - N_EXPERTS)).reshape(1, LANES)
        moe_params.append((w_router, b_router, moe_w1[j].astype(BF16), moe_w3[j].astype(BF16),
                           moe_w2[j].astype(BF16)))
    y_prompt = _trunk(x_prompt, layers, ffn_params, moe_params)
    y_sample = _trunk(x_sample, layers, ffn_params, moe_params)
    return (y_prompt, y_sample)
```

```python
import functools

import jax
import jax.numpy as jnp
from jax import lax
from jax.experimental import pallas as pl
from jax.experimental.pallas import tpu as pltpu

F32 = jnp.float32
BF16 = jnp.bfloat16

D_MODEL = 1024
EPS = 1e-6
GLA_HEADS = 4
GLA_DK = 64
GLA_DV = 128
GLA_LOWRANK = 16
GLA_GATE_NORM = 16.0
GLA_CHUNK = 64
LRU_WIDTH = 512
LRU_BLOCKS = 8
LRU_BLOCK = LRU_WIDTH // LRU_BLOCKS
LRU_C = 8.0
MLA_HEADS = 4
MLA_Q_RANK = 256
MLA_KV_RANK = 128
MLA_NOPE = 128
MLA_ROPE = 64
MLA_V = 128
MLA_QK = MLA_NOPE + MLA_ROPE
MLA_QK_PAD = 256
ROPE_THETA = 10000.0
D_FF = 2816
N_EXPERTS = 8
EXPERT_FF = 1408
LANES = 128
LOG2_E = 1.4426950408889634

COL_GATES = 0
COL_GV = 3072
COL_GOG = 3584
COL_RX = 4096
COL_RG = 4608
COL_GQ = 5120
COL_GK = 5376
COL_MQA = 5632
COL_MKVA = 5888
COL_MISC = 6016
Z_COLS = 6144

ROUTE_TOP1 = 8
ROUTE_TOP2 = 16

VMEM_LIMIT = 56 * 1024 * 1024
MOE_VMEM_LIMIT = 62 * 1024 * 1024


def _cparams(*sem):
    return pltpu.CompilerParams(dimension_semantics=sem, vmem_limit_bytes=VMEM_LIMIT)


def _sigmoid(x):
    return 1.0 / (1.0 + jnp.exp(-x))


def _dot(a, b):
    return jnp.dot(a, b, preferred_element_type=F32)


def _dot_nt(a, b):
    return lax.dot_general(a, b, (((1,), (1,)), ((), ())), preferred_element_type=F32)


def _dot_tn(a, b):
    return lax.dot_general(a, b, (((0,), (0,)), ((), ())), preferred_element_type=F32)


def _inproj_kernel(x_ref, g_ref, w_ref, z_ref, h_ref):
    @pl.when(pl.program_id(1) == 0)
    def _():
        x = x_ref[...]
        ms = jnp.mean(x * x, axis=-1, keepdims=True)
        h_ref[...] = (x * lax.rsqrt(ms + EPS) * g_ref[...]).astype(BF16)

    z_ref[...] = _dot(h_ref[...], w_ref[...]).astype(z_ref.dtype)


def in_proj(x2d, g, w_packed, tm=1024, tn=3072):
    t = x2d.shape[0]
    tm = min(tm, t)
    return pl.pallas_call(
        _inproj_kernel,
        out_shape=jax.ShapeDtypeStruct((t, Z_COLS), BF16),
        grid=(t // tm, Z_COLS // tn),
        in_specs=[
            pl.BlockSpec((tm, D_MODEL), lambda i, j: (i, 0)),
            pl.BlockSpec((1, D_MODEL), lambda i, j: (0, 0)),
            pl.BlockSpec((D_MODEL, tn), lambda i, j: (0, j)),
        ],
        out_specs=pl.BlockSpec((tm, tn), lambda i, j: (i, j)),
        scratch_shapes=[pltpu.VMEM((tm, D_MODEL), BF16)],
        compiler_params=_cparams("parallel", "arbitrary"),
        name="in_proj",
    )(x2d, g, w_packed)


def _gla_kernel(qf_ref, kf_ref, vf_ref, mf_ref, qb_ref, kb_ref, vb_ref, mb_ref, wdec_ref, bdec_ref,
                of_ref, ob_ref, s_ref, g_ref, qm_sc, km_sc, ki_sc, dec_sc, p_sc, u_sc, sb_sc, *, tl):
    c = GLA_CHUNK
    nc = tl // c

    @pl.when(pl.program_id(1) == 0)
    def _():
        s_ref[...] = jnp.zeros_like(s_ref)

    for d, m_ref in enumerate((mf_ref, mb_ref)):
        x = _dot(m_ref[...], wdec_ref[d]) + bdec_ref[d]
        logsig = jnp.minimum(x, 0.0) - jnp.log(1.0 + jnp.exp(-jnp.abs(x)))
        g_ref[d] = logsig * (1.0 / GLA_GATE_NORM)

    row = lax.broadcasted_iota(jnp.int32, (c, c), 0)
    col = lax.broadcasted_iota(jnp.int32, (c, c), 1)
    tri = (row >= col, row <= col)
    lane = lax.broadcasted_iota(jnp.int32, (c, LANES), 1)
    head_mask = (lane < GLA_DK, lane >= GLA_DK)
    dirs = ((qf_ref, kf_ref, vf_ref, of_ref), (qb_ref, kb_ref, vb_ref, ob_ref))

    ones_tri = [jnp.where(t, 1.0, 0.0).astype(BF16) for t in tri]
    work = [(d, cc) for cc in range(nc) for d in range(2)]
    rows = lambda cc: slice(cc * c, (cc + 1) * c)
    cols = lambda h: slice(h * GLA_DV, (h + 1) * GLA_DV)


    for d, cc in work:
        q_ref, k_ref, _, _ = dirs[d]
        g = g_ref[d, rows(cc), :]
        g_hi = g.astype(BF16)
        r1 = g - g_hi.astype(F32)
        g_mid = r1.astype(BF16)
        g_lo = (r1 - g_mid.astype(F32)).astype(BF16)
        b = _dot(ones_tri[d], g_hi) + _dot(ones_tri[d], g_mid) + _dot(ones_tri[d], g_lo)
        b_tot = b[c - 1:c, :] if d == 0 else b[0:1, :]
        qc = q_ref[rows(cc), :].astype(F32) * (GLA_DK ** -0.5)
        kc = k_ref[rows(cc), :].astype(F32)
        q_dec = qc * jnp.exp(b)
        k_inv = kc * jnp.exp(-b)
        k_end = kc * jnp.exp(b_tot - b)
        dec_sc[d, cc] = jnp.exp(b_tot)
        for p in range(GLA_HEADS // 2):
            sl = slice(p * LANES, (p + 1) * LANES)
            ki_sc[d, p, rows(cc), :] = k_inv[:, sl].astype(BF16)
            for hh in range(2):
                qm_sc[d, 2 * p + hh, rows(cc), :] = jnp.where(head_mask[hh], q_dec[:, sl], 0.0).astype(BF16)
                km_sc[d, 2 * p + hh, rows(cc), :] = jnp.where(head_mask[hh], k_end[:, sl], 0.0).astype(BF16)

    for d, cc in work:
        v_ref = dirs[d][2]
        for h in range(GLA_HEADS):
            scores = _dot_nt(qm_sc[d, h, rows(cc), :], ki_sc[d, h // 2, rows(cc), :])
            p_sc[d, h, rows(cc), :] = jnp.where(tri[d], scores, 0.0).astype(BF16)
            u_sc[d, h, cc] = _dot_tn(v_ref[rows(cc), cols(h)], km_sc[d, h, rows(cc), :])

    for h in range(GLA_HEADS):
        sl = slice((h // 2) * LANES, (h // 2 + 1) * LANES)
        for d in range(2):
            st = s_ref[d, h]
            for ci in range(nc):
                cc = ci if d == 0 else nc - 1 - ci
                sb_sc[d, h, cc] = st.astype(BF16)
                st = st * dec_sc[d, cc][:, sl] + u_sc[d, h, cc]
            s_ref[d, h] = st

    for d, cc in work:
        _, _, v_ref, o_ref = dirs[d]
        for h in range(GLA_HEADS):
            o_h = (_dot(p_sc[d, h, rows(cc), :], v_ref[rows(cc), cols(h)])
                   + _dot_nt(qm_sc[d, h, rows(cc), :], sb_sc[d, h, cc]))
            o_ref[rows(cc), cols(h)] = o_h.astype(o_ref.dtype)


def gla_mixer(z, wdec, bdec, batch, seq, tl=512):
    t = batch * seq
    tl = min(tl, seq)
    nb = seq // tl
    fwd = lambda b, i: b * nb + i
    bwd = lambda b, i: b * nb + (nb - 1 - i)

    def specs(rb):
        return [
            pl.BlockSpec((tl, 256), lambda b, i: (rb(b, i), COL_GQ // 256)),
            pl.BlockSpec((tl, 256), lambda b, i: (rb(b, i), COL_GK // 256)),
            pl.BlockSpec((tl, 512), lambda b, i: (rb(b, i), COL_GV // 512)),
            pl.BlockSpec((tl, LANES), lambda b, i: (rb(b, i), COL_MISC // LANES)),
        ]

    out_sd = jax.ShapeDtypeStruct((t, GLA_HEADS * GLA_DV), BF16)
    return pl.pallas_call(
        functools.partial(_gla_kernel, tl=tl),
        out_shape=(out_sd, out_sd),
        grid=(batch, nb),
        in_specs=specs(fwd) + specs(bwd) + [
            pl.BlockSpec((2, LANES, 256), lambda b, i: (0, 0, 0)),
            pl.BlockSpec((2, 1, 256), lambda b, i: (0, 0, 0)),
        ],
        out_specs=(
            pl.BlockSpec((tl, 512), lambda b, i: (fwd(b, i), 0)),
            pl.BlockSpec((tl, 512), lambda b, i: (bwd(b, i), 0)),
        ),
        scratch_shapes=[
            pltpu.VMEM((2, GLA_HEADS, GLA_DV, LANES), F32),
            pltpu.VMEM((2, tl, 256), F32),
            pltpu.VMEM((2, GLA_HEADS, tl, LANES), BF16),
            pltpu.VMEM((2, GLA_HEADS, tl, LANES), BF16),
            pltpu.VMEM((2, GLA_HEADS // 2, tl, LANES), BF16),
            pltpu.VMEM((2, tl // GLA_CHUNK, 1, 256), F32),
            pltpu.VMEM((2, GLA_HEADS, tl, GLA_CHUNK), BF16),
            pltpu.VMEM((2, GLA_HEADS, tl // GLA_CHUNK, GLA_DV, LANES), F32),
            pltpu.VMEM((2, GLA_HEADS, tl // GLA_CHUNK, GLA_DV, LANES), BF16),
        ],
        compiler_params=_cparams("parallel", "arbitrary"),
        name="gla_mixer",
    )(z, z, z, z, z, z, z, z, wdec, bdec)


HALO = 16
SCAN_TILE = 8


def _lru_kernel(xf_ref, xfp_ref, xfn_ref, xb_ref, xbp_ref, xbn_ref, cw_ref, cb_ref, wa_ref, ba_ref,
                wi_ref, bi_ref, lam_ref, hf_ref, hb_ref, h_sc, ac_sc, hc_sc, cin_sc, carry_sc, *, tl, nb):
    i = pl.program_id(1)
    st = SCAN_TILE
    nt = tl // st
    ng = LRU_WIDTH // LANES

    @pl.when(i == 0)
    def _():
        carry_sc[...] = jnp.zeros_like(carry_sc)

    blocks = ((xf_ref, xfp_ref, xfn_ref, i), (xb_ref, xbp_ref, xbn_ref, nb - 1 - i))

    tile_row = lax.broadcasted_iota(jnp.int32, (nt, LRU_WIDTH), 0)
    cw = cw_ref[...]
    conv_bias = cb_ref[...]

    for d, (x_ref, xp_ref, xn_ref, li) in enumerate(blocks):
        x = x_ref[...].astype(F32)
        for g in range(ng):
            h_sc[d, g] = x[:, g * LANES:(g + 1) * LANES]
        xr = [jnp.concatenate([h_sc[d, g, pl.ds(r, nt, stride=st), :] for g in range(ng)], axis=1)
              for r in range(st)]
        prev = jnp.where(li > 0, xp_ref[...].astype(F32), 0.0)
        nxt = jnp.where(li < nb - 1, xn_ref[...].astype(F32), 0.0)

        def from_prev_tile(a, first):
            return jnp.where(tile_row == 0, first, pltpu.roll(a, 1, 0))

        def from_next_tile(a, last):
            return jnp.where(tile_row == nt - 1, last, pltpu.roll(a, nt - 1, 0))

        neighbours = {-2: from_prev_tile(xr[st - 2], prev[HALO - 2:HALO - 1, :]),
                      -1: from_prev_tile(xr[st - 1], prev[HALO - 1:HALO, :]),
                      st: from_next_tile(xr[0], nxt[0:1, :])}
        at = lambda r: xr[r] if 0 <= r < st else neighbours[r]
        u = jnp.concatenate(
            [cw[0:1, :] * at(r - 2) + cw[1:2, :] * at(r - 1) + cw[2:3, :] * at(r) + cw[3:4, :] * at(r + 1) + conv_bias
             for r in range(st)], axis=0)
        ub = u.astype(BF16)
        r_tanh = jnp.tanh(_dot(ub, wa_ref[d]) + ba_ref[d])
        i_tanh = jnp.tanh(_dot(ub, wi_ref[d]) + bi_ref[d])
        lam = lam_ref[d]
        softplus_neg = jnp.maximum(-lam, 0.0) + jnp.log(1.0 + jnp.exp(-jnp.abs(lam)))
        half_rate = (-0.5 * LRU_C * LOG2_E) * softplus_neg
        a = jnp.exp2(r_tanh * half_rate + half_rate)
        y = 1.0 - a * a
        root = y * lax.rsqrt(jnp.maximum(y, 1e-30))
        hh = (root * u) * (0.5 * i_tanh + 0.5)

        order = range(st) if d == 0 else range(st - 1, -1, -1)
        a_run = h_run = None
        for r in order:
            a_r, x_r = a[r * nt:(r + 1) * nt, :], hh[r * nt:(r + 1) * nt, :]
            if a_run is None:
                a_run, h_run = a_r, x_r
            else:
                h_run = a_r * h_run + x_r
                a_run = a_r * a_run
            ac_sc[d, r] = a_run
            hc_sc[d, r] = h_run

    carry = [carry_sc[0], carry_sc[1]]
    ends = (st - 1, 0)
    for step in range(nt):
        for d in range(2):
            j = step if d == 0 else nt - 1 - step
            cin_sc[d, j:j + 1, :] = carry[d]
            carry[d] = ac_sc[d, ends[d], j:j + 1, :] * carry[d] + hc_sc[d, ends[d], j:j + 1, :]
    carry_sc[0] = carry[0]
    carry_sc[1] = carry[1]

    for d, o_ref in enumerate((hf_ref, hb_ref)):
        cin = cin_sc[d]
        for r in range(st):
            h_r = ac_sc[d, r] * cin + hc_sc[d, r]
            for g in range(ng):
                h_sc[d, g, pl.ds(r, nt, stride=st), :] = h_r[:, g * LANES:(g + 1) * LANES]
        for g in range(ng):
            o_ref[:, g * LANES:(g + 1) * LANES] = h_sc[d, g].astype(o_ref.dtype)


def lru_mixer(z, conv_w, conv_b, wa, ba, wi, bi, lam, batch, seq, tl=512):
    t = batch * seq
    tl = min(tl, seq)
    nb = seq // tl
    hpb = tl // HALO
    last_halo = t // HALO - 1
    fwd = lambda b, i: b * nb + i
    bwd = lambda b, i: b * nb + (nb - 1 - i)
    colx = COL_RX // LRU_WIDTH

    def specs(rb):
        return [
            pl.BlockSpec((tl, LRU_WIDTH), lambda b, i: (rb(b, i), colx)),
            pl.BlockSpec((HALO, LRU_WIDTH), lambda b, i: (jnp.maximum(rb(b, i) * hpb - 1, 0), colx)),
            pl.BlockSpec((HALO, LRU_WIDTH), lambda b, i: (jnp.minimum((rb(b, i) + 1) * hpb, last_halo), colx)),
        ]

    full = lambda shape: pl.BlockSpec(shape, lambda b, i: (0,) * len(shape))
    out_sd = jax.ShapeDtypeStruct((t, LRU_WIDTH), BF16)
    return pl.pallas_call(
        functools.partial(_lru_kernel, tl=tl, nb=nb),
        out_shape=(out_sd, out_sd),
        grid=(batch, nb),
        in_specs=specs(fwd) + specs(bwd) + [
            full((4, LRU_WIDTH)), full((1, LRU_WIDTH)),
            full((2, LRU_WIDTH, LRU_WIDTH)), full((2, 1, LRU_WIDTH)),
            full((2, LRU_WIDTH, LRU_WIDTH)), full((2, 1, LRU_WIDTH)),
            full((2, 1, LRU_WIDTH)),
        ],
        out_specs=(
            pl.BlockSpec((tl, LRU_WIDTH), lambda b, i: (fwd(b, i), 0)),
            pl.BlockSpec((tl, LRU_WIDTH), lambda b, i: (bwd(b, i), 0)),
        ),
        scratch_shapes=[
            pltpu.VMEM((2, LRU_WIDTH // LANES, tl, LANES), F32),
            pltpu.VMEM((2, SCAN_TILE, tl // SCAN_TILE, LRU_WIDTH), F32),
            pltpu.VMEM((2, SCAN_TILE, tl // SCAN_TILE, LRU_WIDTH), F32),
            pltpu.VMEM((2, tl // SCAN_TILE, LRU_WIDTH), F32),
            pltpu.VMEM((2, 1, LRU_WIDTH), F32),
        ],
        compiler_params=_cparams("parallel", "arbitrary"),
        name="lru_mixer",
    )(z, z, z, z, z, z, conv_w, conv_b, wa, ba, wi, bi, lam)


def _rope(x, cos_t, sin_a, sin_b):
    return x * cos_t + pltpu.roll(x, LANES - MLA_ROPE // 2, 1) * sin_a + pltpu.roll(x, MLA_ROPE // 2, 1) * sin_b


def _mla_prep_kernel(qa_ref, kva_ref, misc_ref, cos_ref, sa_ref, sb_ref, qag_ref, wqb_ref, kvag_ref, wkvb_ref,
                     qng_ref, kng_ref, q_out, k_out, v_out):
    cos_t, sin_a, sin_b = cos_ref[...], sa_ref[...], sb_ref[...]
    scale = LOG2_E * MLA_QK ** -0.5

    qa = qa_ref[...].astype(F32)
    qa_n = qa * lax.rsqrt(jnp.mean(qa * qa, axis=-1, keepdims=True) + EPS) * qag_ref[...]
    q = _dot(qa_n.astype(BF16), wqb_ref[...])
    qng = qng_ref[...]
    for h in range(MLA_HEADS):
        qh = q[:, h * MLA_QK_PAD:(h + 1) * MLA_QK_PAD]
        rstd = lax.rsqrt(jnp.sum(qh * qh, axis=-1, keepdims=True) * (1.0 / MLA_QK) + EPS)
        qh = qh * rstd * qng
        q_out[:, h * MLA_QK_PAD:h * MLA_QK_PAD + LANES] = (qh[:, :LANES] * scale).astype(q_out.dtype)
        q_out[:, h * MLA_QK_PAD + LANES:(h + 1) * MLA_QK_PAD] = (
            _rope(qh[:, LANES:], cos_t, sin_a, sin_b) * scale).astype(q_out.dtype)

    kva = kva_ref[...].astype(F32)
    kva_n = kva * lax.rsqrt(jnp.mean(kva * kva, axis=-1, keepdims=True) + EPS) * kvag_ref[...]
    kv = _dot(kva_n.astype(BF16), wkvb_ref[...])
    lane = lax.broadcasted_iota(jnp.int32, misc_ref.shape, 1)
    kr = jnp.where(lane < MLA_ROPE, misc_ref[...].astype(F32), 0.0)
    kr_ss = jnp.sum(kr * kr, axis=-1, keepdims=True)
    kng = kng_ref[...]
    kr_rot = _rope(kr * kng[:, LANES:], cos_t, sin_a, sin_b)
    for h in range(MLA_HEADS):
        k_nope = kv[:, h * 256:h * 256 + LANES]
        rstd = lax.rsqrt((jnp.sum(k_nope * k_nope, axis=-1, keepdims=True) + kr_ss) * (1.0 / MLA_QK) + EPS)
        k_out[:, h * MLA_QK_PAD:h * MLA_QK_PAD + LANES] = (k_nope * rstd * kng[:, :LANES]).astype(k_out.dtype)
        k_out[:, h * MLA_QK_PAD + LANES:(h + 1) * MLA_QK_PAD] = (kr_rot * rstd).astype(k_out.dtype)
        v_out[:, h * MLA_V:(h + 1) * MLA_V] = kv[:, h * 256 + LANES:(h + 1) * 256].astype(v_out.dtype)


def mla_prep(z, cos_t, sin_a, sin_b, qag, wqb, kvag, wkvb, qng, kng, batch, seq, tm=512):
    t = batch * seq
    tm = min(tm, seq)
    nb = seq // tm
    full = lambda shape: pl.BlockSpec(shape, lambda i: (0,) * len(shape))
    tab = pl.BlockSpec((tm, LANES), lambda i: (i % nb, 0))
    return pl.pallas_call(
        _mla_prep_kernel,
        out_shape=(
            jax.ShapeDtypeStruct((t, MLA_HEADS * MLA_QK_PAD), BF16),
            jax.ShapeDtypeStruct((t, MLA_HEADS * MLA_QK_PAD), BF16),
            jax.ShapeDtypeStruct((t, MLA_HEADS * MLA_V), BF16),
        ),
        grid=(t // tm,),
        in_specs=[
            pl.BlockSpec((tm, MLA_Q_RANK), lambda i: (i, COL_MQA // MLA_Q_RANK)),
            pl.BlockSpec((tm, MLA_KV_RANK), lambda i: (i, COL_MKVA // MLA_KV_RANK)),
            pl.BlockSpec((tm, LANES), lambda i: (i, COL_MISC // LANES)),
            tab, tab, tab,
            full((1, MLA_Q_RANK)), full((MLA_Q_RANK, MLA_HEADS * MLA_QK_PAD)),
            full((1, MLA_KV_RANK)), full((MLA_KV_RANK, MLA_HEADS * 256)),
            full((1, MLA_QK_PAD)), full((1, MLA_QK_PAD)),
        ],
        out_specs=(
            pl.BlockSpec((tm, MLA_HEADS * MLA_QK_PAD), lambda i: (i, 0)),
            pl.BlockSpec((tm, MLA_HEADS * MLA_QK_PAD), lambda i: (i, 0)),
            pl.BlockSpec((tm, MLA_HEADS * MLA_V), lambda i: (i, 0)),
        ),
        compiler_params=_cparams("parallel"),
        name="mla_prep",
    )(z, z, z, cos_t, sin_a, sin_b, qag, wqb, kvag, wkvb, qng, kng)


def _attn_kernel(q_ref, k_ref, v_ref, o_ref, vp_sc, m_sc, acc_sc, s_sc, *, tk):
    nk = k_ref.shape[0] // tk

    @pl.when(pl.program_id(2) == 0)
    def _():
        vp_sc[:, :MLA_V] = v_ref[...]
        vp_sc[:, MLA_V:] = jnp.ones((vp_sc.shape[0], MLA_V), vp_sc.dtype)

    q = q_ref[...]
    m_sc[...] = jnp.full_like(m_sc, -jnp.inf)
    acc_sc[...] = jnp.zeros_like(acc_sc)

    def scores(j):
        return _dot_nt(q, k_ref[j * tk:(j + 1) * tk, :])

    def consume(slot, j):
        s = s_sc[slot]
        m_prev = m_sc[...]
        m_new = jnp.maximum(m_prev, jnp.max(s, axis=-1, keepdims=True))
        p = jnp.exp2(s - jnp.concatenate([m_new] * (tk // LANES), axis=1))
        alpha = jnp.exp2(m_prev - m_new)
        pv = _dot(p.astype(BF16), vp_sc[j * tk:(j + 1) * tk, :])
        acc_sc[...] = jnp.concatenate([alpha, alpha], axis=1) * acc_sc[...] + pv
        m_sc[...] = m_new

    s_sc[0] = scores(0)
    for j in range(nk):
        if j + 1 < nk:
            s_sc[(j + 1) % 2] = scores(j + 1)
        consume(j % 2, j)
    acc = acc_sc[...]
    o_ref[...] = (acc[:, :MLA_V] / acc[:, MLA_V:]).astype(o_ref.dtype)


def attention(q, k, v, batch, seq, tq=1024, tk=512):
    t = batch * seq
    tq = min(tq, seq)
    tk = min(tk, seq // 2)
    nq = seq // tq
    return pl.pallas_call(
        functools.partial(_attn_kernel, tk=tk),
        out_shape=jax.ShapeDtypeStruct((t, MLA_HEADS * MLA_V), BF16),
        grid=(batch, MLA_HEADS, nq),
        in_specs=[
            pl.BlockSpec((tq, MLA_QK_PAD), lambda b, h, i: (b * nq + i, h)),
            pl.BlockSpec((seq, MLA_QK_PAD), lambda b, h, i: (b, h)),
            pl.BlockSpec((seq, MLA_V), lambda b, h, i: (b, h)),
        ],
        out_specs=pl.BlockSpec((tq, MLA_V), lambda b, h, i: (b * nq + i, h)),
        scratch_shapes=[
            pltpu.VMEM((seq, 2 * MLA_V), BF16),
            pltpu.VMEM((tq, LANES), F32),
            pltpu.VMEM((tq, 2 * MLA_V), F32),
            pltpu.VMEM((2, tq, tk), F32),
        ],
        compiler_params=_cparams("parallel", "parallel", "arbitrary"),
        name="mla_attention",
    )(q, k, v)


def _split2(x):
    hi = x.astype(BF16)
    return hi, (x - hi.astype(F32)).astype(BF16)


def _merge_kernel(*refs, with_router):
    (x_ref, gates_ref, og_ref, rg_ref, gf_ref, gb_ref, lf_ref, lb_ref, am_ref,
     gng_ref, wga_ref, wlr_ref, wml_ref, wo_ref, fng_ref) = refs[:15]
    if with_router:
        wr_ref, br_ref, x_out, h_out, comb_out = refs[15:]
    else:
        x_out, h_out = refs[15:]

    o = gf_ref[...].astype(F32) + gb_ref[...].astype(F32)
    gng = gng_ref[...]
    parts = []
    for h in range(GLA_HEADS):
        oh = o[:, h * GLA_DV:(h + 1) * GLA_DV]
        rstd = lax.rsqrt(jnp.mean(oh * oh, axis=-1, keepdims=True) + EPS)
        parts.append(oh * rstd * gng[:, h * GLA_DV:(h + 1) * GLA_DV])
    og = og_ref[...].astype(F32)
    ya_in = jnp.concatenate(parts, axis=-1) * (og * _sigmoid(og))
    y_a = _dot(ya_in.astype(BF16), wga_ref[...])

    rg = rg_ref[...].astype(F32)
    gelu = 0.5 * rg * (1.0 + jnp.tanh(0.7978845608028654 * (rg + 0.044715 * rg * rg * rg)))
    yb_in = (lf_ref[...].astype(F32) + lb_ref[...].astype(F32)) * gelu
    y_b = _dot(yb_in.astype(BF16), wlr_ref[...])

    y_c = _dot(am_ref[...], wml_ref[...])

    d = D_MODEL
    merged = 0.5 * ((jnp.tanh(gates_ref[:, 0:d].astype(F32)) * y_a
                     + jnp.tanh(gates_ref[:, d:2 * d].astype(F32)) * y_b
                     + jnp.tanh(gates_ref[:, 2 * d:3 * d].astype(F32)) * y_c)
                    + (y_a + y_b + y_c))
    x_new = x_ref[...] + _dot(merged.astype(BF16), wo_ref[...])
    x_out[...] = x_new
    h = x_new * lax.rsqrt(jnp.mean(x_new * x_new, axis=-1, keepdims=True) + EPS) * fng_ref[...]
    h_out[...] = h.astype(h_out.dtype)

    if with_router:
        h_hi, h_mid = _split2(h)
        w_parts = jnp.concatenate(_split2(wr_ref[...]), axis=1)
        part_a = _dot(h_hi, w_parts)
        part_b = _dot(h_mid, w_parts)
        logits = ((part_a[:, :LANES] + part_a[:, LANES:]) + (part_b[:, :LANES] + part_b[:, LANES:])) + br_ref[...]
        lane = lax.broadcasted_iota(jnp.int32, logits.shape, 1)
        logits = jnp.where(lane < N_EXPERTS, logits, -jnp.inf)
        v1 = jnp.max(logits, axis=-1, keepdims=True)
        i1 = jnp.min(jnp.where(logits == v1, lane, LANES), axis=-1, keepdims=True)
        rest = jnp.where(lane == i1, -jnp.inf, logits)
        v2 = jnp.max(rest, axis=-1, keepdims=True)
        i2 = jnp.min(jnp.where(rest == v2, lane, LANES), axis=-1, keepdims=True)
        e2 = jnp.exp(v2 - v1)
        w1 = 1.0 / (1.0 + e2)
        comb_out[...] = (jnp.where(lane == i1, w1, 0.0) + jnp.where(lane == i2, e2 * w1, 0.0)
                         + jnp.where(lane == i1 + ROUTE_TOP1, 1.0, 0.0) + jnp.where(lane == i2 + ROUTE_TOP2, 1.0, 0.0))


def merge(x2d, z, gla_f, gla_b, lru_f, lru_b, attn, gng, wga, wlr, wml, wo, fng, router=None, tm=512):
    t = x2d.shape[0]
    tm = min(tm, t)
    full = lambda shape: pl.BlockSpec(shape, lambda i: (0,) * len(shape))
    row512 = pl.BlockSpec((tm, 512), lambda i: (i, 0))
    in_specs = [
        pl.BlockSpec((tm, D_MODEL), lambda i: (i, 0)),
        pl.BlockSpec((tm, 3 * D_MODEL), lambda i: (i, COL_GATES // (3 * D_MODEL))),
        pl.BlockSpec((tm, 512), lambda i: (i, COL_GOG // 512)),
        pl.BlockSpec((tm, 512), lambda i: (i, COL_RG // 512)),
        row512, row512, row512, row512, row512,
        full((1, 512)), full((512, D_MODEL)), full((512, D_MODEL)), full((512, D_MODEL)),
        full((D_MODEL, D_MODEL)), full((1, D_MODEL)),
    ]
    args = [x2d, z, z, z, gla_f, gla_b, lru_f, lru_b, attn, gng, wga, wlr, wml, wo, fng]
    out_shape = [jax.ShapeDtypeStruct((t, D_MODEL), F32), jax.ShapeDtypeStruct((t, D_MODEL), BF16)]
    out_specs = [pl.BlockSpec((tm, D_MODEL), lambda i: (i, 0)), pl.BlockSpec((tm, D_MODEL), lambda i: (i, 0))]
    if router is not None:
        in_specs += [full((D_MODEL, LANES)), full((1, LANES))]
        args += list(router)
        out_shape.append(jax.ShapeDtypeStruct((t, LANES), F32))
        out_specs.append(pl.BlockSpec((tm, LANES), lambda i: (i, 0)))
    return pl.pallas_call(
        functools.partial(_merge_kernel, with_router=router is not None),
        out_shape=tuple(out_shape),
        grid=(t // tm,),
        in_specs=in_specs,
        out_specs=tuple(out_specs),
        compiler_params=_cparams("parallel"),
        name="merge_router" if router is not None else "merge",
    )(*args)


FFN_SLICE = 512


def _col_slices(total, step):
    return [(lo, min(lo + step, total)) for lo in range(0, total, step)]


def _ffn_kernel(h_ref, x_ref, w1_ref, w3_ref, w2_ref, o_ref):
    h = h_ref[...]
    y = x_ref[...]
    for lo, hi in _col_slices(w1_ref.shape[1], FFN_SLICE):
        a = _dot(h, w1_ref[:, lo:hi])
        t = (a * _sigmoid(a) * _dot(h, w3_ref[:, lo:hi])).astype(BF16)
        y = y + _dot(t, w2_ref[lo:hi, :])
    o_ref[...] = y


def ffn(h, x2d, w1, w3, w2, tm=1024):
    t = x2d.shape[0]
    tm = min(tm, t)
    resident = lambda shape: pl.BlockSpec(shape, lambda i: (0, 0), pipeline_mode=pl.Buffered(1))
    return pl.pallas_call(
        _ffn_kernel,
        out_shape=jax.ShapeDtypeStruct((t, D_MODEL), F32),
        grid=(t // tm,),
        in_specs=[
            pl.BlockSpec((tm, D_MODEL), lambda i: (i, 0)),
            pl.BlockSpec((tm, D_MODEL), lambda i: (i, 0)),
            resident((D_MODEL, D_FF)), resident((D_MODEL, D_FF)), resident((D_FF, D_MODEL)),
        ],
        out_specs=pl.BlockSpec((tm, D_MODEL), lambda i: (i, 0)),
        compiler_params=_cparams("parallel"),
        name="ffn",
    )(h, x2d, w1, w3, w2)


MOE_CHUNK = 128
MOE_PERM_ROWS = 512


def _moe_kernel(h_ref, x_ref, route_ref, w13_ref, w2_ref, o_ref,
                xs_sc, ys_sc, ws_sc, pos_sc, meta_sc, *, tb, rows):
    e = pl.program_id(1)
    n_perm = rows // MOE_PERM_ROWS

    def one_hot_rows(c):
        r = (lax.broadcasted_iota(jnp.int32, (MOE_PERM_ROWS, tb), 0) + c * MOE_PERM_ROWS).astype(F32)
        eq_a = r == pos_sc[0:1, :]
        eq_b = r == pos_sc[1:2, :]
        return eq_a, eq_b

    @pl.when(e == 0)
    def _():
        route_t = route_ref[...].T
        comb_t = route_t[0:N_EXPERTS, :]
        top1_t = route_t[ROUTE_TOP1:ROUTE_TOP1 + N_EXPERTS, :]
        top2_t = route_t[ROUTE_TOP2:ROUTE_TOP2 + N_EXPERTS, :]
        sel_t = top1_t + top2_t
        count = jnp.sum(sel_t, axis=1, keepdims=True)
        n_chunk = jnp.floor((count + (MOE_CHUNK - 1)) * (1.0 / MOE_CHUNK))
        base = []
        run = jnp.zeros((1, 1), F32)
        for ex in range(N_EXPERTS):
            base.append(run)
            run = run + n_chunk[ex:ex + 1, :]
        base_chunk = jnp.concatenate(base, axis=0)
        meta_sc[...] = jnp.concatenate(
            [jnp.broadcast_to(base_chunk, (N_EXPERTS, LANES)), jnp.broadcast_to(n_chunk, (N_EXPERTS, LANES))], axis=0)
        t_row = lax.broadcasted_iota(jnp.int32, (tb, tb), 0)
        t_col = lax.broadcasted_iota(jnp.int32, (tb, tb), 1)
        earlier = jnp.where(t_row < t_col, 1.0, 0.0).astype(BF16)
        rank_t = _dot(sel_t.astype(BF16), earlier)
        pos_t = base_chunk * float(MOE_CHUNK) + rank_t
        pos_sc[0:1, :] = jnp.sum(top1_t * pos_t, axis=0, keepdims=True)
        pos_sc[1:2, :] = jnp.sum(top2_t * pos_t, axis=0, keepdims=True)
        w_a = jnp.sum(top1_t * comb_t, axis=0, keepdims=True)
        w_b = jnp.sum(top2_t * comb_t, axis=0, keepdims=True)
        h = h_ref[...]
        for c in range(n_perm):
            eq_a, eq_b = one_hot_rows(c)
            sl = slice(c * MOE_PERM_ROWS, (c + 1) * MOE_PERM_ROWS)
            perm = jnp.where(eq_a, 1.0, jnp.where(eq_b, 1.0, 0.0)).astype(BF16)
            xs_sc[sl, :] = _dot(perm, h).astype(xs_sc.dtype)
            w_rows = jnp.sum(jnp.where(eq_a, w_a, 0.0) + jnp.where(eq_b, w_b, 0.0), axis=1, keepdims=True)
            ws_sc[sl, :] = jnp.broadcast_to(w_rows, (MOE_PERM_ROWS, LANES))
        ys_sc[...] = jnp.zeros_like(ys_sc)

    meta = meta_sc[...]
    sub = lax.broadcasted_iota(jnp.int32, meta.shape, 0)
    base_e = jnp.sum(jnp.where(sub == e, meta, 0.0)[:, 0:1]).astype(jnp.int32)
    n_e = jnp.sum(jnp.where(sub == e + N_EXPERTS, meta, 0.0)[:, 0:1]).astype(jnp.int32)

    def expert_rows(row0, n_rows):
        sl = pl.ds(row0, n_rows)
        xs = xs_sc[sl, :]
        ab = _dot(xs, w13_ref[...])
        a, b = ab[:, :EXPERT_FF], ab[:, EXPERT_FF:]
        hidden = a * _sigmoid(a) * b * ws_sc[sl, 0:1]
        ys_sc[sl, :] = _dot(hidden.astype(BF16), w2_ref[...]).astype(ys_sc.dtype)

    def pair(c, carry):
        expert_rows(pl.multiple_of((base_e + 2 * c) * MOE_CHUNK, MOE_CHUNK), 2 * MOE_CHUNK)
        return carry

    lax.fori_loop(0, n_e // 2, pair, 0)

    @pl.when(n_e % 2 == 1)
    def _():
        expert_rows(pl.multiple_of((base_e + n_e - 1) * MOE_CHUNK, MOE_CHUNK), MOE_CHUNK)

    @pl.when(e == N_EXPERTS - 1)
    def _():
        o_ref[...] = x_ref[...]
        for c in range(n_perm):
            eq_a, eq_b = one_hot_rows(c)
            perm = jnp.where(eq_a, 1.0, jnp.where(eq_b, 1.0, 0.0)).astype(BF16)
            o_ref[...] += _dot_tn(perm, ys_sc[c * MOE_PERM_ROWS:(c + 1) * MOE_PERM_ROWS, :])


def moe(h, x2d, route, w13, w2, tb=1024):
    t = x2d.shape[0]
    tb = min(tb, t)
    rows = -(-(2 * tb + N_EXPERTS * (MOE_CHUNK - 1)) // MOE_PERM_ROWS) * MOE_PERM_ROWS
    once = pl.Buffered(1)
    return pl.pallas_call(
        functools.partial(_moe_kernel, tb=tb, rows=rows),
        out_shape=jax.ShapeDtypeStruct((t, D_MODEL), F32),
        grid=(t // tb, N_EXPERTS),
        in_specs=[
            pl.BlockSpec((tb, D_MODEL), lambda i, e: (i, 0), pipeline_mode=once),
            pl.BlockSpec((tb, D_MODEL), lambda i, e: (i, 0), pipeline_mode=once),
            pl.BlockSpec((tb, LANES), lambda i, e: (i, 0)),
            pl.BlockSpec((None, D_MODEL, 2 * EXPERT_FF), lambda i, e: (e, 0, 0)),
            pl.BlockSpec((None, EXPERT_FF, D_MODEL), lambda i, e: (e, 0, 0)),
        ],
        out_specs=pl.BlockSpec((tb, D_MODEL), lambda i, e: (i, 0)),
        scratch_shapes=[
            pltpu.VMEM((rows, D_MODEL), BF16),
            pltpu.VMEM((rows, D_MODEL), BF16),
            pltpu.VMEM((rows, LANES), F32),
            pltpu.VMEM((8, tb), F32),
            pltpu.VMEM((2 * N_EXPERTS, LANES), F32),
        ],
        compiler_params=pltpu.CompilerParams(dimension_semantics=("parallel", "arbitrary"),
                                             vmem_limit_bytes=MOE_VMEM_LIMIT),
        name="moe",
    )(h, x2d, route, w13, w2)


def _pack_layer(layer, p):
    w_in = p["w_in"][layer]
    sizes = (256, 256, 512, 512, 32, 512, 512, 256, 128, 64, 3072)
    offs = [0]
    for s in sizes:
        offs.append(offs[-1] + s)
    seg = lambda n: w_in[:, offs[n]:offs[n + 1]]
    g_q, g_k, g_v, g_og, g_dec, r_x, r_gate, m_qa, m_kva, m_kr, gates = (seg(n) for n in range(11))
    w_packed = jnp.concatenate(
        [0.5 * gates, g_v, g_og, r_x, r_gate, g_q, g_k, m_qa, m_kva, m_kr, g_dec, jnp.zeros((D_MODEL, 32), F32)],
        axis=1).astype(BF16)

    wdec = jnp.zeros((2, LANES, 256), F32)
    for d in range(2):
        lo = MLA_ROPE + d * GLA_LOWRANK
        wdec = wdec.at[d, lo:lo + GLA_LOWRANK, :].set(p["gla_w_dec"][layer, d])
    bdec = p["gla_b_dec"][layer].reshape(2, 1, 256)

    def block_diag(w):
        eye = jnp.eye(LRU_BLOCKS, dtype=w.dtype)
        out = w[:, :, :, None, :] * eye[None, :, None, :, None]
        return out.reshape(2, LRU_WIDTH, LRU_WIDTH).astype(BF16)

    wqb = p["mla_w_qb"][layer].reshape(MLA_Q_RANK, MLA_HEADS, MLA_QK)
    wqb = jnp.pad(wqb, ((0, 0), (0, 0), (0, MLA_QK_PAD - MLA_QK))).reshape(MLA_Q_RANK, MLA_HEADS * MLA_QK_PAD)
    pad_qk = lambda g: jnp.pad(g, (0, MLA_QK_PAD - MLA_QK)).reshape(1, MLA_QK_PAD)
    return dict(
        w_in=w_packed,
        norm_mix_g=p["norm_mix_g"][layer].reshape(1, D_MODEL),
        wdec=wdec.astype(BF16), bdec=bdec,
        conv_w=p["lru_conv_w"][layer], conv_b=p["lru_conv_b"][layer].reshape(1, LRU_WIDTH),
        wa=block_diag(0.5 * p["lru_w_a"][layer]), ba=0.5 * p["lru_b_a"][layer].reshape(2, 1, LRU_WIDTH),
        wi=block_diag(0.5 * p["lru_w_i"][layer]), bi=0.5 * p["lru_b_i"][layer].reshape(2, 1, LRU_WIDTH),
        lam=p["lru_lambda"][layer].reshape(2, 1, LRU_WIDTH),
        qag=p["mla_qa_g"][layer].reshape(1, MLA_Q_RANK), wqb=wqb.astype(BF16),
        kvag=p["mla_kva_g"][layer].reshape(1, MLA_KV_RANK), wkvb=p["mla_w_kvb"][layer].astype(BF16),
        qng=pad_qk(p["mla_qn_g"][layer]), kng=pad_qk(p["mla_kn_g"][layer]),
        gng=p["gla_norm_g"][layer].reshape(1, GLA_HEADS * GLA_DV),
        wga=p["gla_w_out"][layer].astype(BF16), wlr=p["lru_w_out"][layer].astype(BF16),
        wml=p["mla_w_out"][layer].astype(BF16), wo=p["w_o"][layer].astype(BF16),
        fng=p["norm_ffn_g"][layer].reshape(1, D_MODEL),
    )


def _rope_tables(length):
    pos = jnp.arange(length, dtype=F32)
    inv = ROPE_THETA ** (-jnp.arange(0, MLA_ROPE, 2, dtype=F32) / MLA_ROPE)
    ang = pos[:, None] * inv[None, :]
    cos, sin = jnp.cos(ang), jnp.sin(ang)
    zero = jnp.zeros_like(cos)
    cos_t = jnp.concatenate([cos, cos, zero, zero], axis=1)
    sin_a = jnp.concatenate([-sin, zero, zero, zero], axis=1)
    sin_b = jnp.concatenate([zero, sin, zero, zero], axis=1)
    return cos_t, sin_a, sin_b


def _trunk(x, layers, ffn_params, moe_params):
    batch, seq, _ = x.shape
    x2d = x.reshape(batch * seq, D_MODEL)
    tables = _rope_tables(seq)
    for layer, lp in enumerate(layers):
        z = in_proj(x2d, lp["norm_mix_g"], lp["w_in"])
        gla_f, gla_b = gla_mixer(z, lp["wdec"], lp["bdec"], batch, seq)
        lru_f, lru_b = lru_mixer(z, lp["conv_w"], lp["conv_b"], lp["wa"], lp["ba"], lp["wi"], lp["bi"], lp["lam"],
                                 batch, seq)
        q, k, v = mla_prep(z, *tables, lp["qag"], lp["wqb"], lp["kvag"], lp["wkvb"], lp["qng"], lp["kng"], batch, seq)
        attn = attention(q, k, v, batch, seq)
        j = layer // 2
        branch = (gla_f, gla_b, lru_f, lru_b, attn, lp["gng"], lp["wga"], lp["wlr"], lp["wml"], lp["wo"], lp["fng"])
        if layer % 2 == 0:
            x2d, h = merge(x2d, z, *branch)
            w1, w3, w2 = ffn_params[j]
            x2d = ffn(h, x2d, w1, w3, w2)
        else:
            x2d, h, route = merge(x2d, z, *branch, router=moe_params[j][:2])
            x2d = moe(h, x2d, route, *moe_params[j][2:])
    return x2d.reshape(batch, seq, D_MODEL)


def _pack_ffn(ffn_w1, ffn_w3, ffn_w2):
    return [(ffn_w1[j].astype(BF16), ffn_w3[j].astype(BF16), ffn_w2[j].astype(BF16))
            for j in range(ffn_w1.shape[0])]


def _pack_moe(moe_w_router, moe_b_router, moe_w1, moe_w3, moe_w2):
    out = []
    for j in range(moe_w1.shape[0]):
        w_router = jnp.pad(moe_w_router[j], ((0, 0), (0, LANES - N_EXPERTS)))
        b_router = jnp.pad(moe_b_router[j], (0, LANES - N_EXPERTS)).reshape(1, LANES)
        w13 = jnp.concatenate([moe_w1[j].astype(BF16), moe_w3[j].astype(BF16)], axis=-1)
        out.append((w_router, b_router, w13, moe_w2[j].astype(BF16)))
    return out


def kernel(x_prompt, x_sample, norm_mix_g, w_in, gla_w_dec, gla_b_dec, gla_norm_g, gla_w_out, lru_conv_w, lru_conv_b, lru_w_a, lru_b_a, lru_w_i, lru_b_i, lru_lambda, lru_w_out, mla_qa_g, mla_w_qb, mla_kva_g, mla_w_kvb, mla_qn_g, mla_kn_g, mla_w_out, w_o, norm_ffn_g, ffn_w1, ffn_w3, ffn_w2, moe_w_router, moe_b_router, moe_w1, moe_w3, moe_w2):
    p = dict(norm_mix_g=norm_mix_g, w_in=w_in, gla_w_dec=gla_w_dec, gla_b_dec=gla_b_dec, gla_norm_g=gla_norm_g,
             gla_w_out=gla_w_out, lru_conv_w=lru_conv_w, lru_conv_b=lru_conv_b, lru_w_a=lru_w_a, lru_b_a=lru_b_a,
             lru_w_i=lru_w_i, lru_b_i=lru_b_i, lru_lambda=lru_lambda, lru_w_out=lru_w_out, mla_qa_g=mla_qa_g,
             mla_w_qb=mla_w_qb, mla_kva_g=mla_kva_g, mla_w_kvb=mla_w_kvb, mla_qn_g=mla_qn_g, mla_kn_g=mla_kn_g,
             mla_w_out=mla_w_out, w_o=w_o, norm_ffn_g=norm_ffn_g)
    depth = w_in.shape[0]
    layers = [_pack_layer(layer, p) for layer in range(depth)]
    ffn_params = _pack_ffn(ffn_w1, ffn_w3, ffn_w2)
    moe_params = _pack_moe(moe_w_router, moe_b_router, moe_w1, moe_w3, moe_w2)
    y_prompt = _trunk(x_prompt, layers, ffn_params, moe_params)
    y_sample = _trunk(x_sample, layers, ffn_params, moe_params)
    return (y_prompt, y_sample)
```

```python
import functools

import jax
import jax.numpy as jnp
from jax import lax
from jax.experimental import pallas as pl
from jax.experimental.pallas import tpu as pltpu

F32 = jnp.float32
BF16 = jnp.bfloat16

D_MODEL = 1024
EPS = 1e-6
GLA_HEADS = 4
GLA_DK = 64
GLA_DV = 128
GLA_LOWRANK = 16
GLA_GATE_NORM = 16.0
GLA_CHUNK = 64
LRU_WIDTH = 512
LRU_BLOCKS = 8
LRU_BLOCK = LRU_WIDTH // LRU_BLOCKS
LRU_C = 8.0
MLA_HEADS = 4
MLA_Q_RANK = 256
MLA_KV_RANK = 128
MLA_NOPE = 128
MLA_ROPE = 64
MLA_V = 128
MLA_QK = MLA_NOPE + MLA_ROPE
MLA_QK_PAD = 256
ROPE_THETA = 10000.0
D_FF = 2816
N_EXPERTS = 8
EXPERT_FF = 1408
LANES = 128
LOG2_E = 1.4426950408889634

COL_GATES = 0
COL_GV = 3072
COL_GOG = 3584
COL_RX = 4096
COL_RG = 4608
COL_GQ = 5120
COL_GK = 5376
COL_MQA = 5632
COL_MKVA = 5888
COL_MISC = 6016
Z_COLS = 6144

ROUTE_TOP1 = 8
ROUTE_TOP2 = 16

VMEM_LIMIT = 56 * 1024 * 1024
MOE_VMEM_LIMIT = 62 * 1024 * 1024


def _cparams(*sem):
    return pltpu.CompilerParams(dimension_semantics=sem, vmem_limit_bytes=VMEM_LIMIT)


def _sigmoid(x):
    return 1.0 / (1.0 + jnp.exp(-x))


def _dot(a, b):
    return jnp.dot(a, b, preferred_element_type=F32)


def _dot_nt(a, b):
    return lax.dot_general(a, b, (((1,), (1,)), ((), ())), preferred_element_type=F32)


def _dot_tn(a, b):
    return lax.dot_general(a, b, (((0,), (0,)), ((), ())), preferred_element_type=F32)


def _inproj_kernel(x_ref, g_ref, w_ref, z_ref, h_ref):
    @pl.when(pl.program_id(1) == 0)
    def _():
        x = x_ref[...]
        ms = jnp.mean(x * x, axis=-1, keepdims=True)
        h_ref[...] = (x * lax.rsqrt(ms + EPS) * g_ref[...]).astype(BF16)

    z_ref[...] = _dot(h_ref[...], w_ref[...]).astype(z_ref.dtype)


def in_proj(x2d, g, w_packed, tm=1024, tn=3072):
    t = x2d.shape[0]
    tm = min(tm, t)
    return pl.pallas_call(
        _inproj_kernel,
        out_shape=jax.ShapeDtypeStruct((t, Z_COLS), BF16),
        grid=(t // tm, Z_COLS // tn),
        in_specs=[
            pl.BlockSpec((tm, D_MODEL), lambda i, j: (i, 0)),
            pl.BlockSpec((1, D_MODEL), lambda i, j: (0, 0)),
            pl.BlockSpec((D_MODEL, tn), lambda i, j: (0, j)),
        ],
        out_specs=pl.BlockSpec((tm, tn), lambda i, j: (i, j)),
        scratch_shapes=[pltpu.VMEM((tm, D_MODEL), BF16)],
        compiler_params=_cparams("parallel", "arbitrary"),
        name="in_proj",
    )(x2d, g, w_packed)


def _gla_kernel(qf_ref, kf_ref, vf_ref, mf_ref, qb_ref, kb_ref, vb_ref, mb_ref, wdec_ref, bdec_ref,
                of_ref, ob_ref, s_ref, g_ref, qm_sc, km_sc, ki_sc, dec_sc, p_sc, u_sc, sb_sc, *, tl):
    c = GLA_CHUNK
    nc = tl // c

    @pl.when(pl.program_id(1) == 0)
    def _():
        s_ref[...] = jnp.zeros_like(s_ref)

    for d, m_ref in enumerate((mf_ref, mb_ref)):
        x = _dot(m_ref[...], wdec_ref[d]) + bdec_ref[d]
        logsig = jnp.minimum(x, 0.0) - jnp.log(1.0 + jnp.exp(-jnp.abs(x)))
        g_ref[d] = logsig * (1.0 / GLA_GATE_NORM)

    row = lax.broadcasted_iota(jnp.int32, (c, c), 0)
    col = lax.broadcasted_iota(jnp.int32, (c, c), 1)
    tri = (row >= col, row <= col)
    lane = lax.broadcasted_iota(jnp.int32, (c, LANES), 1)
    head_mask = (lane < GLA_DK, lane >= GLA_DK)
    dirs = ((qf_ref, kf_ref, vf_ref, of_ref), (qb_ref, kb_ref, vb_ref, ob_ref))

    ones_tri = [jnp.where(t, 1.0, 0.0).astype(BF16) for t in tri]
    work = [(d, cc) for cc in range(nc) for d in range(2)]
    rows = lambda cc: slice(cc * c, (cc + 1) * c)
    cols = lambda h: slice(h * GLA_DV, (h + 1) * GLA_DV)


    for d, cc in work:
        q_ref, k_ref, _, _ = dirs[d]
        g = g_ref[d, rows(cc), :]
        g_hi = g.astype(BF16)
        r1 = g - g_hi.astype(F32)
        g_mid = r1.astype(BF16)
        g_lo = (r1 - g_mid.astype(F32)).astype(BF16)
        b = _dot(ones_tri[d], g_hi) + _dot(ones_tri[d], g_mid) + _dot(ones_tri[d], g_lo)
        b_tot = b[c - 1:c, :] if d == 0 else b[0:1, :]
        qc = q_ref[rows(cc), :].astype(F32) * (GLA_DK ** -0.5)
        kc = k_ref[rows(cc), :].astype(F32)
        q_dec = qc * jnp.exp(b)
        k_inv = kc * jnp.exp(-b)
        k_end = kc * jnp.exp(b_tot - b)
        dec_sc[d, cc] = jnp.exp(b_tot)
        for p in range(GLA_HEADS // 2):
            sl = slice(p * LANES, (p + 1) * LANES)
            ki_sc[d, p, rows(cc), :] = k_inv[:, sl].astype(BF16)
            for hh in range(2):
                qm_sc[d, 2 * p + hh, rows(cc), :] = jnp.where(head_mask[hh], q_dec[:, sl], 0.0).astype(BF16)
                km_sc[d, 2 * p + hh, rows(cc), :] = jnp.where(head_mask[hh], k_end[:, sl], 0.0).astype(BF16)

    for d, cc in work:
        v_ref = dirs[d][2]
        for h in range(GLA_HEADS):
            scores = _dot_nt(qm_sc[d, h, rows(cc), :], ki_sc[d, h // 2, rows(cc), :])
            p_sc[d, h, rows(cc), :] = jnp.where(tri[d], scores, 0.0).astype(BF16)
            u_sc[d, h, cc] = _dot_tn(v_ref[rows(cc), cols(h)], km_sc[d, h, rows(cc), :])

    for h in range(GLA_HEADS):
        sl = slice((h // 2) * LANES, (h // 2 + 1) * LANES)
        for d in range(2):
            st = s_ref[d, h]
            for ci in range(nc):
                cc = ci if d == 0 else nc - 1 - ci
                sb_sc[d, h, cc] = st.astype(BF16)
                st = st * dec_sc[d, cc][:, sl] + u_sc[d, h, cc]
            s_ref[d, h] = st

    for d, cc in work:
        _, _, v_ref, o_ref = dirs[d]
        for h in range(GLA_HEADS):
            o_h = (_dot(p_sc[d, h, rows(cc), :], v_ref[rows(cc), cols(h)])
                   + _dot_nt(qm_sc[d, h, rows(cc), :], sb_sc[d, h, cc]))
            o_ref[rows(cc), cols(h)] = o_h.astype(o_ref.dtype)


def gla_mixer(z, wdec, bdec, batch, seq, tl=512):
    t = batch * seq
    tl = min(tl, seq)
    nb = seq // tl
    fwd = lambda b, i: b * nb + i
    bwd = lambda b, i: b * nb + (nb - 1 - i)

    def specs(rb):
        return [
            pl.BlockSpec((tl, 256), lambda b, i: (rb(b, i), COL_GQ // 256)),
            pl.BlockSpec((tl, 256), lambda b, i: (rb(b, i), COL_GK // 256)),
            pl.BlockSpec((tl, 512), lambda b, i: (rb(b, i), COL_GV // 512)),
            pl.BlockSpec((tl, LANES), lambda b, i: (rb(b, i), COL_MISC // LANES)),
        ]

    out_sd = jax.ShapeDtypeStruct((t, GLA_HEADS * GLA_DV), BF16)
    return pl.pallas_call(
        functools.partial(_gla_kernel, tl=tl),
        out_shape=(out_sd, out_sd),
        grid=(batch, nb),
        in_specs=specs(fwd) + specs(bwd) + [
            pl.BlockSpec((2, LANES, 256), lambda b, i: (0, 0, 0)),
            pl.BlockSpec((2, 1, 256), lambda b, i: (0, 0, 0)),
        ],
        out_specs=(
            pl.BlockSpec((tl, 512), lambda b, i: (fwd(b, i), 0)),
            pl.BlockSpec((tl, 512), lambda b, i: (bwd(b, i), 0)),
        ),
        scratch_shapes=[
            pltpu.VMEM((2, GLA_HEADS, GLA_DV, LANES), F32),
            pltpu.VMEM((2, tl, 256), F32),
            pltpu.VMEM((2, GLA_HEADS, tl, LANES), BF16),
            pltpu.VMEM((2, GLA_HEADS, tl, LANES), BF16),
            pltpu.VMEM((2, GLA_HEADS // 2, tl, LANES), BF16),
            pltpu.VMEM((2, tl // GLA_CHUNK, 1, 256), F32),
            pltpu.VMEM((2, GLA_HEADS, tl, GLA_CHUNK), BF16),
            pltpu.VMEM((2, GLA_HEADS, tl // GLA_CHUNK, GLA_DV, LANES), F32),
            pltpu.VMEM((2, GLA_HEADS, tl // GLA_CHUNK, GLA_DV, LANES), BF16),
        ],
        compiler_params=_cparams("parallel", "arbitrary"),
        name="gla_mixer",
    )(z, z, z, z, z, z, z, z, wdec, bdec)


HALO = 16
SCAN_TILE = 8


def _lru_kernel(xf_ref, xfp_ref, xfn_ref, xb_ref, xbp_ref, xbn_ref, cw_ref, cb_ref, wa_ref, ba_ref,
                wi_ref, bi_ref, lam_ref, hf_ref, hb_ref, h_sc, ac_sc, hc_sc, cin_sc, carry_sc, *, tl, nb):
    i = pl.program_id(1)
    st = SCAN_TILE
    nt = tl // st
    ng = LRU_WIDTH // LANES

    @pl.when(i == 0)
    def _():
        carry_sc[...] = jnp.zeros_like(carry_sc)

    blocks = ((xf_ref, xfp_ref, xfn_ref, i), (xb_ref, xbp_ref, xbn_ref, nb - 1 - i))

    tile_row = lax.broadcasted_iota(jnp.int32, (nt, LRU_WIDTH), 0)
    cw = cw_ref[...]
    conv_bias = cb_ref[...]

    for d, (x_ref, xp_ref, xn_ref, li) in enumerate(blocks):
        x = x_ref[...].astype(F32)
        for g in range(ng):
            h_sc[d, g] = x[:, g * LANES:(g + 1) * LANES]
        xr = [jnp.concatenate([h_sc[d, g, pl.ds(r, nt, stride=st), :] for g in range(ng)], axis=1)
              for r in range(st)]
        prev = jnp.where(li > 0, xp_ref[...].astype(F32), 0.0)
        nxt = jnp.where(li < nb - 1, xn_ref[...].astype(F32), 0.0)

        def from_prev_tile(a, first):
            return jnp.where(tile_row == 0, first, pltpu.roll(a, 1, 0))

        def from_next_tile(a, last):
            return jnp.where(tile_row == nt - 1, last, pltpu.roll(a, nt - 1, 0))

        neighbours = {-2: from_prev_tile(xr[st - 2], prev[HALO - 2:HALO - 1, :]),
                      -1: from_prev_tile(xr[st - 1], prev[HALO - 1:HALO, :]),
                      st: from_next_tile(xr[0], nxt[0:1, :])}
        at = lambda r: xr[r] if 0 <= r < st else neighbours[r]
        u = jnp.concatenate(
            [cw[0:1, :] * at(r - 2) + cw[1:2, :] * at(r - 1) + cw[2:3, :] * at(r) + cw[3:4, :] * at(r + 1) + conv_bias
             for r in range(st)], axis=0)
        ub = u.astype(BF16)
        r_tanh = jnp.tanh(_dot(ub, wa_ref[d]) + ba_ref[d])
        i_tanh = jnp.tanh(_dot(ub, wi_ref[d]) + bi_ref[d])
        lam = lam_ref[d]
        softplus_neg = jnp.maximum(-lam, 0.0) + jnp.log(1.0 + jnp.exp(-jnp.abs(lam)))
        half_rate = (-0.5 * LRU_C * LOG2_E) * softplus_neg
        a = jnp.exp2(r_tanh * half_rate + half_rate)
        y = 1.0 - a * a
        root = y * lax.rsqrt(jnp.maximum(y, 1e-30))
        hh = (root * u) * (0.5 * i_tanh + 0.5)

        order = range(st) if d == 0 else range(st - 1, -1, -1)
        a_run = h_run = None
        for r in order:
            a_r, x_r = a[r * nt:(r + 1) * nt, :], hh[r * nt:(r + 1) * nt, :]
            if a_run is None:
                a_run, h_run = a_r, x_r
            else:
                h_run = a_r * h_run + x_r
                a_run = a_r * a_run
            ac_sc[d, r] = a_run
            hc_sc[d, r] = h_run

    carry = [carry_sc[0], carry_sc[1]]
    ends = (st - 1, 0)
    for step in range(nt):
        for d in range(2):
            j = step if d == 0 else nt - 1 - step
            cin_sc[d, j:j + 1, :] = carry[d]
            carry[d] = ac_sc[d, ends[d], j:j + 1, :] * carry[d] + hc_sc[d, ends[d], j:j + 1, :]
    carry_sc[0] = carry[0]
    carry_sc[1] = carry[1]

    for d, o_ref in enumerate((hf_ref, hb_ref)):
        cin = cin_sc[d]
        for r in range(st):
            h_r = ac_sc[d, r] * cin + hc_sc[d, r]
            for g in range(ng):
                h_sc[d, g, pl.ds(r, nt, stride=st), :] = h_r[:, g * LANES:(g + 1) * LANES]
        for g in range(ng):
            o_ref[:, g * LANES:(g + 1) * LANES] = h_sc[d, g].astype(o_ref.dtype)


def lru_mixer(z, conv_w, conv_b, wa, ba, wi, bi, lam, batch, seq, tl=1024):
    t = batch * seq
    tl = min(tl, seq)
    nb = seq // tl
    hpb = tl // HALO
    last_halo = t // HALO - 1
    fwd = lambda b, i: b * nb + i
    bwd = lambda b, i: b * nb + (nb - 1 - i)
    colx = COL_RX // LRU_WIDTH

    def specs(rb):
        return [
            pl.BlockSpec((tl, LRU_WIDTH), lambda b, i: (rb(b, i), colx)),
            pl.BlockSpec((HALO, LRU_WIDTH), lambda b, i: (jnp.maximum(rb(b, i) * hpb - 1, 0), colx)),
            pl.BlockSpec((HALO, LRU_WIDTH), lambda b, i: (jnp.minimum((rb(b, i) + 1) * hpb, last_halo), colx)),
        ]

    full = lambda shape: pl.BlockSpec(shape, lambda b, i: (0,) * len(shape))
    out_sd = jax.ShapeDtypeStruct((t, LRU_WIDTH), BF16)
    return pl.pallas_call(
        functools.partial(_lru_kernel, tl=tl, nb=nb),
        out_shape=(out_sd, out_sd),
        grid=(batch, nb),
        in_specs=specs(fwd) + specs(bwd) + [
            full((4, LRU_WIDTH)), full((1, LRU_WIDTH)),
            full((2, LRU_WIDTH, LRU_WIDTH)), full((2, 1, LRU_WIDTH)),
            full((2, LRU_WIDTH, LRU_WIDTH)), full((2, 1, LRU_WIDTH)),
            full((2, 1, LRU_WIDTH)),
        ],
        out_specs=(
            pl.BlockSpec((tl, LRU_WIDTH), lambda b, i: (fwd(b, i), 0)),
            pl.BlockSpec((tl, LRU_WIDTH), lambda b, i: (bwd(b, i), 0)),
        ),
        scratch_shapes=[
            pltpu.VMEM((2, LRU_WIDTH // LANES, tl, LANES), F32),
            pltpu.VMEM((2, SCAN_TILE, tl // SCAN_TILE, LRU_WIDTH), F32),
            pltpu.VMEM((2, SCAN_TILE, tl // SCAN_TILE, LRU_WIDTH), F32),
            pltpu.VMEM((2, tl // SCAN_TILE, LRU_WIDTH), F32),
            pltpu.VMEM((2, 1, LRU_WIDTH), F32),
        ],
        compiler_params=_cparams("parallel", "arbitrary"),
        name="lru_mixer",
    )(z, z, z, z, z, z, conv_w, conv_b, wa, ba, wi, bi, lam)


def _rope(x, cos_t, sin_a, sin_b):
    return x * cos_t + pltpu.roll(x, LANES - MLA_ROPE // 2, 1) * sin_a + pltpu.roll(x, MLA_ROPE // 2, 1) * sin_b


def _mla_prep_kernel(qa_ref, kva_ref, misc_ref, cos_ref, sa_ref, sb_ref, qag_ref, wqb_ref, kvag_ref, wkvb_ref,
                     qng_ref, kng_ref, q_out, k_out, v_out):
    cos_t, sin_a, sin_b = cos_ref[...], sa_ref[...], sb_ref[...]
    scale = LOG2_E * MLA_QK ** -0.5

    qa = qa_ref[...].astype(F32)
    qa_n = qa * lax.rsqrt(jnp.mean(qa * qa, axis=-1, keepdims=True) + EPS) * qag_ref[...]
    q = _dot(qa_n.astype(BF16), wqb_ref[...])
    qng = qng_ref[...]
    for h in range(MLA_HEADS):
        qh = q[:, h * MLA_QK_PAD:(h + 1) * MLA_QK_PAD]
        rstd = lax.rsqrt(jnp.sum(qh * qh, axis=-1, keepdims=True) * (1.0 / MLA_QK) + EPS)
        qh = qh * rstd * qng
        q_out[:, h * MLA_QK_PAD:h * MLA_QK_PAD + LANES] = (qh[:, :LANES] * scale).astype(q_out.dtype)
        q_out[:, h * MLA_QK_PAD + LANES:(h + 1) * MLA_QK_PAD] = (
            _rope(qh[:, LANES:], cos_t, sin_a, sin_b) * scale).astype(q_out.dtype)

    kva = kva_ref[...].astype(F32)
    kva_n = kva * lax.rsqrt(jnp.mean(kva * kva, axis=-1, keepdims=True) + EPS) * kvag_ref[...]
    kv = _dot(kva_n.astype(BF16), wkvb_ref[...])
    lane = lax.broadcasted_iota(jnp.int32, misc_ref.shape, 1)
    kr = jnp.where(lane < MLA_ROPE, misc_ref[...].astype(F32), 0.0)
    kr_ss = jnp.sum(kr * kr, axis=-1, keepdims=True)
    kng = kng_ref[...]
    kr_rot = _rope(kr * kng[:, LANES:], cos_t, sin_a, sin_b)
    for h in range(MLA_HEADS):
        k_nope = kv[:, h * 256:h * 256 + LANES]
        rstd = lax.rsqrt((jnp.sum(k_nope * k_nope, axis=-1, keepdims=True) + kr_ss) * (1.0 / MLA_QK) + EPS)
        k_out[:, h * MLA_QK_PAD:h * MLA_QK_PAD + LANES] = (k_nope * rstd * kng[:, :LANES]).astype(k_out.dtype)
        k_out[:, h * MLA_QK_PAD + LANES:(h + 1) * MLA_QK_PAD] = (kr_rot * rstd).astype(k_out.dtype)
        v_out[:, h * MLA_V:(h + 1) * MLA_V] = kv[:, h * 256 + LANES:(h + 1) * 256].astype(v_out.dtype)


def mla_prep(z, cos_t, sin_a, sin_b, qag, wqb, kvag, wkvb, qng, kng, batch, seq, tm=1024):
    t = batch * seq
    tm = min(tm, seq)
    nb = seq // tm
    full = lambda shape: pl.BlockSpec(shape, lambda i: (0,) * len(shape))
    tab = pl.BlockSpec((tm, LANES), lambda i: (i % nb, 0))
    return pl.pallas_call(
        _mla_prep_kernel,
        out_shape=(
            jax.ShapeDtypeStruct((t, MLA_HEADS * MLA_QK_PAD), BF16),
            jax.ShapeDtypeStruct((t, MLA_HEADS * MLA_QK_PAD), BF16),
            jax.ShapeDtypeStruct((t, MLA_HEADS * MLA_V), BF16),
        ),
        grid=(t // tm,),
        in_specs=[
            pl.BlockSpec((tm, MLA_Q_RANK), lambda i: (i, COL_MQA // MLA_Q_RANK)),
            pl.BlockSpec((tm, MLA_KV_RANK), lambda i: (i, COL_MKVA // MLA_KV_RANK)),
            pl.BlockSpec((tm, LANES), lambda i: (i, COL_MISC // LANES)),
            tab, tab, tab,
            full((1, MLA_Q_RANK)), full((MLA_Q_RANK, MLA_HEADS * MLA_QK_PAD)),
            full((1, MLA_KV_RANK)), full((MLA_KV_RANK, MLA_HEADS * 256)),
            full((1, MLA_QK_PAD)), full((1, MLA_QK_PAD)),
        ],
        out_specs=(
            pl.BlockSpec((tm, MLA_HEADS * MLA_QK_PAD), lambda i: (i, 0)),
            pl.BlockSpec((tm, MLA_HEADS * MLA_QK_PAD), lambda i: (i, 0)),
            pl.BlockSpec((tm, MLA_HEADS * MLA_V), lambda i: (i, 0)),
        ),
        compiler_params=_cparams("parallel"),
        name="mla_prep",
    )(z, z, z, cos_t, sin_a, sin_b, qag, wqb, kvag, wkvb, qng, kng)


def _attn_kernel(q_ref, k_ref, v_ref, o_ref, vp_sc, m_sc, acc_sc, s_sc, *, tk):
    nk = k_ref.shape[0] // tk
    nh = ATTN_HEADS_PER_STEP
    qcols = lambda hh: slice(hh * MLA_QK_PAD, (hh + 1) * MLA_QK_PAD)

    @pl.when(pl.program_id(2) == 0)
    def _():
        for hh in range(nh):
            vp_sc[hh, :, :MLA_V] = v_ref[:, hh * MLA_V:(hh + 1) * MLA_V]
            vp_sc[hh, :, MLA_V:] = jnp.ones((vp_sc.shape[1], MLA_V), vp_sc.dtype)

    def scores(hh, j):
        return _dot_nt(q_ref[:, qcols(hh)], k_ref[j * tk:(j + 1) * tk, qcols(hh)])

    def consume(hh, slot, j):
        s = s_sc[hh, slot]
        m_cur = jnp.max(s, axis=-1, keepdims=True)
        if j == 0:
            m_new = jnp.broadcast_to(m_cur, m_sc.shape[1:])
        else:
            m_prev = m_sc[hh]
            m_new = jnp.maximum(m_prev, m_cur)
        p = jnp.exp2(s - jnp.concatenate([m_new] * (tk // LANES), axis=1))
        pv = _dot(p.astype(BF16), vp_sc[hh, j * tk:(j + 1) * tk, :])
        if j == 0:
            acc_sc[hh] = pv
        else:
            alpha = jnp.exp2(m_prev - m_new)
            acc_sc[hh] = jnp.concatenate([alpha, alpha], axis=1) * acc_sc[hh] + pv
        m_sc[hh] = m_new

    for hh in range(nh):
        s_sc[hh, 0] = scores(hh, 0)
    for j in range(nk):
        for hh in range(nh):
            if j + 1 < nk:
                s_sc[hh, (j + 1) % 2] = scores(hh, j + 1)
            consume(hh, j % 2, j)
    for hh in range(nh):
        acc = acc_sc[hh]
        o_ref[:, hh * MLA_V:(hh + 1) * MLA_V] = (acc[:, :MLA_V] / acc[:, MLA_V:]).astype(o_ref.dtype)


ATTN_HEADS_PER_STEP = 2


def attention(q, k, v, batch, seq, tq=1024, tk=512):
    t = batch * seq
    tq = min(tq, seq)
    tk = min(tk, seq // 2)
    nq = seq // tq
    nh = ATTN_HEADS_PER_STEP
    return pl.pallas_call(
        functools.partial(_attn_kernel, tk=tk),
        out_shape=jax.ShapeDtypeStruct((t, MLA_HEADS * MLA_V), BF16),
        grid=(batch, MLA_HEADS // nh, nq),
        in_specs=[
            pl.BlockSpec((tq, nh * MLA_QK_PAD), lambda b, h, i: (b * nq + i, h)),
            pl.BlockSpec((seq, nh * MLA_QK_PAD), lambda b, h, i: (b, h)),
            pl.BlockSpec((seq, nh * MLA_V), lambda b, h, i: (b, h)),
        ],
        out_specs=pl.BlockSpec((tq, nh * MLA_V), lambda b, h, i: (b * nq + i, h)),
        scratch_shapes=[
            pltpu.VMEM((nh, seq, 2 * MLA_V), BF16),
            pltpu.VMEM((nh, tq, LANES), F32),
            pltpu.VMEM((nh, tq, 2 * MLA_V), F32),
            pltpu.VMEM((nh, 2, tq, tk), F32),
        ],
        compiler_params=_cparams("parallel", "parallel", "arbitrary"),
        name="mla_attention",
    )(q, k, v)


def _split2(x):
    hi = x.astype(BF16)
    return hi, (x - hi.astype(F32)).astype(BF16)


def _merge_kernel(*refs, with_router):
    (x_ref, gates_ref, og_ref, rg_ref, gf_ref, gb_ref, lf_ref, lb_ref, am_ref,
     gng_ref, wga_ref, wlr_ref, wml_ref, wo_ref, fng_ref) = refs[:15]
    if with_router:
        wr_ref, br_ref, x_out, h_out, comb_out = refs[15:]
    else:
        x_out, h_out = refs[15:]

    o = gf_ref[...].astype(F32) + gb_ref[...].astype(F32)
    gng = gng_ref[...]
    parts = []
    for h in range(GLA_HEADS):
        oh = o[:, h * GLA_DV:(h + 1) * GLA_DV]
        rstd = lax.rsqrt(jnp.mean(oh * oh, axis=-1, keepdims=True) + EPS)
        parts.append(oh * rstd * gng[:, h * GLA_DV:(h + 1) * GLA_DV])
    og = og_ref[...].astype(F32)
    ya_in = jnp.concatenate(parts, axis=-1) * (og * _sigmoid(og))
    y_a = _dot(ya_in.astype(BF16), wga_ref[...])

    rg = rg_ref[...].astype(F32)
    gelu = 0.5 * rg * (1.0 + jnp.tanh(0.7978845608028654 * (rg + 0.044715 * rg * rg * rg)))
    yb_in = (lf_ref[...].astype(F32) + lb_ref[...].astype(F32)) * gelu
    y_b = _dot(yb_in.astype(BF16), wlr_ref[...])

    y_c = _dot(am_ref[...], wml_ref[...])

    d = D_MODEL
    merged = 0.5 * ((jnp.tanh(gates_ref[:, 0:d].astype(F32)) * y_a
                     + jnp.tanh(gates_ref[:, d:2 * d].astype(F32)) * y_b
                     + jnp.tanh(gates_ref[:, 2 * d:3 * d].astype(F32)) * y_c)
                    + (y_a + y_b + y_c))
    x_new = x_ref[...] + _dot(merged.astype(BF16), wo_ref[...])
    x_out[...] = x_new
    h = x_new * lax.rsqrt(jnp.mean(x_new * x_new, axis=-1, keepdims=True) + EPS) * fng_ref[...]
    h_out[...] = h.astype(h_out.dtype)

    if with_router:
        h_hi, h_mid = _split2(h)
        w_parts = jnp.concatenate(_split2(wr_ref[...]), axis=1)
        part_a = _dot(h_hi, w_parts)
        part_b = _dot(h_mid, w_parts)
        logits = ((part_a[:, :LANES] + part_a[:, LANES:]) + (part_b[:, :LANES] + part_b[:, LANES:])) + br_ref[...]
        lane = lax.broadcasted_iota(jnp.int32, logits.shape, 1)
        logits = jnp.where(lane < N_EXPERTS, logits, -jnp.inf)
        v1 = jnp.max(logits, axis=-1, keepdims=True)
        i1 = jnp.min(jnp.where(logits == v1, lane, LANES), axis=-1, keepdims=True)
        rest = jnp.where(lane == i1, -jnp.inf, logits)
        v2 = jnp.max(rest, axis=-1, keepdims=True)
        i2 = jnp.min(jnp.where(rest == v2, lane, LANES), axis=-1, keepdims=True)
        e2 = jnp.exp(v2 - v1)
        w1 = 1.0 / (1.0 + e2)
        comb_out[...] = (jnp.where(lane == i1, w1, 0.0) + jnp.where(lane == i2, e2 * w1, 0.0)
                         + jnp.where(lane == i1 + ROUTE_TOP1, 1.0, 0.0) + jnp.where(lane == i2 + ROUTE_TOP2, 1.0, 0.0))


def merge(x2d, z, gla_f, gla_b, lru_f, lru_b, attn, gng, wga, wlr, wml, wo, fng, router=None, tm=512):
    t = x2d.shape[0]
    tm = min(tm, t)
    full = lambda shape: pl.BlockSpec(shape, lambda i: (0,) * len(shape))
    row512 = pl.BlockSpec((tm, 512), lambda i: (i, 0))
    in_specs = [
        pl.BlockSpec((tm, D_MODEL), lambda i: (i, 0)),
        pl.BlockSpec((tm, 3 * D_MODEL), lambda i: (i, COL_GATES // (3 * D_MODEL))),
        pl.BlockSpec((tm, 512), lambda i: (i, COL_GOG // 512)),
        pl.BlockSpec((tm, 512), lambda i: (i, COL_RG // 512)),
        row512, row512, row512, row512, row512,
        full((1, 512)), full((512, D_MODEL)), full((512, D_MODEL)), full((512, D_MODEL)),
        full((D_MODEL, D_MODEL)), full((1, D_MODEL)),
    ]
    args = [x2d, z, z, z, gla_f, gla_b, lru_f, lru_b, attn, gng, wga, wlr, wml, wo, fng]
    out_shape = [jax.ShapeDtypeStruct((t, D_MODEL), F32), jax.ShapeDtypeStruct((t, D_MODEL), BF16)]
    out_specs = [pl.BlockSpec((tm, D_MODEL), lambda i: (i, 0)), pl.BlockSpec((tm, D_MODEL), lambda i: (i, 0))]
    if router is not None:
        in_specs += [full((D_MODEL, LANES)), full((1, LANES))]
        args += list(router)
        out_shape.append(jax.ShapeDtypeStruct((t, LANES), F32))
        out_specs.append(pl.BlockSpec((tm, LANES), lambda i: (i, 0)))
    return pl.pallas_call(
        functools.partial(_merge_kernel, with_router=router is not None),
        out_shape=tuple(out_shape),
        grid=(t // tm,),
        in_specs=in_specs,
        out_specs=tuple(out_specs),
        compiler_params=_cparams("parallel"),
        name="merge_router" if router is not None else "merge",
    )(*args)


FFN_SLICE = 512


def _col_slices(total, step):
    return [(lo, min(lo + step, total)) for lo in range(0, total, step)]


def _ffn_kernel(h_ref, x_ref, w1_ref, w3_ref, w2_ref, o_ref):
    h = h_ref[...]
    y = x_ref[...]
    for lo, hi in _col_slices(w1_ref.shape[1], FFN_SLICE):
        a = _dot(h, w1_ref[:, lo:hi])
        t = (a * _sigmoid(a) * _dot(h, w3_ref[:, lo:hi])).astype(BF16)
        y = y + _dot(t, w2_ref[lo:hi, :])
    o_ref[...] = y


def ffn(h, x2d, w1, w3, w2, tm=1024):
    t = x2d.shape[0]
    tm = min(tm, t)
    resident = lambda shape: pl.BlockSpec(shape, lambda i: (0, 0), pipeline_mode=pl.Buffered(1))
    return pl.pallas_call(
        _ffn_kernel,
        out_shape=jax.ShapeDtypeStruct((t, D_MODEL), F32),
        grid=(t // tm,),
        in_specs=[
            pl.BlockSpec((tm, D_MODEL), lambda i: (i, 0)),
            pl.BlockSpec((tm, D_MODEL), lambda i: (i, 0)),
            resident((D_MODEL, D_FF)), resident((D_MODEL, D_FF)), resident((D_FF, D_MODEL)),
        ],
        out_specs=pl.BlockSpec((tm, D_MODEL), lambda i: (i, 0)),
        compiler_params=_cparams("parallel"),
        name="ffn",
    )(h, x2d, w1, w3, w2)


MOE_CHUNK = 128
MOE_PERM_ROWS = 512


def _moe_kernel(h_ref, x_ref, route_ref, w13_ref, w2_ref, o_ref,
                xs_sc, ys_sc, ws_sc, pos_sc, meta_sc, *, tb, rows):
    e = pl.program_id(1)
    n_perm = rows // MOE_PERM_ROWS

    def one_hot_rows(c):
        r = (lax.broadcasted_iota(jnp.int32, (MOE_PERM_ROWS, tb), 0) + c * MOE_PERM_ROWS).astype(F32)
        eq_a = r == pos_sc[0:1, :]
        eq_b = r == pos_sc[1:2, :]
        return eq_a, eq_b

    @pl.when(e == 0)
    def _():
        route_t = route_ref[...].T
        comb_t = route_t[0:N_EXPERTS, :]
        top1_t = route_t[ROUTE_TOP1:ROUTE_TOP1 + N_EXPERTS, :]
        top2_t = route_t[ROUTE_TOP2:ROUTE_TOP2 + N_EXPERTS, :]
        sel_t = top1_t + top2_t
        count = jnp.sum(sel_t, axis=1, keepdims=True)
        n_chunk = jnp.floor((count + (MOE_CHUNK - 1)) * (1.0 / MOE_CHUNK))
        base = []
        run = jnp.zeros((1, 1), F32)
        for ex in range(N_EXPERTS):
            base.append(run)
            run = run + n_chunk[ex:ex + 1, :]
        base_chunk = jnp.concatenate(base, axis=0)
        meta_sc[...] = jnp.concatenate(
            [jnp.broadcast_to(base_chunk, (N_EXPERTS, LANES)), jnp.broadcast_to(n_chunk, (N_EXPERTS, LANES))], axis=0)
        t_row = lax.broadcasted_iota(jnp.int32, (tb, tb), 0)
        t_col = lax.broadcasted_iota(jnp.int32, (tb, tb), 1)
        earlier = jnp.where(t_row < t_col, 1.0, 0.0).astype(BF16)
        rank_t = _dot(sel_t.astype(BF16), earlier)
        pos_t = base_chunk * float(MOE_CHUNK) + rank_t
        pos_sc[0:1, :] = jnp.sum(top1_t * pos_t, axis=0, keepdims=True)
        pos_sc[1:2, :] = jnp.sum(top2_t * pos_t, axis=0, keepdims=True)
        w_a = jnp.sum(top1_t * comb_t, axis=0, keepdims=True)
        w_b = jnp.sum(top2_t * comb_t, axis=0, keepdims=True)
        h = h_ref[...]
        for c in range(n_perm):
            eq_a, eq_b = one_hot_rows(c)
            sl = slice(c * MOE_PERM_ROWS, (c + 1) * MOE_PERM_ROWS)
            perm = jnp.where(eq_a, 1.0, jnp.where(eq_b, 1.0, 0.0)).astype(BF16)
            xs_sc[sl, :] = _dot(perm, h).astype(xs_sc.dtype)
            w_rows = jnp.sum(jnp.where(eq_a, w_a, 0.0) + jnp.where(eq_b, w_b, 0.0), axis=1, keepdims=True)
            ws_sc[sl, :] = jnp.broadcast_to(w_rows, (MOE_PERM_ROWS, LANES))
        ys_sc[...] = jnp.zeros_like(ys_sc)

    meta = meta_sc[...]
    sub = lax.broadcasted_iota(jnp.int32, meta.shape, 0)
    base_e = jnp.sum(jnp.where(sub == e, meta, 0.0)[:, 0:1]).astype(jnp.int32)
    n_e = jnp.sum(jnp.where(sub == e + N_EXPERTS, meta, 0.0)[:, 0:1]).astype(jnp.int32)

    def expert_rows(row0, n_rows):
        sl = pl.ds(row0, n_rows)
        xs = xs_sc[sl, :]
        ab = _dot(xs, w13_ref[...])
        a, b = ab[:, :EXPERT_FF], ab[:, EXPERT_FF:]
        hidden = a * _sigmoid(a) * b * ws_sc[sl, 0:1]
        ys_sc[sl, :] = _dot(hidden.astype(BF16), w2_ref[...]).astype(ys_sc.dtype)

    def pair(c, carry):
        expert_rows(pl.multiple_of((base_e + 2 * c) * MOE_CHUNK, MOE_CHUNK), 2 * MOE_CHUNK)
        return carry

    lax.fori_loop(0, n_e // 2, pair, 0)

    @pl.when(n_e % 2 == 1)
    def _():
        expert_rows(pl.multiple_of((base_e + n_e - 1) * MOE_CHUNK, MOE_CHUNK), MOE_CHUNK)

    @pl.when(e == N_EXPERTS - 1)
    def _():
        o_ref[...] = x_ref[...]
        for c in range(n_perm):
            eq_a, eq_b = one_hot_rows(c)
            perm = jnp.where(eq_a, 1.0, jnp.where(eq_b, 1.0, 0.0)).astype(BF16)
            o_ref[...] += _dot_tn(perm, ys_sc[c * MOE_PERM_ROWS:(c + 1) * MOE_PERM_ROWS, :])


def moe(h, x2d, route, w13, w2, tb=1024):
    t = x2d.shape[0]
    tb = min(tb, t)
    rows = -(-(2 * tb + N_EXPERTS * (MOE_CHUNK - 1)) // MOE_PERM_ROWS) * MOE_PERM_ROWS
    once = pl.Buffered(1)
    return pl.pallas_call(
        functools.partial(_moe_kernel, tb=tb, rows=rows),
        out_shape=jax.ShapeDtypeStruct((t, D_MODEL), F32),
        grid=(t // tb, N_EXPERTS),
        in_specs=[
            pl.BlockSpec((tb, D_MODEL), lambda i, e: (i, 0), pipeline_mode=once),
            pl.BlockSpec((tb, D_MODEL), lambda i, e: (i, 0), pipeline_mode=once),
            pl.BlockSpec((tb, LANES), lambda i, e: (i, 0)),
            pl.BlockSpec((None, D_MODEL, 2 * EXPERT_FF), lambda i, e: (e, 0, 0)),
            pl.BlockSpec((None, EXPERT_FF, D_MODEL), lambda i, e: (e, 0, 0)),
        ],
        out_specs=pl.BlockSpec((tb, D_MODEL), lambda i, e: (i, 0)),
        scratch_shapes=[
            pltpu.VMEM((rows, D_MODEL), BF16),
            pltpu.VMEM((rows, D_MODEL), BF16),
            pltpu.VMEM((rows, LANES), F32),
            pltpu.VMEM((8, tb), F32),
            pltpu.VMEM((2 * N_EXPERTS, LANES), F32),
        ],
        compiler_params=pltpu.CompilerParams(dimension_semantics=("parallel", "arbitrary"),
                                             vmem_limit_bytes=MOE_VMEM_LIMIT),
        name="moe",
    )(h, x2d, route, w13, w2)


def _pack_layer(layer, p):
    w_in = p["w_in"][layer]
    sizes = (256, 256, 512, 512, 32, 512, 512, 256, 128, 64, 3072)
    offs = [0]
    for s in sizes:
        offs.append(offs[-1] + s)
    seg = lambda n: w_in[:, offs[n]:offs[n + 1]]
    g_q, g_k, g_v, g_og, g_dec, r_x, r_gate, m_qa, m_kva, m_kr, gates = (seg(n) for n in range(11))
    w_packed = jnp.concatenate(
        [0.5 * gates, g_v, g_og, r_x, r_gate, g_q, g_k, m_qa, m_kva, m_kr, g_dec, jnp.zeros((D_MODEL, 32), F32)],
        axis=1).astype(BF16)

    wdec = jnp.zeros((2, LANES, 256), F32)
    for d in range(2):
        lo = MLA_ROPE + d * GLA_LOWRANK
        wdec = wdec.at[d, lo:lo + GLA_LOWRANK, :].set(p["gla_w_dec"][layer, d])
    bdec = p["gla_b_dec"][layer].reshape(2, 1, 256)

    def block_diag(w):
        eye = jnp.eye(LRU_BLOCKS, dtype=w.dtype)
        out = w[:, :, :, None, :] * eye[None, :, None, :, None]
        return out.reshape(2, LRU_WIDTH, LRU_WIDTH).astype(BF16)

    wqb = p["mla_w_qb"][layer].reshape(MLA_Q_RANK, MLA_HEADS, MLA_QK)
    wqb = jnp.pad(wqb, ((0, 0), (0, 0), (0, MLA_QK_PAD - MLA_QK))).reshape(MLA_Q_RANK, MLA_HEADS * MLA_QK_PAD)
    pad_qk = lambda g: jnp.pad(g, (0, MLA_QK_PAD - MLA_QK)).reshape(1, MLA_QK_PAD)
    return dict(
        w_in=w_packed,
        norm_mix_g=p["norm_mix_g"][layer].reshape(1, D_MODEL),
        wdec=wdec.astype(BF16), bdec=bdec,
        conv_w=p["lru_conv_w"][layer], conv_b=p["lru_conv_b"][layer].reshape(1, LRU_WIDTH),
        wa=block_diag(0.5 * p["lru_w_a"][layer]), ba=0.5 * p["lru_b_a"][layer].reshape(2, 1, LRU_WIDTH),
        wi=block_diag(0.5 * p["lru_w_i"][layer]), bi=0.5 * p["lru_b_i"][layer].reshape(2, 1, LRU_WIDTH),
        lam=p["lru_lambda"][layer].reshape(2, 1, LRU_WIDTH),
        qag=p["mla_qa_g"][layer].reshape(1, MLA_Q_RANK), wqb=wqb.astype(BF16),
        kvag=p["mla_kva_g"][layer].reshape(1, MLA_KV_RANK), wkvb=p["mla_w_kvb"][layer].astype(BF16),
        qng=pad_qk(p["mla_qn_g"][layer]), kng=pad_qk(p["mla_kn_g"][layer]),
        gng=p["gla_norm_g"][layer].reshape(1, GLA_HEADS * GLA_DV),
        wga=p["gla_w_out"][layer].astype(BF16), wlr=p["lru_w_out"][layer].astype(BF16),
        wml=p["mla_w_out"][layer].astype(BF16), wo=p["w_o"][layer].astype(BF16),
        fng=p["norm_ffn_g"][layer].reshape(1, D_MODEL),
    )


def _rope_tables(length):
    pos = jnp.arange(length, dtype=F32)
    inv = ROPE_THETA ** (-jnp.arange(0, MLA_ROPE, 2, dtype=F32) / MLA_ROPE)
    ang = pos[:, None] * inv[None, :]
    cos, sin = jnp.cos(ang), jnp.sin(ang)
    zero = jnp.zeros_like(cos)
    cos_t = jnp.concatenate([cos, cos, zero, zero], axis=1)
    sin_a = jnp.concatenate([-sin, zero, zero, zero], axis=1)
    sin_b = jnp.concatenate([zero, sin, zero, zero], axis=1)
    return cos_t, sin_a, sin_b


def _trunk(x, layers, ffn_params, moe_params):
    batch, seq, _ = x.shape
    x2d = x.reshape(batch * seq, D_MODEL)
    tables = _rope_tables(seq)
    for layer, lp in enumerate(layers):
        z = in_proj(x2d, lp["norm_mix_g"], lp["w_in"])
        gla_f, gla_b = gla_mixer(z, lp["wdec"], lp["bdec"], batch, seq)
        lru_f, lru_b = lru_mixer(z, lp["conv_w"], lp["conv_b"], lp["wa"], lp["ba"], lp["wi"], lp["bi"], lp["lam"],
                                 batch, seq)
        q, k, v = mla_prep(z, *tables, lp["qag"], lp["wqb"], lp["kvag"], lp["wkvb"], lp["qng"], lp["kng"], batch, seq)
        attn = attention(q, k, v, batch, seq)
        j = layer // 2
        branch = (gla_f, gla_b, lru_f, lru_b, attn, lp["gng"], lp["wga"], lp["wlr"], lp["wml"], lp["wo"], lp["fng"])
        if layer % 2 == 0:
            x2d, h = merge(x2d, z, *branch)
            w1, w3, w2 = ffn_params[j]
            x2d = ffn(h, x2d, w1, w3, w2)
        else:
            x2d, h, route = merge(x2d, z, *branch, router=moe_params[j][:2])
            x2d = moe(h, x2d, route, *moe_params[j][2:])
    return x2d.reshape(batch, seq, D_MODEL)


def _pack_ffn(ffn_w1, ffn_w3, ffn_w2):
    return [(ffn_w1[j].astype(BF16), ffn_w3[j].astype(BF16), ffn_w2[j].astype(BF16))
            for j in range(ffn_w1.shape[0])]


def _pack_moe(moe_w_router, moe_b_router, moe_w1, moe_w3, moe_w2):
    out = []
    for j in range(moe_w1.shape[0]):
        w_router = jnp.pad(moe_w_router[j], ((0, 0), (0, LANES - N_EXPERTS)))
        b_router = jnp.pad(moe_b_router[j], (0, LANES - N_EXPERTS)).reshape(1, LANES)
        w13 = jnp.concatenate([moe_w1[j].astype(BF16), moe_w3[j].astype(BF16)], axis=-1)
        out.append((w_router, b_router, w13, moe_w2[j].astype(BF16)))
    return out


def kernel(x_prompt, x_sample, norm_mix_g, w_in, gla_w_dec, gla_b_dec, gla_norm_g, gla_w_out, lru_conv_w, lru_conv_b, lru_w_a, lru_b_a, lru_w_i, lru_b_i, lru_lambda, lru_w_out, mla_qa_g, mla_w_qb, mla_kva_g, mla_w_kvb, mla_qn_g, mla_kn_g, mla_w_out, w_o, norm_ffn_g, ffn_w1, ffn_w3, ffn_w2, moe_w_router, moe_b_router, moe_w1, moe_w3, moe_w2):
    p = dict(norm_mix_g=norm_mix_g, w_in=w_in, gla_w_dec=gla_w_dec, gla_b_dec=gla_b_dec, gla_norm_g=gla_norm_g,
             gla_w_out=gla_w_out, lru_conv_w=lru_conv_w, lru_conv_b=lru_conv_b, lru_w_a=lru_w_a, lru_b_a=lru_b_a,
             lru_w_i=lru_w_i, lru_b_i=lru_b_i, lru_lambda=lru_lambda, lru_w_out=lru_w_out, mla_qa_g=mla_qa_g,
             mla_w_qb=mla_w_qb, mla_kva_g=mla_kva_g, mla_w_kvb=mla_w_kvb, mla_qn_g=mla_qn_g, mla_kn_g=mla_kn_g,
             mla_w_out=mla_w_out, w_o=w_o, norm_ffn_g=norm_ffn_g)
    depth = w_in.shape[0]
    layers = [_pack_layer(layer, p) for layer in range(depth)]
    ffn_params = _pack_ffn(ffn_w1, ffn_w3, ffn_w2)
    moe_params = _pack_moe(moe_w_router, moe_b_router, moe_w1, moe_w3, moe_w2)
    y_prompt = _trunk(x_prompt, layers, ffn_params, moe_params)
    y_sample = _trunk(x_sample, layers, ffn_params, moe_params)
    return (y_prompt, y_sample)
```

```python
import functools

import jax
import jax.numpy as jnp
from jax import lax
from jax.experimental import pallas as pl
from jax.experimental.pallas import tpu as pltpu

F32 = jnp.float32
BF16 = jnp.bfloat16

D_MODEL = 1024
EPS = 1e-6
GLA_HEADS = 4
GLA_DK = 64
GLA_DV = 128
GLA_LOWRANK = 16
GLA_GATE_NORM = 16.0
GLA_CHUNK = 64
LRU_WIDTH = 512
LRU_BLOCKS = 8
LRU_BLOCK = LRU_WIDTH // LRU_BLOCKS
LRU_C = 8.0
MLA_HEADS = 4
MLA_Q_RANK = 256
MLA_KV_RANK = 128
MLA_NOPE = 128
MLA_ROPE = 64
MLA_V = 128
MLA_QK = MLA_NOPE + MLA_ROPE
MLA_QK_PAD = 256
ROPE_THETA = 10000.0
D_FF = 2816
N_EXPERTS = 8
EXPERT_FF = 1408
LANES = 128
LOG2_E = 1.4426950408889634

COL_GV = 0
COL_GOG = 512
COL_RX = 1024
COL_RG = 1536
COL_GQ = 2048
COL_GK = 2304
COL_MQA = 2560
COL_MKVA = 2816
COL_MISC = 2944
Z_COLS = 3072

ROUTE_TOP1 = 8
ROUTE_TOP2 = 16

VMEM_LIMIT = 56 * 1024 * 1024
MOE_VMEM_LIMIT = 62 * 1024 * 1024


def _cparams(*sem):
    return pltpu.CompilerParams(dimension_semantics=sem, vmem_limit_bytes=VMEM_LIMIT)


def _sigmoid(x):
    return 1.0 / (1.0 + jnp.exp(-x))


def _dot(a, b):
    return jnp.dot(a, b, preferred_element_type=F32)


def _dot_nt(a, b):
    return lax.dot_general(a, b, (((1,), (1,)), ((), ())), preferred_element_type=F32)


def _dot_tn(a, b):
    return lax.dot_general(a, b, (((0,), (0,)), ((), ())), preferred_element_type=F32)


def _inproj_kernel(x_ref, g_ref, w_ref, z_ref, h_ref):
    @pl.when(pl.program_id(1) == 0)
    def _():
        x = x_ref[...]
        ms = jnp.mean(x * x, axis=-1, keepdims=True)
        h_ref[...] = (x * lax.rsqrt(ms + EPS) * g_ref[...]).astype(BF16)

    z_ref[...] = _dot(h_ref[...], w_ref[...]).astype(z_ref.dtype)


def in_proj(x2d, g, w_packed, tm=1024, tn=3072):
    t = x2d.shape[0]
    tm = min(tm, t)
    return pl.pallas_call(
        _inproj_kernel,
        out_shape=jax.ShapeDtypeStruct((t, Z_COLS), BF16),
        grid=(t // tm, Z_COLS // tn),
        in_specs=[
            pl.BlockSpec((tm, D_MODEL), lambda i, j: (i, 0)),
            pl.BlockSpec((1, D_MODEL), lambda i, j: (0, 0)),
            pl.BlockSpec((D_MODEL, tn), lambda i, j: (0, j)),
        ],
        out_specs=pl.BlockSpec((tm, tn), lambda i, j: (i, j)),
        scratch_shapes=[pltpu.VMEM((tm, D_MODEL), BF16)],
        compiler_params=_cparams("parallel", "arbitrary"),
        name="in_proj",
    )(x2d, g, w_packed)


def _gla_kernel(qf_ref, kf_ref, vf_ref, mf_ref, qb_ref, kb_ref, vb_ref, mb_ref, wdec_ref, bdec_ref,
                of_ref, ob_ref, s_ref, g_ref, qm_sc, km_sc, ki_sc, dec_sc, p_sc, u_sc, sb_sc, *, tl):
    c = GLA_CHUNK
    nc = tl // c

    @pl.when(pl.program_id(1) == 0)
    def _():
        s_ref[...] = jnp.zeros_like(s_ref)

    for d, m_ref in enumerate((mf_ref, mb_ref)):
        x = _dot(m_ref[...], wdec_ref[d]) + bdec_ref[d]
        logsig = jnp.minimum(x, 0.0) - jnp.log(1.0 + jnp.exp(-jnp.abs(x)))
        g_ref[d] = logsig * (1.0 / GLA_GATE_NORM)

    row = lax.broadcasted_iota(jnp.int32, (c, c), 0)
    col = lax.broadcasted_iota(jnp.int32, (c, c), 1)
    tri = (row >= col, row <= col)
    lane = lax.broadcasted_iota(jnp.int32, (c, LANES), 1)
    head_mask = (lane < GLA_DK, lane >= GLA_DK)
    dirs = ((qf_ref, kf_ref, vf_ref, of_ref), (qb_ref, kb_ref, vb_ref, ob_ref))

    ones_tri = [jnp.where(t, 1.0, 0.0).astype(BF16) for t in tri]
    work = [(d, cc) for cc in range(nc) for d in range(2)]
    rows = lambda cc: slice(cc * c, (cc + 1) * c)
    cols = lambda h: slice(h * GLA_DV, (h + 1) * GLA_DV)


    for d, cc in work:
        q_ref, k_ref, _, _ = dirs[d]
        g = g_ref[d, rows(cc), :]
        g_hi = g.astype(BF16)
        r1 = g - g_hi.astype(F32)
        g_mid = r1.astype(BF16)
        g_lo = (r1 - g_mid.astype(F32)).astype(BF16)
        b = _dot(ones_tri[d], g_hi) + _dot(ones_tri[d], g_mid) + _dot(ones_tri[d], g_lo)
        b_tot = b[c - 1:c, :] if d == 0 else b[0:1, :]
        qc = q_ref[rows(cc), :].astype(F32) * (GLA_DK ** -0.5)
        kc = k_ref[rows(cc), :].astype(F32)
        q_dec = qc * jnp.exp(b)
        k_inv = kc * jnp.exp(-b)
        k_end = kc * jnp.exp(b_tot - b)
        dec_sc[d, cc] = jnp.exp(b_tot)
        for p in range(GLA_HEADS // 2):
            sl = slice(p * LANES, (p + 1) * LANES)
            ki_sc[d, p, rows(cc), :] = k_inv[:, sl].astype(BF16)
            for hh in range(2):
                qm_sc[d, 2 * p + hh, rows(cc), :] = jnp.where(head_mask[hh], q_dec[:, sl], 0.0).astype(BF16)
                km_sc[d, 2 * p + hh, rows(cc), :] = jnp.where(head_mask[hh], k_end[:, sl], 0.0).astype(BF16)

    for d, cc in work:
        v_ref = dirs[d][2]
        for h in range(GLA_HEADS):
            scores = _dot_nt(qm_sc[d, h, rows(cc), :], ki_sc[d, h // 2, rows(cc), :])
            p_sc[d, h, rows(cc), :] = jnp.where(tri[d], scores, 0.0).astype(BF16)
            u_sc[d, h, cc] = _dot_tn(v_ref[rows(cc), cols(h)], km_sc[d, h, rows(cc), :])

    for h in range(GLA_HEADS):
        sl = slice((h // 2) * LANES, (h // 2 + 1) * LANES)
        for d in range(2):
            st = s_ref[d, h]
            for ci in range(nc):
                cc = ci if d == 0 else nc - 1 - ci
                sb_sc[d, h, cc] = st.astype(BF16)
                st = st * dec_sc[d, cc][:, sl] + u_sc[d, h, cc]
            s_ref[d, h] = st

    for d, cc in work:
        _, _, v_ref, o_ref = dirs[d]
        for h in range(GLA_HEADS):
            o_h = (_dot(p_sc[d, h, rows(cc), :], v_ref[rows(cc), cols(h)])
                   + _dot_nt(qm_sc[d, h, rows(cc), :], sb_sc[d, h, cc]))
            o_ref[rows(cc), cols(h)] = o_h.astype(o_ref.dtype)


def gla_mixer(z, wdec, bdec, batch, seq, tl=512):
    t = batch * seq
    tl = min(tl, seq)
    nb = seq // tl
    fwd = lambda b, i: b * nb + i
    bwd = lambda b, i: b * nb + (nb - 1 - i)

    def specs(rb):
        return [
            pl.BlockSpec((tl, 256), lambda b, i: (rb(b, i), COL_GQ // 256)),
            pl.BlockSpec((tl, 256), lambda b, i: (rb(b, i), COL_GK // 256)),
            pl.BlockSpec((tl, 512), lambda b, i: (rb(b, i), COL_GV // 512)),
            pl.BlockSpec((tl, LANES), lambda b, i: (rb(b, i), COL_MISC // LANES)),
        ]

    out_sd = jax.ShapeDtypeStruct((t, GLA_HEADS * GLA_DV), BF16)
    return pl.pallas_call(
        functools.partial(_gla_kernel, tl=tl),
        out_shape=(out_sd, out_sd),
        grid=(batch, nb),
        in_specs=specs(fwd) + specs(bwd) + [
            pl.BlockSpec((2, LANES, 256), lambda b, i: (0, 0, 0)),
            pl.BlockSpec((2, 1, 256), lambda b, i: (0, 0, 0)),
        ],
        out_specs=(
            pl.BlockSpec((tl, 512), lambda b, i: (fwd(b, i), 0)),
            pl.BlockSpec((tl, 512), lambda b, i: (bwd(b, i), 0)),
        ),
        scratch_shapes=[
            pltpu.VMEM((2, GLA_HEADS, GLA_DV, LANES), F32),
            pltpu.VMEM((2, tl, 256), F32),
            pltpu.VMEM((2, GLA_HEADS, tl, LANES), BF16),
            pltpu.VMEM((2, GLA_HEADS, tl, LANES), BF16),
            pltpu.VMEM((2, GLA_HEADS // 2, tl, LANES), BF16),
            pltpu.VMEM((2, tl // GLA_CHUNK, 1, 256), F32),
            pltpu.VMEM((2, GLA_HEADS, tl, GLA_CHUNK), BF16),
            pltpu.VMEM((2, GLA_HEADS, tl // GLA_CHUNK, GLA_DV, LANES), F32),
            pltpu.VMEM((2, GLA_HEADS, tl // GLA_CHUNK, GLA_DV, LANES), BF16),
        ],
        compiler_params=_cparams("parallel", "arbitrary"),
        name="gla_mixer",
    )(z, z, z, z, z, z, z, z, wdec, bdec)


HALO = 16
SCAN_TILE = 8


def _lru_kernel(xf_ref, xfp_ref, xfn_ref, xb_ref, xbp_ref, xbn_ref, cw_ref, cb_ref, wa_ref, ba_ref,
                wi_ref, bi_ref, lam_ref, hf_ref, hb_ref, h_sc, ac_sc, hc_sc, cin_sc, carry_sc, *, tl, nb):
    i = pl.program_id(1)
    st = SCAN_TILE
    nt = tl // st
    ng = LRU_WIDTH // LANES

    @pl.when(i == 0)
    def _():
        carry_sc[...] = jnp.zeros_like(carry_sc)

    blocks = ((xf_ref, xfp_ref, xfn_ref, i), (xb_ref, xbp_ref, xbn_ref, nb - 1 - i))

    tile_row = lax.broadcasted_iota(jnp.int32, (nt, LRU_WIDTH), 0)
    cw = cw_ref[...]
    conv_bias = cb_ref[...]

    for d, (x_ref, xp_ref, xn_ref, li) in enumerate(blocks):
        x = x_ref[...].astype(F32)
        for g in range(ng):
            h_sc[d, g] = x[:, g * LANES:(g + 1) * LANES]
        xr = [jnp.concatenate([h_sc[d, g, pl.ds(r, nt, stride=st), :] for g in range(ng)], axis=1)
              for r in range(st)]
        prev = jnp.where(li > 0, xp_ref[...].astype(F32), 0.0)
        nxt = jnp.where(li < nb - 1, xn_ref[...].astype(F32), 0.0)

        def from_prev_tile(a, first):
            return jnp.where(tile_row == 0, first, pltpu.roll(a, 1, 0))

        def from_next_tile(a, last):
            return jnp.where(tile_row == nt - 1, last, pltpu.roll(a, nt - 1, 0))

        neighbours = {-2: from_prev_tile(xr[st - 2], prev[HALO - 2:HALO - 1, :]),
                      -1: from_prev_tile(xr[st - 1], prev[HALO - 1:HALO, :]),
                      st: from_next_tile(xr[0], nxt[0:1, :])}
        at = lambda r: xr[r] if 0 <= r < st else neighbours[r]
        u = jnp.concatenate(
            [cw[0:1, :] * at(r - 2) + cw[1:2, :] * at(r - 1) + cw[2:3, :] * at(r) + cw[3:4, :] * at(r + 1) + conv_bias
             for r in range(st)], axis=0)
        ub = u.astype(BF16)
        r_tanh = jnp.tanh(_dot(ub, wa_ref[d]) + ba_ref[d])
        i_tanh = jnp.tanh(_dot(ub, wi_ref[d]) + bi_ref[d])
        lam = lam_ref[d]
        softplus_neg = jnp.maximum(-lam, 0.0) + jnp.log(1.0 + jnp.exp(-jnp.abs(lam)))
        half_rate = (-0.5 * LRU_C * LOG2_E) * softplus_neg
        a = jnp.exp2(r_tanh * half_rate + half_rate)
        y = 1.0 - a * a
        root = y * lax.rsqrt(jnp.maximum(y, 1e-30))
        hh = (root * u) * (0.5 * i_tanh + 0.5)

        order = range(st) if d == 0 else range(st - 1, -1, -1)
        a_run = h_run = None
        for r in order:
            a_r, x_r = a[r * nt:(r + 1) * nt, :], hh[r * nt:(r + 1) * nt, :]
            if a_run is None:
                a_run, h_run = a_r, x_r
            else:
                h_run = a_r * h_run + x_r
                a_run = a_r * a_run
            ac_sc[d, r] = a_run
            hc_sc[d, r] = h_run

    carry = [carry_sc[0], carry_sc[1]]
    ends = (st - 1, 0)
    for step in range(nt):
        for d in range(2):
            j = step if d == 0 else nt - 1 - step
            cin_sc[d, j:j + 1, :] = carry[d]
            carry[d] = ac_sc[d, ends[d], j:j + 1, :] * carry[d] + hc_sc[d, ends[d], j:j + 1, :]
    carry_sc[0] = carry[0]
    carry_sc[1] = carry[1]

    for d, o_ref in enumerate((hf_ref, hb_ref)):
        cin = cin_sc[d]
        for r in range(st):
            h_r = ac_sc[d, r] * cin + hc_sc[d, r]
            for g in range(ng):
                h_sc[d, g, pl.ds(r, nt, stride=st), :] = h_r[:, g * LANES:(g + 1) * LANES]
        for g in range(ng):
            o_ref[:, g * LANES:(g + 1) * LANES] = h_sc[d, g].astype(o_ref.dtype)


def lru_mixer(z, conv_w, conv_b, wa, ba, wi, bi, lam, batch, seq, tl=1024):
    t = batch * seq
    tl = min(tl, seq)
    nb = seq // tl
    hpb = tl // HALO
    last_halo = t // HALO - 1
    fwd = lambda b, i: b * nb + i
    bwd = lambda b, i: b * nb + (nb - 1 - i)
    colx = COL_RX // LRU_WIDTH

    def specs(rb):
        return [
            pl.BlockSpec((tl, LRU_WIDTH), lambda b, i: (rb(b, i), colx)),
            pl.BlockSpec((HALO, LRU_WIDTH), lambda b, i: (jnp.maximum(rb(b, i) * hpb - 1, 0), colx)),
            pl.BlockSpec((HALO, LRU_WIDTH), lambda b, i: (jnp.minimum((rb(b, i) + 1) * hpb, last_halo), colx)),
        ]

    full = lambda shape: pl.BlockSpec(shape, lambda b, i: (0,) * len(shape))
    out_sd = jax.ShapeDtypeStruct((t, LRU_WIDTH), BF16)
    return pl.pallas_call(
        functools.partial(_lru_kernel, tl=tl, nb=nb),
        out_shape=(out_sd, out_sd),
        grid=(batch, nb),
        in_specs=specs(fwd) + specs(bwd) + [
            full((4, LRU_WIDTH)), full((1, LRU_WIDTH)),
            full((2, LRU_WIDTH, LRU_WIDTH)), full((2, 1, LRU_WIDTH)),
            full((2, LRU_WIDTH, LRU_WIDTH)), full((2, 1, LRU_WIDTH)),
            full((2, 1, LRU_WIDTH)),
        ],
        out_specs=(
            pl.BlockSpec((tl, LRU_WIDTH), lambda b, i: (fwd(b, i), 0)),
            pl.BlockSpec((tl, LRU_WIDTH), lambda b, i: (bwd(b, i), 0)),
        ),
        scratch_shapes=[
            pltpu.VMEM((2, LRU_WIDTH // LANES, tl, LANES), F32),
            pltpu.VMEM((2, SCAN_TILE, tl // SCAN_TILE, LRU_WIDTH), F32),
            pltpu.VMEM((2, SCAN_TILE, tl // SCAN_TILE, LRU_WIDTH), F32),
            pltpu.VMEM((2, tl // SCAN_TILE, LRU_WIDTH), F32),
            pltpu.VMEM((2, 1, LRU_WIDTH), F32),
        ],
        compiler_params=_cparams("parallel", "arbitrary"),
        name="lru_mixer",
    )(z, z, z, z, z, z, conv_w, conv_b, wa, ba, wi, bi, lam)


def _rope(x, cos_t, sin_a, sin_b):
    return x * cos_t + pltpu.roll(x, LANES - MLA_ROPE // 2, 1) * sin_a + pltpu.roll(x, MLA_ROPE // 2, 1) * sin_b


def _mla_prep_kernel(qa_ref, kva_ref, misc_ref, cos_ref, sa_ref, sb_ref, qag_ref, wqb_ref, kvag_ref, wkvb_ref,
                     qng_ref, kng_ref, q_out, k_out, v_out):
    cos_t, sin_a, sin_b = cos_ref[...], sa_ref[...], sb_ref[...]
    scale = LOG2_E * MLA_QK ** -0.5

    qa = qa_ref[...].astype(F32)
    qa_n = qa * lax.rsqrt(jnp.mean(qa * qa, axis=-1, keepdims=True) + EPS) * qag_ref[...]
    q = _dot(qa_n.astype(BF16), wqb_ref[...])
    qng = qng_ref[...]
    for h in range(MLA_HEADS):
        qh = q[:, h * MLA_QK_PAD:(h + 1) * MLA_QK_PAD]
        rstd = lax.rsqrt(jnp.sum(qh * qh, axis=-1, keepdims=True) * (1.0 / MLA_QK) + EPS)
        qh = qh * rstd * qng
        q_out[:, h * MLA_QK_PAD:h * MLA_QK_PAD + LANES] = (qh[:, :LANES] * scale).astype(q_out.dtype)
        q_out[:, h * MLA_QK_PAD + LANES:(h + 1) * MLA_QK_PAD] = (
            _rope(qh[:, LANES:], cos_t, sin_a, sin_b) * scale).astype(q_out.dtype)

    kva = kva_ref[...].astype(F32)
    kva_n = kva * lax.rsqrt(jnp.mean(kva * kva, axis=-1, keepdims=True) + EPS) * kvag_ref[...]
    kv = _dot(kva_n.astype(BF16), wkvb_ref[...])
    lane = lax.broadcasted_iota(jnp.int32, misc_ref.shape, 1)
    kr = jnp.where(lane < MLA_ROPE, misc_ref[...].astype(F32), 0.0)
    kr_ss = jnp.sum(kr * kr, axis=-1, keepdims=True)
    kng = kng_ref[...]
    kr_rot = _rope(kr * kng[:, LANES:], cos_t, sin_a, sin_b)
    for h in range(MLA_HEADS):
        k_nope = kv[:, h * 256:h * 256 + LANES]
        rstd = lax.rsqrt((jnp.sum(k_nope * k_nope, axis=-1, keepdims=True) + kr_ss) * (1.0 / MLA_QK) + EPS)
        k_out[:, h * MLA_QK_PAD:h * MLA_QK_PAD + LANES] = (k_nope * rstd * kng[:, :LANES]).astype(k_out.dtype)
        k_out[:, h * MLA_QK_PAD + LANES:(h + 1) * MLA_QK_PAD] = (kr_rot * rstd).astype(k_out.dtype)
        v_out[:, h * MLA_V:(h + 1) * MLA_V] = kv[:, h * 256 + LANES:(h + 1) * 256].astype(v_out.dtype)


def mla_prep(z, cos_t, sin_a, sin_b, qag, wqb, kvag, wkvb, qng, kng, batch, seq, tm=1024):
    t = batch * seq
    tm = min(tm, seq)
    nb = seq // tm
    full = lambda shape: pl.BlockSpec(shape, lambda i: (0,) * len(shape))
    tab = pl.BlockSpec((tm, LANES), lambda i: (i % nb, 0))
    return pl.pallas_call(
        _mla_prep_kernel,
        out_shape=(
            jax.ShapeDtypeStruct((t, MLA_HEADS * MLA_QK_PAD), BF16),
            jax.ShapeDtypeStruct((t, MLA_HEADS * MLA_QK_PAD), BF16),
            jax.ShapeDtypeStruct((t, MLA_HEADS * MLA_V), BF16),
        ),
        grid=(t // tm,),
        in_specs=[
            pl.BlockSpec((tm, MLA_Q_RANK), lambda i: (i, COL_MQA // MLA_Q_RANK)),
            pl.BlockSpec((tm, MLA_KV_RANK), lambda i: (i, COL_MKVA // MLA_KV_RANK)),
            pl.BlockSpec((tm, LANES), lambda i: (i, COL_MISC // LANES)),
            tab, tab, tab,
            full((1, MLA_Q_RANK)), full((MLA_Q_RANK, MLA_HEADS * MLA_QK_PAD)),
            full((1, MLA_KV_RANK)), full((MLA_KV_RANK, MLA_HEADS * 256)),
            full((1, MLA_QK_PAD)), full((1, MLA_QK_PAD)),
        ],
        out_specs=(
            pl.BlockSpec((tm, MLA_HEADS * MLA_QK_PAD), lambda i: (i, 0)),
            pl.BlockSpec((tm, MLA_HEADS * MLA_QK_PAD), lambda i: (i, 0)),
            pl.BlockSpec((tm, MLA_HEADS * MLA_V), lambda i: (i, 0)),
        ),
        compiler_params=_cparams("parallel"),
        name="mla_prep",
    )(z, z, z, cos_t, sin_a, sin_b, qag, wqb, kvag, wkvb, qng, kng)


def _attn_kernel(q_ref, k_ref, v_ref, o_ref, vp_sc, m_sc, acc_sc, s_sc, *, tk):
    nk = k_ref.shape[0] // tk
    nh = ATTN_HEADS_PER_STEP
    qcols = lambda hh: slice(hh * MLA_QK_PAD, (hh + 1) * MLA_QK_PAD)

    @pl.when(pl.program_id(2) == 0)
    def _():
        for hh in range(nh):
            vp_sc[hh, :, :MLA_V] = v_ref[:, hh * MLA_V:(hh + 1) * MLA_V]
            vp_sc[hh, :, MLA_V:] = jnp.ones((vp_sc.shape[1], MLA_V), vp_sc.dtype)

    def scores(hh, j):
        return _dot_nt(q_ref[:, qcols(hh)], k_ref[j * tk:(j + 1) * tk, qcols(hh)])

    def consume(hh, slot, j):
        s = s_sc[hh, slot]
        m_cur = jnp.max(s, axis=-1, keepdims=True)
        if j == 0:
            m_new = jnp.broadcast_to(m_cur, m_sc.shape[1:])
        else:
            m_prev = m_sc[hh]
            m_new = jnp.maximum(m_prev, m_cur)
        p = jnp.exp2(s - jnp.concatenate([m_new] * (tk // LANES), axis=1))
        pv = _dot(p.astype(BF16), vp_sc[hh, j * tk:(j + 1) * tk, :])
        if j == 0:
            acc_sc[hh] = pv
        else:
            alpha = jnp.exp2(m_prev - m_new)
            acc_sc[hh] = jnp.concatenate([alpha, alpha], axis=1) * acc_sc[hh] + pv
        m_sc[hh] = m_new

    for hh in range(nh):
        s_sc[hh, 0] = scores(hh, 0)
    for j in range(nk):
        for hh in range(nh):
            if j + 1 < nk:
                s_sc[hh, (j + 1) % 2] = scores(hh, j + 1)
            consume(hh, j % 2, j)
    for hh in range(nh):
        acc = acc_sc[hh]
        o_ref[:, hh * MLA_V:(hh + 1) * MLA_V] = (acc[:, :MLA_V] / acc[:, MLA_V:]).astype(o_ref.dtype)


ATTN_HEADS_PER_STEP = 2


def attention(q, k, v, batch, seq, tq=1024, tk=512):
    t = batch * seq
    tq = min(tq, seq)
    tk = min(tk, seq // 2)
    nq = seq // tq
    nh = ATTN_HEADS_PER_STEP
    return pl.pallas_call(
        functools.partial(_attn_kernel, tk=tk),
        out_shape=jax.ShapeDtypeStruct((t, MLA_HEADS * MLA_V), BF16),
        grid=(batch, MLA_HEADS // nh, nq),
        in_specs=[
            pl.BlockSpec((tq, nh * MLA_QK_PAD), lambda b, h, i: (b * nq + i, h)),
            pl.BlockSpec((seq, nh * MLA_QK_PAD), lambda b, h, i: (b, h)),
            pl.BlockSpec((seq, nh * MLA_V), lambda b, h, i: (b, h)),
        ],
        out_specs=pl.BlockSpec((tq, nh * MLA_V), lambda b, h, i: (b * nq + i, h)),
        scratch_shapes=[
            pltpu.VMEM((nh, seq, 2 * MLA_V), BF16),
            pltpu.VMEM((nh, tq, LANES), F32),
            pltpu.VMEM((nh, tq, 2 * MLA_V), F32),
            pltpu.VMEM((nh, 2, tq, tk), F32),
        ],
        compiler_params=_cparams("parallel", "parallel", "arbitrary"),
        name="mla_attention",
    )(q, k, v)


def _split2(x):
    hi = x.astype(BF16)
    return hi, (x - hi.astype(F32)).astype(BF16)


def _merge_kernel(*refs, with_router):
    (x_ref, og_ref, rg_ref, gf_ref, gb_ref, lf_ref, lb_ref, am_ref,
     mixg_ref, wgate_ref, gng_ref, wga_ref, wlr_ref, wml_ref, wo_ref, fng_ref) = refs[:16]
    if with_router:
        wr_ref, br_ref, x_out, h_out, comb_out = refs[16:]
    else:
        x_out, h_out = refs[16:]

    o = gf_ref[...].astype(F32) + gb_ref[...].astype(F32)
    gng = gng_ref[...]
    parts = []
    for h in range(GLA_HEADS):
        oh = o[:, h * GLA_DV:(h + 1) * GLA_DV]
        rstd = lax.rsqrt(jnp.mean(oh * oh, axis=-1, keepdims=True) + EPS)
        parts.append(oh * rstd * gng[:, h * GLA_DV:(h + 1) * GLA_DV])
    og = og_ref[...].astype(F32)
    ya_in = jnp.concatenate(parts, axis=-1) * (og * _sigmoid(og))
    y_a = _dot(ya_in.astype(BF16), wga_ref[...])

    rg = rg_ref[...].astype(F32)
    gelu = 0.5 * rg * (1.0 + jnp.tanh(0.7978845608028654 * (rg + 0.044715 * rg * rg * rg)))
    yb_in = (lf_ref[...].astype(F32) + lb_ref[...].astype(F32)) * gelu
    y_b = _dot(yb_in.astype(BF16), wlr_ref[...])

    y_c = _dot(am_ref[...], wml_ref[...])

    d = D_MODEL
    x = x_ref[...]
    h_mix = (x * lax.rsqrt(jnp.mean(x * x, axis=-1, keepdims=True) + EPS) * mixg_ref[...]).astype(BF16)
    gated = [jnp.tanh(_dot(h_mix, wgate_ref[:, k * d:(k + 1) * d])) * y for k, y in enumerate((y_a, y_b, y_c))]
    merged = 0.5 * ((gated[0] + gated[1] + gated[2]) + (y_a + y_b + y_c))
    x_new = x + _dot(merged.astype(BF16), wo_ref[...])
    x_out[...] = x_new
    h = x_new * lax.rsqrt(jnp.mean(x_new * x_new, axis=-1, keepdims=True) + EPS) * fng_ref[...]
    h_out[...] = h.astype(h_out.dtype)

    if with_router:
        h_hi, h_mid = _split2(h)
        w_parts = jnp.concatenate(_split2(wr_ref[...]), axis=1)
        part_a = _dot(h_hi, w_parts)
        part_b = _dot(h_mid, w_parts)
        logits = ((part_a[:, :LANES] + part_a[:, LANES:]) + (part_b[:, :LANES] + part_b[:, LANES:])) + br_ref[...]
        lane = lax.broadcasted_iota(jnp.int32, logits.shape, 1)
        logits = jnp.where(lane < N_EXPERTS, logits, -jnp.inf)
        v1 = jnp.max(logits, axis=-1, keepdims=True)
        i1 = jnp.min(jnp.where(logits == v1, lane, LANES), axis=-1, keepdims=True)
        rest = jnp.where(lane == i1, -jnp.inf, logits)
        v2 = jnp.max(rest, axis=-1, keepdims=True)
        i2 = jnp.min(jnp.where(rest == v2, lane, LANES), axis=-1, keepdims=True)
        e2 = jnp.exp(v2 - v1)
        w1 = 1.0 / (1.0 + e2)
        comb_out[...] = (jnp.where(lane == i1, w1, 0.0) + jnp.where(lane == i2, e2 * w1, 0.0)
                         + jnp.where(lane == i1 + ROUTE_TOP1, 1.0, 0.0) + jnp.where(lane == i2 + ROUTE_TOP2, 1.0, 0.0))


def merge(x2d, z, gla_f, gla_b, lru_f, lru_b, attn, mixg, wgate, gng, wga, wlr, wml, wo, fng, router=None, tm=512):
    t = x2d.shape[0]
    tm = min(tm, t)
    full = lambda shape: pl.BlockSpec(shape, lambda i: (0,) * len(shape), pipeline_mode=pl.Buffered(1))
    row512 = pl.BlockSpec((tm, 512), lambda i: (i, 0))
    in_specs = [
        pl.BlockSpec((tm, D_MODEL), lambda i: (i, 0)),
        pl.BlockSpec((tm, 512), lambda i: (i, COL_GOG // 512)),
        pl.BlockSpec((tm, 512), lambda i: (i, COL_RG // 512)),
        row512, row512, row512, row512, row512,
        full((1, D_MODEL)), full((D_MODEL, 3 * D_MODEL)),
        full((1, 512)), full((512, D_MODEL)), full((512, D_MODEL)), full((512, D_MODEL)),
        full((D_MODEL, D_MODEL)), full((1, D_MODEL)),
    ]
    args = [x2d, z, z, gla_f, gla_b, lru_f, lru_b, attn, mixg, wgate, gng, wga, wlr, wml, wo, fng]
    out_shape = [jax.ShapeDtypeStruct((t, D_MODEL), F32), jax.ShapeDtypeStruct((t, D_MODEL), BF16)]
    out_specs = [pl.BlockSpec((tm, D_MODEL), lambda i: (i, 0)), pl.BlockSpec((tm, D_MODEL), lambda i: (i, 0))]
    if router is not None:
        in_specs += [full((D_MODEL, LANES)), full((1, LANES))]
        args += list(router)
        out_shape.append(jax.ShapeDtypeStruct((t, LANES), F32))
        out_specs.append(pl.BlockSpec((tm, LANES), lambda i: (i, 0)))
    return pl.pallas_call(
        functools.partial(_merge_kernel, with_router=router is not None),
        out_shape=tuple(out_shape),
        grid=(t // tm,),
        in_specs=in_specs,
        out_specs=tuple(out_specs),
        compiler_params=_cparams("parallel"),
        name="merge_router" if router is not None else "merge",
    )(*args)


FFN_SLICE = 512


def _col_slices(total, step):
    return [(lo, min(lo + step, total)) for lo in range(0, total, step)]


def _ffn_kernel(h_ref, x_ref, w1_ref, w3_ref, w2_ref, o_ref):
    h = h_ref[...]
    y = x_ref[...]
    for lo, hi in _col_slices(w1_ref.shape[1], FFN_SLICE):
        a = _dot(h, w1_ref[:, lo:hi])
        t = (a * _sigmoid(a) * _dot(h, w3_ref[:, lo:hi])).astype(BF16)
        y = y + _dot(t, w2_ref[lo:hi, :])
    o_ref[...] = y


def ffn(h, x2d, w1, w3, w2, tm=1024):
    t = x2d.shape[0]
    tm = min(tm, t)
    resident = lambda shape: pl.BlockSpec(shape, lambda i: (0, 0), pipeline_mode=pl.Buffered(1))
    return pl.pallas_call(
        _ffn_kernel,
        out_shape=jax.ShapeDtypeStruct((t, D_MODEL), F32),
        grid=(t // tm,),
        in_specs=[
            pl.BlockSpec((tm, D_MODEL), lambda i: (i, 0)),
            pl.BlockSpec((tm, D_MODEL), lambda i: (i, 0)),
            resident((D_MODEL, D_FF)), resident((D_MODEL, D_FF)), resident((D_FF, D_MODEL)),
        ],
        out_specs=pl.BlockSpec((tm, D_MODEL), lambda i: (i, 0)),
        compiler_params=_cparams("parallel"),
        name="ffn",
    )(h, x2d, w1, w3, w2)


MOE_CHUNK = 128
MOE_PERM_ROWS = 512


def _moe_kernel(h_ref, x_ref, route_ref, w13_ref, w2_ref, o_ref,
                xs_sc, ys_sc, ws_sc, pos_sc, meta_sc, *, tb, rows):
    e = pl.program_id(1)
    n_perm = rows // MOE_PERM_ROWS

    def one_hot_rows(c):
        r = (lax.broadcasted_iota(jnp.int32, (MOE_PERM_ROWS, tb), 0) + c * MOE_PERM_ROWS).astype(F32)
        eq_a = r == pos_sc[0:1, :]
        eq_b = r == pos_sc[1:2, :]
        return eq_a, eq_b

    @pl.when(e == 0)
    def _():
        route_t = route_ref[...].T
        comb_t = route_t[0:N_EXPERTS, :]
        top1_t = route_t[ROUTE_TOP1:ROUTE_TOP1 + N_EXPERTS, :]
        top2_t = route_t[ROUTE_TOP2:ROUTE_TOP2 + N_EXPERTS, :]
        sel_t = top1_t + top2_t
        count = jnp.sum(sel_t, axis=1, keepdims=True)
        n_chunk = jnp.floor((count + (MOE_CHUNK - 1)) * (1.0 / MOE_CHUNK))
        base = []
        run = jnp.zeros((1, 1), F32)
        for ex in range(N_EXPERTS):
            base.append(run)
            run = run + n_chunk[ex:ex + 1, :]
        base_chunk = jnp.concatenate(base, axis=0)
        meta_sc[...] = jnp.concatenate(
            [jnp.broadcast_to(base_chunk, (N_EXPERTS, LANES)), jnp.broadcast_to(n_chunk, (N_EXPERTS, LANES))], axis=0)
        t_row = lax.broadcasted_iota(jnp.int32, (tb, tb), 0)
        t_col = lax.broadcasted_iota(jnp.int32, (tb, tb), 1)
        earlier = jnp.where(t_row < t_col, 1.0, 0.0).astype(BF16)
        rank_t = _dot(sel_t.astype(BF16), earlier)
        pos_t = base_chunk * float(MOE_CHUNK) + rank_t
        pos_sc[0:1, :] = jnp.sum(top1_t * pos_t, axis=0, keepdims=True)
        pos_sc[1:2, :] = jnp.sum(top2_t * pos_t, axis=0, keepdims=True)
        w_a = jnp.sum(top1_t * comb_t, axis=0, keepdims=True)
        w_b = jnp.sum(top2_t * comb_t, axis=0, keepdims=True)
        h = h_ref[...]
        for c in range(n_perm):
            eq_a, eq_b = one_hot_rows(c)
            sl = slice(c * MOE_PERM_ROWS, (c + 1) * MOE_PERM_ROWS)
            perm = jnp.where(eq_a, 1.0, jnp.where(eq_b, 1.0, 0.0)).astype(BF16)
            xs_sc[sl, :] = _dot(perm, h).astype(xs_sc.dtype)
            w_rows = jnp.sum(jnp.where(eq_a, w_a, 0.0) + jnp.where(eq_b, w_b, 0.0), axis=1, keepdims=True)
            ws_sc[sl, :] = jnp.broadcast_to(w_rows, (MOE_PERM_ROWS, LANES))
        ys_sc[...] = jnp.zeros_like(ys_sc)

    meta = meta_sc[...]
    sub = lax.broadcasted_iota(jnp.int32, meta.shape, 0)
    base_e = jnp.sum(jnp.where(sub == e, meta, 0.0)[:, 0:1]).astype(jnp.int32)
    n_e = jnp.sum(jnp.where(sub == e + N_EXPERTS, meta, 0.0)[:, 0:1]).astype(jnp.int32)

    def expert_rows(row0, n_rows):
        sl = pl.ds(row0, n_rows)
        xs = xs_sc[sl, :]
        ab = _dot(xs, w13_ref[...])
        a, b = ab[:, :EXPERT_FF], ab[:, EXPERT_FF:]
        hidden = a * _sigmoid(a) * b * ws_sc[sl, 0:1]
        ys_sc[sl, :] = _dot(hidden.astype(BF16), w2_ref[...]).astype(ys_sc.dtype)

    def pair(c, carry):
        expert_rows(pl.multiple_of((base_e + 2 * c) * MOE_CHUNK, MOE_CHUNK), 2 * MOE_CHUNK)
        return carry

    lax.fori_loop(0, n_e // 2, pair, 0)

    @pl.when(n_e % 2 == 1)
    def _():
        expert_rows(pl.multiple_of((base_e + n_e - 1) * MOE_CHUNK, MOE_CHUNK), MOE_CHUNK)

    @pl.when(e == N_EXPERTS - 1)
    def _():
        o_ref[...] = x_ref[...]
        for c in range(n_perm):
            eq_a, eq_b = one_hot_rows(c)
            perm = jnp.where(eq_a, 1.0, jnp.where(eq_b, 1.0, 0.0)).astype(BF16)
            o_ref[...] += _dot_tn(perm, ys_sc[c * MOE_PERM_ROWS:(c + 1) * MOE_PERM_ROWS, :])


def moe(h, x2d, route, w13, w2, tb=1024):
    t = x2d.shape[0]
    tb = min(tb, t)
    rows = -(-(2 * tb + N_EXPERTS * (MOE_CHUNK - 1)) // MOE_PERM_ROWS) * MOE_PERM_ROWS
    once = pl.Buffered(1)
    return pl.pallas_call(
        functools.partial(_moe_kernel, tb=tb, rows=rows),
        out_shape=jax.ShapeDtypeStruct((t, D_MODEL), F32),
        grid=(t // tb, N_EXPERTS),
        in_specs=[
            pl.BlockSpec((tb, D_MODEL), lambda i, e: (i, 0), pipeline_mode=once),
            pl.BlockSpec((tb, D_MODEL), lambda i, e: (i, 0), pipeline_mode=once),
            pl.BlockSpec((tb, LANES), lambda i, e: (i, 0)),
            pl.BlockSpec((None, D_MODEL, 2 * EXPERT_FF), lambda i, e: (e, 0, 0)),
            pl.BlockSpec((None, EXPERT_FF, D_MODEL), lambda i, e: (e, 0, 0)),
        ],
        out_specs=pl.BlockSpec((tb, D_MODEL), lambda i, e: (i, 0)),
        scratch_shapes=[
            pltpu.VMEM((rows, D_MODEL), BF16),
            pltpu.VMEM((rows, D_MODEL), BF16),
            pltpu.VMEM((rows, LANES), F32),
            pltpu.VMEM((8, tb), F32),
            pltpu.VMEM((2 * N_EXPERTS, LANES), F32),
        ],
        compiler_params=pltpu.CompilerParams(dimension_semantics=("parallel", "arbitrary"),
                                             vmem_limit_bytes=MOE_VMEM_LIMIT),
        name="moe",
    )(h, x2d, route, w13, w2)


def _pack_layer(layer, p):
    w_in = p["w_in"][layer]
    sizes = (256, 256, 512, 512, 32, 512, 512, 256, 128, 64, 3072)
    offs = [0]
    for s in sizes:
        offs.append(offs[-1] + s)
    seg = lambda n: w_in[:, offs[n]:offs[n + 1]]
    g_q, g_k, g_v, g_og, g_dec, r_x, r_gate, m_qa, m_kva, m_kr, gates = (seg(n) for n in range(11))
    w_packed = jnp.concatenate(
        [g_v, g_og, r_x, r_gate, g_q, g_k, m_qa, m_kva, m_kr, g_dec, jnp.zeros((D_MODEL, 32), F32)],
        axis=1).astype(BF16)
    w_gates = (0.5 * gates).astype(BF16)

    wdec = jnp.zeros((2, LANES, 256), F32)
    for d in range(2):
        lo = MLA_ROPE + d * GLA_LOWRANK
        wdec = wdec.at[d, lo:lo + GLA_LOWRANK, :].set(p["gla_w_dec"][layer, d])
    bdec = p["gla_b_dec"][layer].reshape(2, 1, 256)

    def block_diag(w):
        eye = jnp.eye(LRU_BLOCKS, dtype=w.dtype)
        out = w[:, :, :, None, :] * eye[None, :, None, :, None]
        return out.reshape(2, LRU_WIDTH, LRU_WIDTH).astype(BF16)

    wqb = p["mla_w_qb"][layer].reshape(MLA_Q_RANK, MLA_HEADS, MLA_QK)
    wqb = jnp.pad(wqb, ((0, 0), (0, 0), (0, MLA_QK_PAD - MLA_QK))).reshape(MLA_Q_RANK, MLA_HEADS * MLA_QK_PAD)
    pad_qk = lambda g: jnp.pad(g, (0, MLA_QK_PAD - MLA_QK)).reshape(1, MLA_QK_PAD)
    return dict(
        w_in=w_packed, w_gates=w_gates,
        norm_mix_g=p["norm_mix_g"][layer].reshape(1, D_MODEL),
        wdec=wdec.astype(BF16), bdec=bdec,
        conv_w=p["lru_conv_w"][layer], conv_b=p["lru_conv_b"][layer].reshape(1, LRU_WIDTH),
        wa=block_diag(0.5 * p["lru_w_a"][layer]), ba=0.5 * p["lru_b_a"][layer].reshape(2, 1, LRU_WIDTH),
        wi=block_diag(0.5 * p["lru_w_i"][layer]), bi=0.5 * p["lru_b_i"][layer].reshape(2, 1, LRU_WIDTH),
        lam=p["lru_lambda"][layer].reshape(2, 1, LRU_WIDTH),
        qag=p["mla_qa_g"][layer].reshape(1, MLA_Q_RANK), wqb=wqb.astype(BF16),
        kvag=p["mla_kva_g"][layer].reshape(1, MLA_KV_RANK), wkvb=p["mla_w_kvb"][layer].astype(BF16),
        qng=pad_qk(p["mla_qn_g"][layer]), kng=pad_qk(p["mla_kn_g"][layer]),
        gng=p["gla_norm_g"][layer].reshape(1, GLA_HEADS * GLA_DV),
        wga=p["gla_w_out"][layer].astype(BF16), wlr=p["lru_w_out"][layer].astype(BF16),
        wml=p["mla_w_out"][layer].astype(BF16), wo=p["w_o"][layer].astype(BF16),
        fng=p["norm_ffn_g"][layer].reshape(1, D_MODEL),
    )


def _rope_tables(length):
    pos = jnp.arange(length, dtype=F32)
    inv = ROPE_THETA ** (-jnp.arange(0, MLA_ROPE, 2, dtype=F32) / MLA_ROPE)
    ang = pos[:, None] * inv[None, :]
    cos, sin = jnp.cos(ang), jnp.sin(ang)
    zero = jnp.zeros_like(cos)
    cos_t = jnp.concatenate([cos, cos, zero, zero], axis=1)
    sin_a = jnp.concatenate([-sin, zero, zero, zero], axis=1)
    sin_b = jnp.concatenate([zero, sin, zero, zero], axis=1)
    return cos_t, sin_a, sin_b


def _trunk(x, layers, ffn_params, moe_params):
    batch, seq, _ = x.shape
    x2d = x.reshape(batch * seq, D_MODEL)
    tables = _rope_tables(seq)
    for layer, lp in enumerate(layers):
        z = in_proj(x2d, lp["norm_mix_g"], lp["w_in"])
        gla_f, gla_b = gla_mixer(z, lp["wdec"], lp["bdec"], batch, seq)
        lru_f, lru_b = lru_mixer(z, lp["conv_w"], lp["conv_b"], lp["wa"], lp["ba"], lp["wi"], lp["bi"], lp["lam"],
                                 batch, seq)
        q, k, v = mla_prep(z, *tables, lp["qag"], lp["wqb"], lp["kvag"], lp["wkvb"], lp["qng"], lp["kng"], batch, seq)
        attn = attention(q, k, v, batch, seq)
        j = layer // 2
        branch = (gla_f, gla_b, lru_f, lru_b, attn, lp["norm_mix_g"], lp["w_gates"], lp["gng"], lp["wga"], lp["wlr"],
                  lp["wml"], lp["wo"], lp["fng"])
        if layer % 2 == 0:
            x2d, h = merge(x2d, z, *branch)
            w1, w3, w2 = ffn_params[j]
            x2d = ffn(h, x2d, w1, w3, w2)
        else:
            x2d, h, route = merge(x2d, z, *branch, router=moe_params[j][:2])
            x2d = moe(h, x2d, route, *moe_params[j][2:])
    return x2d.reshape(batch, seq, D_MODEL)


def _pack_ffn(ffn_w1, ffn_w3, ffn_w2):
    return [(ffn_w1[j].astype(BF16), ffn_w3[j].astype(BF16), ffn_w2[j].astype(BF16))
            for j in range(ffn_w1.shape[0])]


def _pack_moe(moe_w_router, moe_b_router, moe_w1, moe_w3, moe_w2):
    out = []
    for j in range(moe_w1.shape[0]):
        w_router = jnp.pad(moe_w_router[j], ((0, 0), (0, LANES - N_EXPERTS)))
        b_router = jnp.pad(moe_b_router[j], (0, LANES - N_EXPERTS)).reshape(1, LANES)
        w13 = jnp.concatenate([moe_w1[j].astype(BF16), moe_w3[j].astype(BF16)], axis=-1)
        out.append((w_router, b_router, w13, moe_w2[j].astype(BF16)))
    return out


def kernel(x_prompt, x_sample, norm_mix_g, w_in, gla_w_dec, gla_b_dec, gla_norm_g, gla_w_out, lru_conv_w, lru_conv_b, lru_w_a, lru_b_a, lru_w_i, lru_b_i, lru_lambda, lru_w_out, mla_qa_g, mla_w_qb, mla_kva_g, mla_w_kvb, mla_qn_g, mla_kn_g, mla_w_out, w_o, norm_ffn_g, ffn_w1, ffn_w3, ffn_w2, moe_w_router, moe_b_router, moe_w1, moe_w3, moe_w2):
    p = dict(norm_mix_g=norm_mix_g, w_in=w_in, gla_w_dec=gla_w_dec, gla_b_dec=gla_b_dec, gla_norm_g=gla_norm_g,
             gla_w_out=gla_w_out, lru_conv_w=lru_conv_w, lru_conv_b=lru_conv_b, lru_w_a=lru_w_a, lru_b_a=lru_b_a,
             lru_w_i=lru_w_i, lru_b_i=lru_b_i, lru_lambda=lru_lambda, lru_w_out=lru_w_out, mla_qa_g=mla_qa_g,
             mla_w_qb=mla_w_qb, mla_kva_g=mla_kva_g, mla_w_kvb=mla_w_kvb, mla_qn_g=mla_qn_g, mla_kn_g=mla_kn_g,
             mla_w_out=mla_w_out, w_o=w_o, norm_ffn_g=norm_ffn_g)
    depth = w_in.shape[0]
    layers = [_pack_layer(layer, p) for layer in range(depth)]
    ffn_params = _pack_ffn(ffn_w1, ffn_w3, ffn_w2)
    moe_params = _pack_moe(moe_w_router, moe_b_router, moe_w1, moe_w3, moe_w2)
    y_prompt = _trunk(x_prompt, layers, ffn_params, moe_params)
    y_sample = _trunk(x_sample, layers, ffn_params, moe_params)
    return (y_prompt, y_sample)
```

```python
import functools

import jax
import jax.numpy as jnp
from jax import lax
from jax.experimental import pallas as pl
from jax.experimental.pallas import tpu as pltpu

F32 = jnp.float32
BF16 = jnp.bfloat16

D_MODEL = 1024
EPS = 1e-6
GLA_HEADS = 4
GLA_DK = 64
GLA_DV = 128
GLA_LOWRANK = 16
GLA_GATE_NORM = 16.0
GLA_CHUNK = 64
LRU_WIDTH = 512
LRU_BLOCKS = 8
LRU_BLOCK = LRU_WIDTH // LRU_BLOCKS
LRU_C = 8.0
MLA_HEADS = 4
MLA_Q_RANK = 256
MLA_KV_RANK = 128
MLA_NOPE = 128
MLA_ROPE = 64
MLA_V = 128
MLA_QK = MLA_NOPE + MLA_ROPE
MLA_QK_PAD = 256
ROPE_THETA = 10000.0
D_FF = 2816
N_EXPERTS = 8
EXPERT_FF = 1408
LANES = 128
LOG2_E = 1.4426950408889634

COL_GV = 0
COL_GOG = 512
COL_RX = 1024
COL_RG = 1536
COL_GQ = 2048
COL_GK = 2304
COL_MQA = 2560
COL_MKVA = 2816
COL_MISC = 2944
Z_COLS = 3072

ROUTE_TOP1 = 8
ROUTE_TOP2 = 16

VMEM_LIMIT = 56 * 1024 * 1024
MOE_VMEM_LIMIT = 62 * 1024 * 1024


def _cparams(*sem):
    return pltpu.CompilerParams(dimension_semantics=sem, vmem_limit_bytes=VMEM_LIMIT)


def _sigmoid(x):
    return 1.0 / (1.0 + jnp.exp(-x))


def _dot(a, b):
    return jnp.dot(a, b, preferred_element_type=F32)


def _dot_nt(a, b):
    return lax.dot_general(a, b, (((1,), (1,)), ((), ())), preferred_element_type=F32)


def _dot_tn(a, b):
    return lax.dot_general(a, b, (((0,), (0,)), ((), ())), preferred_element_type=F32)


def _inproj_kernel(x_ref, g_ref, w_ref, z_ref, h_ref):
    @pl.when(pl.program_id(1) == 0)
    def _():
        x = x_ref[...]
        ms = jnp.mean(x * x, axis=-1, keepdims=True)
        h_ref[...] = (x * lax.rsqrt(ms + EPS) * g_ref[...]).astype(BF16)

    z_ref[...] = _dot(h_ref[...], w_ref[...]).astype(z_ref.dtype)


def in_proj(x2d, g, w_packed, tm=1024, tn=3072):
    t = x2d.shape[0]
    tm = min(tm, t)
    return pl.pallas_call(
        _inproj_kernel,
        out_shape=jax.ShapeDtypeStruct((t, Z_COLS), BF16),
        grid=(t // tm, Z_COLS // tn),
        in_specs=[
            pl.BlockSpec((tm, D_MODEL), lambda i, j: (i, 0)),
            pl.BlockSpec((1, D_MODEL), lambda i, j: (0, 0)),
            pl.BlockSpec((D_MODEL, tn), lambda i, j: (0, j)),
        ],
        out_specs=pl.BlockSpec((tm, tn), lambda i, j: (i, j)),
        scratch_shapes=[pltpu.VMEM((tm, D_MODEL), BF16)],
        compiler_params=_cparams("parallel", "arbitrary"),
        name="in_proj",
    )(x2d, g, w_packed)


def _gla_kernel(qf_ref, kf_ref, vf_ref, mf_ref, qb_ref, kb_ref, vb_ref, mb_ref, wdec_ref, bdec_ref,
                of_ref, ob_ref, s_ref, g_ref, qm_sc, km_sc, ki_sc, dec_sc, p_sc, u_sc, sb_sc, *, tl):
    c = GLA_CHUNK
    nc = tl // c

    @pl.when(pl.program_id(1) == 0)
    def _():
        s_ref[...] = jnp.zeros_like(s_ref)

    for d, m_ref in enumerate((mf_ref, mb_ref)):
        x = _dot(m_ref[...], wdec_ref[d]) + bdec_ref[d]
        logsig = jnp.minimum(x, 0.0) - jnp.log(1.0 + jnp.exp(-jnp.abs(x)))
        g_ref[d] = logsig * (1.0 / GLA_GATE_NORM)

    row = lax.broadcasted_iota(jnp.int32, (c, c), 0)
    col = lax.broadcasted_iota(jnp.int32, (c, c), 1)
    tri = (row >= col, row <= col)
    lane = lax.broadcasted_iota(jnp.int32, (c, LANES), 1)
    head_mask = (lane < GLA_DK, lane >= GLA_DK)
    dirs = ((qf_ref, kf_ref, vf_ref, of_ref), (qb_ref, kb_ref, vb_ref, ob_ref))

    ones_tri = [jnp.where(t, 1.0, 0.0).astype(BF16) for t in tri]
    work = [(d, cc) for cc in range(nc) for d in range(2)]
    rows = lambda cc: slice(cc * c, (cc + 1) * c)
    cols = lambda h: slice(h * GLA_DV, (h + 1) * GLA_DV)


    for d, cc in work:
        q_ref, k_ref, _, _ = dirs[d]
        g = g_ref[d, rows(cc), :]
        g_hi = g.astype(BF16)
        r1 = g - g_hi.astype(F32)
        g_mid = r1.astype(BF16)
        g_lo = (r1 - g_mid.astype(F32)).astype(BF16)
        b = _dot(ones_tri[d], g_hi) + _dot(ones_tri[d], g_mid) + _dot(ones_tri[d], g_lo)
        b_tot = b[c - 1:c, :] if d == 0 else b[0:1, :]
        qc = q_ref[rows(cc), :].astype(F32) * (GLA_DK ** -0.5)
        kc = k_ref[rows(cc), :].astype(F32)
        q_dec = qc * jnp.exp(b)
        k_inv = kc * jnp.exp(-b)
        k_end = kc * jnp.exp(b_tot - b)
        dec_sc[d, cc] = jnp.exp(b_tot)
        for p in range(GLA_HEADS // 2):
            sl = slice(p * LANES, (p + 1) * LANES)
            ki_sc[d, p, rows(cc), :] = k_inv[:, sl].astype(BF16)
            for hh in range(2):
                qm_sc[d, 2 * p + hh, rows(cc), :] = jnp.where(head_mask[hh], q_dec[:, sl], 0.0).astype(BF16)
                km_sc[d, 2 * p + hh, rows(cc), :] = jnp.where(head_mask[hh], k_end[:, sl], 0.0).astype(BF16)

    for d, cc in work:
        v_ref = dirs[d][2]
        for h in range(GLA_HEADS):
            scores = _dot_nt(qm_sc[d, h, rows(cc), :], ki_sc[d, h // 2, rows(cc), :])
            p_sc[d, h, rows(cc), :] = jnp.where(tri[d], scores, 0.0).astype(BF16)
            u_sc[d, h, cc] = _dot_tn(v_ref[rows(cc), cols(h)], km_sc[d, h, rows(cc), :])

    for h in range(GLA_HEADS):
        sl = slice((h // 2) * LANES, (h // 2 + 1) * LANES)
        for d in range(2):
            st = s_ref[d, h]
            for ci in range(nc):
                cc = ci if d == 0 else nc - 1 - ci
                sb_sc[d, h, cc] = st.T.astype(BF16)
                st = st * dec_sc[d, cc][:, sl] + u_sc[d, h, cc]
            s_ref[d, h] = st

    for d, cc in work:
        _, _, v_ref, o_ref = dirs[d]
        for h in range(GLA_HEADS):
            o_h = (_dot(p_sc[d, h, rows(cc), :], v_ref[rows(cc), cols(h)])
                   + _dot(qm_sc[d, h, rows(cc), :], sb_sc[d, h, cc]))
            o_ref[rows(cc), cols(h)] = o_h.astype(o_ref.dtype)


def gla_mixer(z, wdec, bdec, batch, seq, tl=512):
    t = batch * seq
    tl = min(tl, seq)
    nb = seq // tl
    fwd = lambda b, i: b * nb + i
    bwd = lambda b, i: b * nb + (nb - 1 - i)

    def specs(rb):
        return [
            pl.BlockSpec((tl, 256), lambda b, i: (rb(b, i), COL_GQ // 256)),
            pl.BlockSpec((tl, 256), lambda b, i: (rb(b, i), COL_GK // 256)),
            pl.BlockSpec((tl, 512), lambda b, i: (rb(b, i), COL_GV // 512)),
            pl.BlockSpec((tl, LANES), lambda b, i: (rb(b, i), COL_MISC // LANES)),
        ]

    out_sd = jax.ShapeDtypeStruct((t, GLA_HEADS * GLA_DV), BF16)
    return pl.pallas_call(
        functools.partial(_gla_kernel, tl=tl),
        out_shape=(out_sd, out_sd),
        grid=(batch, nb),
        in_specs=specs(fwd) + specs(bwd) + [
            pl.BlockSpec((2, LANES, 256), lambda b, i: (0, 0, 0)),
            pl.BlockSpec((2, 1, 256), lambda b, i: (0, 0, 0)),
        ],
        out_specs=(
            pl.BlockSpec((tl, 512), lambda b, i: (fwd(b, i), 0)),
            pl.BlockSpec((tl, 512), lambda b, i: (bwd(b, i), 0)),
        ),
        scratch_shapes=[
            pltpu.VMEM((2, GLA_HEADS, GLA_DV, LANES), F32),
            pltpu.VMEM((2, tl, 256), F32),
            pltpu.VMEM((2, GLA_HEADS, tl, LANES), BF16),
            pltpu.VMEM((2, GLA_HEADS, tl, LANES), BF16),
            pltpu.VMEM((2, GLA_HEADS // 2, tl, LANES), BF16),
            pltpu.VMEM((2, tl // GLA_CHUNK, 1, 256), F32),
            pltpu.VMEM((2, GLA_HEADS, tl, GLA_CHUNK), BF16),
            pltpu.VMEM((2, GLA_HEADS, tl // GLA_CHUNK, GLA_DV, LANES), F32),
            pltpu.VMEM((2, GLA_HEADS, tl // GLA_CHUNK, GLA_DV, LANES), BF16),
        ],
        compiler_params=_cparams("parallel", "arbitrary"),
        name="gla_mixer",
    )(z, z, z, z, z, z, z, z, wdec, bdec)


HALO = 16
SCAN_TILE = 8


def _lru_kernel(xf_ref, xfp_ref, xfn_ref, xb_ref, xbp_ref, xbn_ref, cw_ref, cb_ref, wa_ref, ba_ref,
                wi_ref, bi_ref, lam_ref, hf_ref, hb_ref, h_sc, ac_sc, hc_sc, cin_sc, carry_sc, *, tl, nb):
    i = pl.program_id(1)
    st = SCAN_TILE
    nt = tl // st
    ng = LRU_WIDTH // LANES

    @pl.when(i == 0)
    def _():
        carry_sc[...] = jnp.zeros_like(carry_sc)

    blocks = ((xf_ref, xfp_ref, xfn_ref, i), (xb_ref, xbp_ref, xbn_ref, nb - 1 - i))

    tile_row = lax.broadcasted_iota(jnp.int32, (nt, LRU_WIDTH), 0)
    cw = cw_ref[...]
    conv_bias = cb_ref[...]

    for d, (x_ref, xp_ref, xn_ref, li) in enumerate(blocks):
        x = x_ref[...].astype(F32)
        for g in range(ng):
            h_sc[d, g] = x[:, g * LANES:(g + 1) * LANES]
        xr = [jnp.concatenate([h_sc[d, g, pl.ds(r, nt, stride=st), :] for g in range(ng)], axis=1)
              for r in range(st)]
        prev = jnp.where(li > 0, xp_ref[...].astype(F32), 0.0)
        nxt = jnp.where(li < nb - 1, xn_ref[...].astype(F32), 0.0)

        def from_prev_tile(a, first):
            return jnp.where(tile_row == 0, first, pltpu.roll(a, 1, 0))

        def from_next_tile(a, last):
            return jnp.where(tile_row == nt - 1, last, pltpu.roll(a, nt - 1, 0))

        neighbours = {-2: from_prev_tile(xr[st - 2], prev[HALO - 2:HALO - 1, :]),
                      -1: from_prev_tile(xr[st - 1], prev[HALO - 1:HALO, :]),
                      st: from_next_tile(xr[0], nxt[0:1, :])}
        at = lambda r: xr[r] if 0 <= r < st else neighbours[r]
        u = jnp.concatenate(
            [cw[0:1, :] * at(r - 2) + cw[1:2, :] * at(r - 1) + cw[2:3, :] * at(r) + cw[3:4, :] * at(r + 1) + conv_bias
             for r in range(st)], axis=0)
        ub = u.astype(BF16)
        r_tanh = jnp.tanh(_dot(ub, wa_ref[d]) + ba_ref[d])
        i_tanh = jnp.tanh(_dot(ub, wi_ref[d]) + bi_ref[d])
        lam = lam_ref[d]
        softplus_neg = jnp.maximum(-lam, 0.0) + jnp.log(1.0 + jnp.exp(-jnp.abs(lam)))
        half_rate = (-0.5 * LRU_C * LOG2_E) * softplus_neg
        a = jnp.exp2(r_tanh * half_rate + half_rate)
        y = 1.0 - a * a
        root = y * lax.rsqrt(jnp.maximum(y, 1e-30))
        hh = (root * u) * (0.5 * i_tanh + 0.5)

        order = range(st) if d == 0 else range(st - 1, -1, -1)
        a_run = h_run = None
        for r in order:
            a_r, x_r = a[r * nt:(r + 1) * nt, :], hh[r * nt:(r + 1) * nt, :]
            if a_run is None:
                a_run, h_run = a_r, x_r
            else:
                h_run = a_r * h_run + x_r
                a_run = a_r * a_run
            ac_sc[d, r] = a_run
            hc_sc[d, r] = h_run

    carry = [carry_sc[0], carry_sc[1]]
    ends = (st - 1, 0)
    for step in range(nt):
        for d in range(2):
            j = step if d == 0 else nt - 1 - step
            cin_sc[d, j:j + 1, :] = carry[d]
            carry[d] = ac_sc[d, ends[d], j:j + 1, :] * carry[d] + hc_sc[d, ends[d], j:j + 1, :]
    carry_sc[0] = carry[0]
    carry_sc[1] = carry[1]

    for d, o_ref in enumerate((hf_ref, hb_ref)):
        cin = cin_sc[d]
        for r in range(st):
            h_r = ac_sc[d, r] * cin + hc_sc[d, r]
            for g in range(ng):
                h_sc[d, g, pl.ds(r, nt, stride=st), :] = h_r[:, g * LANES:(g + 1) * LANES]
        for g in range(ng):
            o_ref[:, g * LANES:(g + 1) * LANES] = h_sc[d, g].astype(o_ref.dtype)


def lru_mixer(z, conv_w, conv_b, wa, ba, wi, bi, lam, batch, seq, tl=1024):
    t = batch * seq
    tl = min(tl, seq)
    nb = seq // tl
    hpb = tl // HALO
    last_halo = t // HALO - 1
    fwd = lambda b, i: b * nb + i
    bwd = lambda b, i: b * nb + (nb - 1 - i)
    colx = COL_RX // LRU_WIDTH

    def specs(rb):
        return [
            pl.BlockSpec((tl, LRU_WIDTH), lambda b, i: (rb(b, i), colx)),
            pl.BlockSpec((HALO, LRU_WIDTH), lambda b, i: (jnp.maximum(rb(b, i) * hpb - 1, 0), colx)),
            pl.BlockSpec((HALO, LRU_WIDTH), lambda b, i: (jnp.minimum((rb(b, i) + 1) * hpb, last_halo), colx)),
        ]

    full = lambda shape: pl.BlockSpec(shape, lambda b, i: (0,) * len(shape))
    out_sd = jax.ShapeDtypeStruct((t, LRU_WIDTH), BF16)
    return pl.pallas_call(
        functools.partial(_lru_kernel, tl=tl, nb=nb),
        out_shape=(out_sd, out_sd),
        grid=(batch, nb),
        in_specs=specs(fwd) + specs(bwd) + [
            full((4, LRU_WIDTH)), full((1, LRU_WIDTH)),
            full((2, LRU_WIDTH, LRU_WIDTH)), full((2, 1, LRU_WIDTH)),
            full((2, LRU_WIDTH, LRU_WIDTH)), full((2, 1, LRU_WIDTH)),
            full((2, 1, LRU_WIDTH)),
        ],
        out_specs=(
            pl.BlockSpec((tl, LRU_WIDTH), lambda b, i: (fwd(b, i), 0)),
            pl.BlockSpec((tl, LRU_WIDTH), lambda b, i: (bwd(b, i), 0)),
        ),
        scratch_shapes=[
            pltpu.VMEM((2, LRU_WIDTH // LANES, tl, LANES), F32),
            pltpu.VMEM((2, SCAN_TILE, tl // SCAN_TILE, LRU_WIDTH), F32),
            pltpu.VMEM((2, SCAN_TILE, tl // SCAN_TILE, LRU_WIDTH), F32),
            pltpu.VMEM((2, tl // SCAN_TILE, LRU_WIDTH), F32),
            pltpu.VMEM((2, 1, LRU_WIDTH), F32),
        ],
        compiler_params=_cparams("parallel", "arbitrary"),
        name="lru_mixer",
    )(z, z, z, z, z, z, conv_w, conv_b, wa, ba, wi, bi, lam)


def _rope(x, cos_t, sin_a, sin_b):
    return x * cos_t + pltpu.roll(x, LANES - MLA_ROPE // 2, 1) * sin_a + pltpu.roll(x, MLA_ROPE // 2, 1) * sin_b


def _mla_prep_kernel(qa_ref, kva_ref, misc_ref, cos_ref, sa_ref, sb_ref, qag_ref, wqb_ref, kvag_ref, wkvb_ref,
                     qng_ref, kng_ref, q_out, k_out, v_out):
    cos_t, sin_a, sin_b = cos_ref[...], sa_ref[...], sb_ref[...]
    scale = LOG2_E * MLA_QK ** -0.5

    qa = qa_ref[...].astype(F32)
    qa_n = qa * lax.rsqrt(jnp.mean(qa * qa, axis=-1, keepdims=True) + EPS) * qag_ref[...]
    q = _dot(qa_n.astype(BF16), wqb_ref[...])
    qng = qng_ref[...]
    for h in range(MLA_HEADS):
        qh = q[:, h * MLA_QK_PAD:(h + 1) * MLA_QK_PAD]
        rstd = lax.rsqrt(jnp.sum(qh * qh, axis=-1, keepdims=True) * (1.0 / MLA_QK) + EPS)
        qh = qh * rstd * qng
        q_out[:, h * MLA_QK_PAD:h * MLA_QK_PAD + LANES] = (qh[:, :LANES] * scale).astype(q_out.dtype)
        q_out[:, h * MLA_QK_PAD + LANES:(h + 1) * MLA_QK_PAD] = (
            _rope(qh[:, LANES:], cos_t, sin_a, sin_b) * scale).astype(q_out.dtype)

    kva = kva_ref[...].astype(F32)
    kva_n = kva * lax.rsqrt(jnp.mean(kva * kva, axis=-1, keepdims=True) + EPS) * kvag_ref[...]
    kv = _dot(kva_n.astype(BF16), wkvb_ref[...])
    lane = lax.broadcasted_iota(jnp.int32, misc_ref.shape, 1)
    kr = jnp.where(lane < MLA_ROPE, misc_ref[...].astype(F32), 0.0)
    kr_ss = jnp.sum(kr * kr, axis=-1, keepdims=True)
    kng = kng_ref[...]
    kr_rot = _rope(kr * kng[:, LANES:], cos_t, sin_a, sin_b)
    for h in range(MLA_HEADS):
        k_nope = kv[:, h * 256:h * 256 + LANES]
        rstd = lax.rsqrt((jnp.sum(k_nope * k_nope, axis=-1, keepdims=True) + kr_ss) * (1.0 / MLA_QK) + EPS)
        k_out[:, h * MLA_QK_PAD:h * MLA_QK_PAD + LANES] = (k_nope * rstd * kng[:, :LANES]).astype(k_out.dtype)
        k_out[:, h * MLA_QK_PAD + LANES:(h + 1) * MLA_QK_PAD] = (kr_rot * rstd).astype(k_out.dtype)
        v_out[:, h * MLA_V:(h + 1) * MLA_V] = kv[:, h * 256 + LANES:(h + 1) * 256].astype(v_out.dtype)


def mla_prep(z, cos_t, sin_a, sin_b, qag, wqb, kvag, wkvb, qng, kng, batch, seq, tm=1024):
    t = batch * seq
    tm = min(tm, seq)
    nb = seq // tm
    full = lambda shape: pl.BlockSpec(shape, lambda i: (0,) * len(shape))
    tab = pl.BlockSpec((tm, LANES), lambda i: (i % nb, 0))
    return pl.pallas_call(
        _mla_prep_kernel,
        out_shape=(
            jax.ShapeDtypeStruct((t, MLA_HEADS * MLA_QK_PAD), BF16),
            jax.ShapeDtypeStruct((t, MLA_HEADS * MLA_QK_PAD), BF16),
            jax.ShapeDtypeStruct((t, MLA_HEADS * MLA_V), BF16),
        ),
        grid=(t // tm,),
        in_specs=[
            pl.BlockSpec((tm, MLA_Q_RANK), lambda i: (i, COL_MQA // MLA_Q_RANK)),
            pl.BlockSpec((tm, MLA_KV_RANK), lambda i: (i, COL_MKVA // MLA_KV_RANK)),
            pl.BlockSpec((tm, LANES), lambda i: (i, COL_MISC // LANES)),
            tab, tab, tab,
            full((1, MLA_Q_RANK)), full((MLA_Q_RANK, MLA_HEADS * MLA_QK_PAD)),
            full((1, MLA_KV_RANK)), full((MLA_KV_RANK, MLA_HEADS * 256)),
            full((1, MLA_QK_PAD)), full((1, MLA_QK_PAD)),
        ],
        out_specs=(
            pl.BlockSpec((tm, MLA_HEADS * MLA_QK_PAD), lambda i: (i, 0)),
            pl.BlockSpec((tm, MLA_HEADS * MLA_QK_PAD), lambda i: (i, 0)),
            pl.BlockSpec((tm, MLA_HEADS * MLA_V), lambda i: (i, 0)),
        ),
        compiler_params=_cparams("parallel"),
        name="mla_prep",
    )(z, z, z, cos_t, sin_a, sin_b, qag, wqb, kvag, wkvb, qng, kng)


def _attn_kernel(q_ref, k_ref, v_ref, o_ref, vp_sc, m_sc, acc_sc, s_sc, *, tk):
    nk = k_ref.shape[0] // tk
    nh = ATTN_HEADS_PER_STEP
    qcols = lambda hh: slice(hh * MLA_QK_PAD, (hh + 1) * MLA_QK_PAD)

    @pl.when(pl.program_id(2) == 0)
    def _():
        for hh in range(nh):
            vp_sc[hh, :, :MLA_V] = v_ref[:, hh * MLA_V:(hh + 1) * MLA_V]
            vp_sc[hh, :, MLA_V:] = jnp.ones((vp_sc.shape[1], MLA_V), vp_sc.dtype)

    def scores(hh, j):
        return _dot_nt(q_ref[:, qcols(hh)], k_ref[j * tk:(j + 1) * tk, qcols(hh)])

    def consume(hh, slot, j):
        s = s_sc[hh, slot]
        m_cur = jnp.max(s, axis=-1, keepdims=True)
        if j == 0:
            m_new = jnp.broadcast_to(m_cur, m_sc.shape[1:])
        else:
            m_prev = m_sc[hh]
            m_new = jnp.maximum(m_prev, m_cur)
        p = jnp.exp2(s - jnp.concatenate([m_new] * (tk // LANES), axis=1))
        pv = _dot(p.astype(BF16), vp_sc[hh, j * tk:(j + 1) * tk, :])
        if j == 0:
            acc_sc[hh] = pv
        else:
            alpha = jnp.exp2(m_prev - m_new)
            acc_sc[hh] = jnp.concatenate([alpha, alpha], axis=1) * acc_sc[hh] + pv
        m_sc[hh] = m_new

    for hh in range(nh):
        s_sc[hh, 0] = scores(hh, 0)
    for j in range(nk):
        for hh in range(nh):
            if j + 1 < nk:
                s_sc[hh, (j + 1) % 2] = scores(hh, j + 1)
            consume(hh, j % 2, j)
    for hh in range(nh):
        acc = acc_sc[hh]
        o_ref[:, hh * MLA_V:(hh + 1) * MLA_V] = (acc[:, :MLA_V] / acc[:, MLA_V:]).astype(o_ref.dtype)


ATTN_HEADS_PER_STEP = 2


def attention(q, k, v, batch, seq, tq=1024, tk=512):
    t = batch * seq
    tq = min(tq, seq)
    tk = min(tk, seq // 2)
    nq = seq // tq
    nh = ATTN_HEADS_PER_STEP
    return pl.pallas_call(
        functools.partial(_attn_kernel, tk=tk),
        out_shape=jax.ShapeDtypeStruct((t, MLA_HEADS * MLA_V), BF16),
        grid=(batch, MLA_HEADS // nh, nq),
        in_specs=[
            pl.BlockSpec((tq, nh * MLA_QK_PAD), lambda b, h, i: (b * nq + i, h)),
            pl.BlockSpec((seq, nh * MLA_QK_PAD), lambda b, h, i: (b, h)),
            pl.BlockSpec((seq, nh * MLA_V), lambda b, h, i: (b, h)),
        ],
        out_specs=pl.BlockSpec((tq, nh * MLA_V), lambda b, h, i: (b * nq + i, h)),
        scratch_shapes=[
            pltpu.VMEM((nh, seq, 2 * MLA_V), BF16),
            pltpu.VMEM((nh, tq, LANES), F32),
            pltpu.VMEM((nh, tq, 2 * MLA_V), F32),
            pltpu.VMEM((nh, 2, tq, tk), F32),
        ],
        compiler_params=_cparams("parallel", "parallel", "arbitrary"),
        name="mla_attention",
    )(q, k, v)


def _split2(x):
    hi = x.astype(BF16)
    return hi, (x - hi.astype(F32)).astype(BF16)


def _merge_kernel(*refs, with_router):
    (x_ref, og_ref, rg_ref, gf_ref, gb_ref, lf_ref, lb_ref, am_ref,
     mixg_ref, wgate_ref, gng_ref, wga_ref, wlr_ref, wml_ref, wo_ref, fng_ref) = refs[:16]
    if with_router:
        wr_ref, br_ref, x_out, h_out, comb_out = refs[16:]
    else:
        x_out, h_out = refs[16:]

    o = gf_ref[...].astype(F32) + gb_ref[...].astype(F32)
    gng = gng_ref[...]
    parts = []
    for h in range(GLA_HEADS):
        oh = o[:, h * GLA_DV:(h + 1) * GLA_DV]
        rstd = lax.rsqrt(jnp.mean(oh * oh, axis=-1, keepdims=True) + EPS)
        parts.append(oh * rstd * gng[:, h * GLA_DV:(h + 1) * GLA_DV])
    og = og_ref[...].astype(F32)
    ya_in = jnp.concatenate(parts, axis=-1) * (og * _sigmoid(og))
    y_a = _dot(ya_in.astype(BF16), wga_ref[...])

    rg = rg_ref[...].astype(F32)
    gelu = 0.5 * rg * (1.0 + jnp.tanh(0.7978845608028654 * (rg + 0.044715 * rg * rg * rg)))
    yb_in = (lf_ref[...].astype(F32) + lb_ref[...].astype(F32)) * gelu
    y_b = _dot(yb_in.astype(BF16), wlr_ref[...])

    y_c = _dot(am_ref[...], wml_ref[...])

    d = D_MODEL
    x = x_ref[...]
    h_mix = (x * lax.rsqrt(jnp.mean(x * x, axis=-1, keepdims=True) + EPS) * mixg_ref[...]).astype(BF16)
    gated = [jnp.tanh(_dot(h_mix, wgate_ref[:, k * d:(k + 1) * d])) * y for k, y in enumerate((y_a, y_b, y_c))]
    merged = 0.5 * ((gated[0] + gated[1] + gated[2]) + (y_a + y_b + y_c))
    x_new = x + _dot(merged.astype(BF16), wo_ref[...])
    x_out[...] = x_new
    h = x_new * lax.rsqrt(jnp.mean(x_new * x_new, axis=-1, keepdims=True) + EPS) * fng_ref[...]
    h_out[...] = h.astype(h_out.dtype)

    if with_router:
        h_hi, h_mid = _split2(h)
        w_parts = jnp.concatenate(_split2(wr_ref[...]), axis=1)
        part_a = _dot(h_hi, w_parts)
        part_b = _dot(h_mid, w_parts)
        logits = ((part_a[:, :LANES] + part_a[:, LANES:]) + (part_b[:, :LANES] + part_b[:, LANES:])) + br_ref[...]
        lane = lax.broadcasted_iota(jnp.int32, logits.shape, 1)
        logits = jnp.where(lane < N_EXPERTS, logits, -jnp.inf)
        v1 = jnp.max(logits, axis=-1, keepdims=True)
        i1 = jnp.min(jnp.where(logits == v1, lane, LANES), axis=-1, keepdims=True)
        rest = jnp.where(lane == i1, -jnp.inf, logits)
        v2 = jnp.max(rest, axis=-1, keepdims=True)
        i2 = jnp.min(jnp.where(rest == v2, lane, LANES), axis=-1, keepdims=True)
        e2 = jnp.exp(v2 - v1)
        w1 = 1.0 / (1.0 + e2)
        comb_out[...] = (jnp.where(lane == i1, w1, 0.0) + jnp.where(lane == i2, e2 * w1, 0.0)
                         + jnp.where(lane == i1 + ROUTE_TOP1, 1.0, 0.0) + jnp.where(lane == i2 + ROUTE_TOP2, 1.0, 0.0))


def merge(x2d, z, gla_f, gla_b, lru_f, lru_b, attn, mixg, wgate, gng, wga, wlr, wml, wo, fng, router=None, tm=512):
    t = x2d.shape[0]
    tm = min(tm, t)
    full = lambda shape: pl.BlockSpec(shape, lambda i: (0,) * len(shape), pipeline_mode=pl.Buffered(1))
    row512 = pl.BlockSpec((tm, 512), lambda i: (i, 0))
    in_specs = [
        pl.BlockSpec((tm, D_MODEL), lambda i: (i, 0)),
        pl.BlockSpec((tm, 512), lambda i: (i, COL_GOG // 512)),
        pl.BlockSpec((tm, 512), lambda i: (i, COL_RG // 512)),
        row512, row512, row512, row512, row512,
        full((1, D_MODEL)), full((D_MODEL, 3 * D_MODEL)),
        full((1, 512)), full((512, D_MODEL)), full((512, D_MODEL)), full((512, D_MODEL)),
        full((D_MODEL, D_MODEL)), full((1, D_MODEL)),
    ]
    args = [x2d, z, z, gla_f, gla_b, lru_f, lru_b, attn, mixg, wgate, gng, wga, wlr, wml, wo, fng]
    out_shape = [jax.ShapeDtypeStruct((t, D_MODEL), F32), jax.ShapeDtypeStruct((t, D_MODEL), BF16)]
    out_specs = [pl.BlockSpec((tm, D_MODEL), lambda i: (i, 0)), pl.BlockSpec((tm, D_MODEL), lambda i: (i, 0))]
    if router is not None:
        in_specs += [full((D_MODEL, LANES)), full((1, LANES))]
        args += list(router)
        out_shape.append(jax.ShapeDtypeStruct((t, LANES), F32))
        out_specs.append(pl.BlockSpec((tm, LANES), lambda i: (i, 0)))
    return pl.pallas_call(
        functools.partial(_merge_kernel, with_router=router is not None),
        out_shape=tuple(out_shape),
        grid=(t // tm,),
        in_specs=in_specs,
        out_specs=tuple(out_specs),
        compiler_params=_cparams("parallel"),
        name="merge_router" if router is not None else "merge",
    )(*args)


FFN_SLICE = 512


def _col_slices(total, step):
    return [(lo, min(lo + step, total)) for lo in range(0, total, step)]


def _ffn_kernel(h_ref, x_ref, w1_ref, w3_ref, w2_ref, o_ref):
    h = h_ref[...]
    y = x_ref[...]
    for lo, hi in _col_slices(w1_ref.shape[1], FFN_SLICE):
        a = _dot(h, w1_ref[:, lo:hi])
        t = (a * _sigmoid(a) * _dot(h, w3_ref[:, lo:hi])).astype(BF16)
        y = y + _dot(t, w2_ref[lo:hi, :])
    o_ref[...] = y


def ffn(h, x2d, w1, w3, w2, tm=1024):
    t = x2d.shape[0]
    tm = min(tm, t)
    resident = lambda shape: pl.BlockSpec(shape, lambda i: (0, 0), pipeline_mode=pl.Buffered(1))
    return pl.pallas_call(
        _ffn_kernel,
        out_shape=jax.ShapeDtypeStruct((t, D_MODEL), F32),
        grid=(t // tm,),
        in_specs=[
            pl.BlockSpec((tm, D_MODEL), lambda i: (i, 0)),
            pl.BlockSpec((tm, D_MODEL), lambda i: (i, 0)),
            resident((D_MODEL, D_FF)), resident((D_MODEL, D_FF)), resident((D_FF, D_MODEL)),
        ],
        out_specs=pl.BlockSpec((tm, D_MODEL), lambda i: (i, 0)),
        compiler_params=_cparams("parallel"),
        name="ffn",
    )(h, x2d, w1, w3, w2)


MOE_CHUNK = 128
MOE_PERM_ROWS = 512


def _moe_kernel(h_ref, x_ref, route_ref, w13_ref, w2_ref, o_ref,
                xs_sc, ys_sc, ws_sc, pos_sc, meta_sc, *, tb, rows):
    e = pl.program_id(1)
    n_perm = rows // MOE_PERM_ROWS

    def one_hot_rows(c):
        r = (lax.broadcasted_iota(jnp.int32, (MOE_PERM_ROWS, tb), 0) + c * MOE_PERM_ROWS).astype(F32)
        eq_a = r == pos_sc[0:1, :]
        eq_b = r == pos_sc[1:2, :]
        return eq_a, eq_b

    @pl.when(e == 0)
    def _():
        route_t = route_ref[...].T
        comb_t = route_t[0:N_EXPERTS, :]
        top1_t = route_t[ROUTE_TOP1:ROUTE_TOP1 + N_EXPERTS, :]
        top2_t = route_t[ROUTE_TOP2:ROUTE_TOP2 + N_EXPERTS, :]
        sel_t = top1_t + top2_t
        count = jnp.sum(sel_t, axis=1, keepdims=True)
        n_chunk = jnp.floor((count + (MOE_CHUNK - 1)) * (1.0 / MOE_CHUNK))
        base = []
        run = jnp.zeros((1, 1), F32)
        for ex in range(N_EXPERTS):
            base.append(run)
            run = run + n_chunk[ex:ex + 1, :]
        base_chunk = jnp.concatenate(base, axis=0)
        meta_sc[...] = jnp.concatenate(
            [jnp.broadcast_to(base_chunk, (N_EXPERTS, LANES)), jnp.broadcast_to(n_chunk, (N_EXPERTS, LANES))], axis=0)
        t_row = lax.broadcasted_iota(jnp.int32, (tb, tb), 0)
        t_col = lax.broadcasted_iota(jnp.int32, (tb, tb), 1)
        earlier = jnp.where(t_row < t_col, 1.0, 0.0).astype(BF16)
        rank_t = _dot(sel_t.astype(BF16), earlier)
        pos_t = base_chunk * float(MOE_CHUNK) + rank_t
        pos_sc[0:1, :] = jnp.sum(top1_t * pos_t, axis=0, keepdims=True)
        pos_sc[1:2, :] = jnp.sum(top2_t * pos_t, axis=0, keepdims=True)
        w_a = jnp.sum(top1_t * comb_t, axis=0, keepdims=True)
        w_b = jnp.sum(top2_t * comb_t, axis=0, keepdims=True)
        h = h_ref[...]
        for c in range(n_perm):
            eq_a, eq_b = one_hot_rows(c)
            sl = slice(c * MOE_PERM_ROWS, (c + 1) * MOE_PERM_ROWS)
            perm = jnp.where(eq_a, 1.0, jnp.where(eq_b, 1.0, 0.0)).astype(BF16)
            xs_sc[sl, :] = _dot(perm, h).astype(xs_sc.dtype)
            w_rows = jnp.sum(jnp.where(eq_a, w_a, 0.0) + jnp.where(eq_b, w_b, 0.0), axis=1, keepdims=True)
            ws_sc[sl, :] = jnp.broadcast_to(w_rows, (MOE_PERM_ROWS, LANES))
        ys_sc[...] = jnp.zeros_like(ys_sc)

    meta = meta_sc[...]
    sub = lax.broadcasted_iota(jnp.int32, meta.shape, 0)
    base_e = jnp.sum(jnp.where(sub == e, meta, 0.0)[:, 0:1]).astype(jnp.int32)
    n_e = jnp.sum(jnp.where(sub == e + N_EXPERTS, meta, 0.0)[:, 0:1]).astype(jnp.int32)

    def expert_rows(row0, n_rows):
        sl = pl.ds(row0, n_rows)
        xs = xs_sc[sl, :]
        ab = _dot(xs, w13_ref[...])
        a, b = ab[:, :EXPERT_FF], ab[:, EXPERT_FF:]
        hidden = a * _sigmoid(a) * b * ws_sc[sl, 0:1]
        ys_sc[sl, :] = _dot(hidden.astype(BF16), w2_ref[...]).astype(ys_sc.dtype)

    def pair(c, carry):
        expert_rows(pl.multiple_of((base_e + 2 * c) * MOE_CHUNK, MOE_CHUNK), 2 * MOE_CHUNK)
        return carry

    lax.fori_loop(0, n_e // 2, pair, 0)

    @pl.when(n_e % 2 == 1)
    def _():
        expert_rows(pl.multiple_of((base_e + n_e - 1) * MOE_CHUNK, MOE_CHUNK), MOE_CHUNK)

    @pl.when(e == N_EXPERTS - 1)
    def _():
        o_ref[...] = x_ref[...]
        for c in range(n_perm):
            eq_a, eq_b = one_hot_rows(c)
            perm = jnp.where(eq_a, 1.0, jnp.where(eq_b, 1.0, 0.0)).astype(BF16)
            o_ref[...] += _dot_tn(perm, ys_sc[c * MOE_PERM_ROWS:(c + 1) * MOE_PERM_ROWS, :])


def moe(h, x2d, route, w13, w2, tb=1024):
    t = x2d.shape[0]
    tb = min(tb, t)
    rows = -(-(2 * tb + N_EXPERTS * (MOE_CHUNK - 1)) // MOE_PERM_ROWS) * MOE_PERM_ROWS
    once = pl.Buffered(1)
    return pl.pallas_call(
        functools.partial(_moe_kernel, tb=tb, rows=rows),
        out_shape=jax.ShapeDtypeStruct((t, D_MODEL), F32),
        grid=(t // tb, N_EXPERTS),
        in_specs=[
            pl.BlockSpec((tb, D_MODEL), lambda i, e: (i, 0), pipeline_mode=once),
            pl.BlockSpec((tb, D_MODEL), lambda i, e: (i, 0), pipeline_mode=once),
            pl.BlockSpec((tb, LANES), lambda i, e: (i, 0)),
            pl.BlockSpec((None, D_MODEL, 2 * EXPERT_FF), lambda i, e: (e, 0, 0)),
            pl.BlockSpec((None, EXPERT_FF, D_MODEL), lambda i, e: (e, 0, 0)),
        ],
        out_specs=pl.BlockSpec((tb, D_MODEL), lambda i, e: (i, 0)),
        scratch_shapes=[
            pltpu.VMEM((rows, D_MODEL), BF16),
            pltpu.VMEM((rows, D_MODEL), BF16),
            pltpu.VMEM((rows, LANES), F32),
            pltpu.VMEM((8, tb), F32),
            pltpu.VMEM((2 * N_EXPERTS, LANES), F32),
        ],
        compiler_params=pltpu.CompilerParams(dimension_semantics=("parallel", "arbitrary"),
                                             vmem_limit_bytes=MOE_VMEM_LIMIT),
        name="moe",
    )(h, x2d, route, w13, w2)


def _pack_layer(layer, p):
    w_in = p["w_in"][layer]
    sizes = (256, 256, 512, 512, 32, 512, 512, 256, 128, 64, 3072)
    offs = [0]
    for s in sizes:
        offs.append(offs[-1] + s)
    seg = lambda n: w_in[:, offs[n]:offs[n + 1]]
    g_q, g_k, g_v, g_og, g_dec, r_x, r_gate, m_qa, m_kva, m_kr, gates = (seg(n) for n in range(11))
    w_packed = jnp.concatenate(
        [g_v, g_og, r_x, r_gate, g_q, g_k, m_qa, m_kva, m_kr, g_dec, jnp.zeros((D_MODEL, 32), F32)],
        axis=1).astype(BF16)
    w_gates = (0.5 * gates).astype(BF16)

    wdec = jnp.zeros((2, LANES, 256), F32)
    for d in range(2):
        lo = MLA_ROPE + d * GLA_LOWRANK
        wdec = wdec.at[d, lo:lo + GLA_LOWRANK, :].set(p["gla_w_dec"][layer, d])
    bdec = p["gla_b_dec"][layer].reshape(2, 1, 256)

    def block_diag(w):
        eye = jnp.eye(LRU_BLOCKS, dtype=w.dtype)
        out = w[:, :, :, None, :] * eye[None, :, None, :, None]
        return out.reshape(2, LRU_WIDTH, LRU_WIDTH).astype(BF16)

    wqb = p["mla_w_qb"][layer].reshape(MLA_Q_RANK, MLA_HEADS, MLA_QK)
    wqb = jnp.pad(wqb, ((0, 0), (0, 0), (0, MLA_QK_PAD - MLA_QK))).reshape(MLA_Q_RANK, MLA_HEADS * MLA_QK_PAD)
    pad_qk = lambda g: jnp.pad(g, (0, MLA_QK_PAD - MLA_QK)).reshape(1, MLA_QK_PAD)
    return dict(
        w_in=w_packed, w_gates=w_gates,
        norm_mix_g=p["norm_mix_g"][layer].reshape(1, D_MODEL),
        wdec=wdec.astype(BF16), bdec=bdec,
        conv_w=p["lru_conv_w"][layer], conv_b=p["lru_conv_b"][layer].reshape(1, LRU_WIDTH),
        wa=block_diag(0.5 * p["lru_w_a"][layer]), ba=0.5 * p["lru_b_a"][layer].reshape(2, 1, LRU_WIDTH),
        wi=block_diag(0.5 * p["lru_w_i"][layer]), bi=0.5 * p["lru_b_i"][layer].reshape(2, 1, LRU_WIDTH),
        lam=p["lru_lambda"][layer].reshape(2, 1, LRU_WIDTH),
        qag=p["mla_qa_g"][layer].reshape(1, MLA_Q_RANK), wqb=wqb.astype(BF16),
        kvag=p["mla_kva_g"][layer].reshape(1, MLA_KV_RANK), wkvb=p["mla_w_kvb"][layer].astype(BF16),
        qng=pad_qk(p["mla_qn_g"][layer]), kng=pad_qk(p["mla_kn_g"][layer]),
        gng=p["gla_norm_g"][layer].reshape(1, GLA_HEADS * GLA_DV),
        wga=p["gla_w_out"][layer].astype(BF16), wlr=p["lru_w_out"][layer].astype(BF16),
        wml=p["mla_w_out"][layer].astype(BF16), wo=p["w_o"][layer].astype(BF16),
        fng=p["norm_ffn_g"][layer].reshape(1, D_MODEL),
    )


def _rope_tables(length):
    pos = jnp.arange(length, dtype=F32)
    inv = ROPE_THETA ** (-jnp.arange(0, MLA_ROPE, 2, dtype=F32) / MLA_ROPE)
    ang = pos[:, None] * inv[None, :]
    cos, sin = jnp.cos(ang), jnp.sin(ang)
    zero = jnp.zeros_like(cos)
    cos_t = jnp.concatenate([cos, cos, zero, zero], axis=1)
    sin_a = jnp.concatenate([-sin, zero, zero, zero], axis=1)
    sin_b = jnp.concatenate([zero, sin, zero, zero], axis=1)
    return cos_t, sin_a, sin_b


def _trunk(x, layers, ffn_params, moe_params):
    batch, seq, _ = x.shape
    x2d = x.reshape(batch * seq, D_MODEL)
    tables = _rope_tables(seq)
    for layer, lp in enumerate(layers):
        z = in_proj(x2d, lp["norm_mix_g"], lp["w_in"])
        gla_f, gla_b = gla_mixer(z, lp["wdec"], lp["bdec"], batch, seq)
        lru_f, lru_b = lru_mixer(z, lp["conv_w"], lp["conv_b"], lp["wa"], lp["ba"], lp["wi"], lp["bi"], lp["lam"],
                                 batch, seq)
        q, k, v = mla_prep(z, *tables, lp["qag"], lp["wqb"], lp["kvag"], lp["wkvb"], lp["qng"], lp["kng"], batch, seq)
        attn = attention(q, k, v, batch, seq)
        j = layer // 2
        branch = (gla_f, gla_b, lru_f, lru_b, attn, lp["norm_mix_g"], lp["w_gates"], lp["gng"], lp["wga"], lp["wlr"],
                  lp["wml"], lp["wo"], lp["fng"])
        if layer % 2 == 0:
            x2d, h = merge(x2d, z, *branch)
            w1, w3, w2 = ffn_params[j]
            x2d = ffn(h, x2d, w1, w3, w2)
        else:
            x2d, h, route = merge(x2d, z, *branch, router=moe_params[j][:2])
            x2d = moe(h, x2d, route, *moe_params[j][2:])
    return x2d.reshape(batch, seq, D_MODEL)


def _pack_ffn(ffn_w1, ffn_w3, ffn_w2):
    return [(ffn_w1[j].astype(BF16), ffn_w3[j].astype(BF16), ffn_w2[j].astype(BF16))
            for j in range(ffn_w1.shape[0])]


def _pack_moe(moe_w_router, moe_b_router, moe_w1, moe_w3, moe_w2):
    out = []
    for j in range(moe_w1.shape[0]):
        w_router = jnp.pad(moe_w_router[j], ((0, 0), (0, LANES - N_EXPERTS)))
        b_router = jnp.pad(moe_b_router[j], (0, LANES - N_EXPERTS)).reshape(1, LANES)
        w13 = jnp.concatenate([moe_w1[j].astype(BF16), moe_w3[j].astype(BF16)], axis=-1)
        out.append((w_router, b_router, w13, moe_w2[j].astype(BF16)))
    return out


def kernel(x_prompt, x_sample, norm_mix_g, w_in, gla_w_dec, gla_b_dec, gla_norm_g, gla_w_out, lru_conv_w, lru_conv_b, lru_w_a, lru_b_a, lru_w_i, lru_b_i, lru_lambda, lru_w_out, mla_qa_g, mla_w_qb, mla_kva_g, mla_w_kvb, mla_qn_g, mla_kn_g, mla_w_out, w_o, norm_ffn_g, ffn_w1, ffn_w3, ffn_w2, moe_w_router, moe_b_router, moe_w1, moe_w3, moe_w2):
    p = dict(norm_mix_g=norm_mix_g, w_in=w_in, gla_w_dec=gla_w_dec, gla_b_dec=gla_b_dec, gla_norm_g=gla_norm_g,
             gla_w_out=gla_w_out, lru_conv_w=lru_conv_w, lru_conv_b=lru_conv_b, lru_w_a=lru_w_a, lru_b_a=lru_b_a,
             lru_w_i=lru_w_i, lru_b_i=lru_b_i, lru_lambda=lru_lambda, lru_w_out=lru_w_out, mla_qa_g=mla_qa_g,
             mla_w_qb=mla_w_qb, mla_kva_g=mla_kva_g, mla_w_kvb=mla_w_kvb, mla_qn_g=mla_qn_g, mla_kn_g=mla_kn_g,
             mla_w_out=mla_w_out, w_o=w_o, norm_ffn_g=norm_ffn_g)
    depth = w_in.shape[0]
    layers = [_pack_layer(layer, p) for layer in range(depth)]
    ffn_params = _pack_ffn(ffn_w1, ffn_w3, ffn_w2)
    moe_params = _pack_moe(moe_w_router, moe_b_router, moe_w1, moe_w3, moe_w2)
    y_prompt = _trunk(x_prompt, layers, ffn_params, moe_params)
    y_sample = _trunk(x_sample, layers, ffn_params, moe_params)
    return (y_prompt, y_sample)
```

```python
import functools

import jax
import jax.numpy as jnp
from jax import lax
from jax.experimental import pallas as pl
from jax.experimental.pallas import tpu as pltpu

F32 = jnp.float32
BF16 = jnp.bfloat16

D_MODEL = 1024
EPS = 1e-6
GLA_HEADS = 4
GLA_DK = 64
GLA_DV = 128
GLA_LOWRANK = 16
GLA_GATE_NORM = 16.0
GLA_CHUNK = 64
LRU_WIDTH = 512
LRU_BLOCKS = 8
LRU_BLOCK = LRU_WIDTH // LRU_BLOCKS
LRU_C = 8.0
MLA_HEADS = 4
MLA_Q_RANK = 256
MLA_KV_RANK = 128
MLA_NOPE = 128
MLA_ROPE = 64
MLA_V = 128
MLA_QK = MLA_NOPE + MLA_ROPE
MLA_QK_PAD = 256
ROPE_THETA = 10000.0
D_FF = 2816
N_EXPERTS = 8
EXPERT_FF = 1408
LANES = 128
LOG2_E = 1.4426950408889634

COL_GV = 0
COL_GOG = 512
COL_RX = 1024
COL_RG = 1536
COL_GQ = 2048
COL_GK = 2304
COL_MQA = 2560
COL_MKVA = 2816
COL_MISC = 2944
Z_COLS = 3072

ROUTE_TOP1 = 8
ROUTE_TOP2 = 16

VMEM_LIMIT = 56 * 1024 * 1024
MOE_VMEM_LIMIT = 62 * 1024 * 1024


def _cparams(*sem):
    return pltpu.CompilerParams(dimension_semantics=sem, vmem_limit_bytes=VMEM_LIMIT)


def _sigmoid(x):
    return 1.0 / (1.0 + jnp.exp(-x))


def _dot(a, b):
    return jnp.dot(a, b, preferred_element_type=F32)


def _dot_nt(a, b):
    return lax.dot_general(a, b, (((1,), (1,)), ((), ())), preferred_element_type=F32)


def _dot_tn(a, b):
    return lax.dot_general(a, b, (((0,), (0,)), ((), ())), preferred_element_type=F32)


def _inproj_kernel(x_ref, g_ref, w_ref, z_ref, h_ref):
    @pl.when(pl.program_id(1) == 0)
    def _():
        x = x_ref[...]
        ms = jnp.mean(x * x, axis=-1, keepdims=True)
        h_ref[...] = (x * lax.rsqrt(ms + EPS) * g_ref[...]).astype(BF16)

    z_ref[...] = _dot(h_ref[...], w_ref[...]).astype(z_ref.dtype)


def in_proj(x2d, g, w_packed, tm=1024, tn=3072):
    t = x2d.shape[0]
    tm = min(tm, t)
    return pl.pallas_call(
        _inproj_kernel,
        out_shape=jax.ShapeDtypeStruct((t, Z_COLS), BF16),
        grid=(t // tm, Z_COLS // tn),
        in_specs=[
            pl.BlockSpec((tm, D_MODEL), lambda i, j: (i, 0)),
            pl.BlockSpec((1, D_MODEL), lambda i, j: (0, 0)),
            pl.BlockSpec((D_MODEL, tn), lambda i, j: (0, j)),
        ],
        out_specs=pl.BlockSpec((tm, tn), lambda i, j: (i, j)),
        scratch_shapes=[pltpu.VMEM((tm, D_MODEL), BF16)],
        compiler_params=_cparams("parallel", "arbitrary"),
        name="in_proj",
    )(x2d, g, w_packed)


def _gla_kernel(qf_ref, kf_ref, vf_ref, mf_ref, qb_ref, kb_ref, vb_ref, mb_ref, wdec_ref, bdec_ref,
                of_ref, ob_ref, s_ref, g_ref, qm_sc, km_sc, ki_sc, dec_sc, p_sc, u_sc, sb_sc, *, tl):
    c = GLA_CHUNK
    nc = tl // c

    @pl.when(pl.program_id(1) == 0)
    def _():
        s_ref[...] = jnp.zeros_like(s_ref)

    for d, m_ref in enumerate((mf_ref, mb_ref)):
        x = _dot(m_ref[...], wdec_ref[d]) + bdec_ref[d]
        logsig = jnp.minimum(x, 0.0) - jnp.log(1.0 + jnp.exp(-jnp.abs(x)))
        g_ref[d] = logsig * (1.0 / GLA_GATE_NORM)

    row = lax.broadcasted_iota(jnp.int32, (c, c), 0)
    col = lax.broadcasted_iota(jnp.int32, (c, c), 1)
    tri = (row >= col, row <= col)
    lane = lax.broadcasted_iota(jnp.int32, (c, LANES), 1)
    head_mask = (lane < GLA_DK, lane >= GLA_DK)
    dirs = ((qf_ref, kf_ref, vf_ref, of_ref), (qb_ref, kb_ref, vb_ref, ob_ref))

    ones_tri = [jnp.where(t, 1.0, 0.0).astype(BF16) for t in tri]
    work = [(d, cc) for cc in range(nc) for d in range(2)]
    rows = lambda cc: slice(cc * c, (cc + 1) * c)
    cols = lambda h: slice(h * GLA_DV, (h + 1) * GLA_DV)


    for d, cc in work:
        q_ref, k_ref, _, _ = dirs[d]
        g = g_ref[d, rows(cc), :]
        g_hi = g.astype(BF16)
        r1 = g - g_hi.astype(F32)
        g_mid = r1.astype(BF16)
        g_lo = (r1 - g_mid.astype(F32)).astype(BF16)
        b = _dot(ones_tri[d], g_hi) + _dot(ones_tri[d], g_mid) + _dot(ones_tri[d], g_lo)
        b_tot = b[c - 1:c, :] if d == 0 else b[0:1, :]
        qc = q_ref[rows(cc), :].astype(F32) * (GLA_DK ** -0.5)
        kc = k_ref[rows(cc), :].astype(F32)
        q_dec = qc * jnp.exp(b)
        k_inv = kc * jnp.exp(-b)
        k_end = kc * jnp.exp(b_tot - b)
        dec_sc[d, cc] = jnp.exp(b_tot)
        for p in range(GLA_HEADS // 2):
            sl = slice(p * LANES, (p + 1) * LANES)
            ki_sc[d, p, rows(cc), :] = k_inv[:, sl].astype(BF16)
            for hh in range(2):
                qm_sc[d, 2 * p + hh, rows(cc), :] = jnp.where(head_mask[hh], q_dec[:, sl], 0.0).astype(BF16)
                km_sc[d, 2 * p + hh, rows(cc), :] = jnp.where(head_mask[hh], k_end[:, sl], 0.0).astype(BF16)

    for d, cc in work:
        v_ref = dirs[d][2]
        for h in range(GLA_HEADS):
            scores = _dot_nt(qm_sc[d, h, rows(cc), :], ki_sc[d, h // 2, rows(cc), :])
            p_sc[d, h, rows(cc), :] = jnp.where(tri[d], scores, 0.0).astype(BF16)
            u_sc[d, h, cc] = _dot_tn(v_ref[rows(cc), cols(h)], km_sc[d, h, rows(cc), :])

    for h in range(GLA_HEADS):
        sl = slice((h // 2) * LANES, (h // 2 + 1) * LANES)
        for d in range(2):
            st = s_ref[d, h]
            for ci in range(nc):
                cc = ci if d == 0 else nc - 1 - ci
                sb_sc[d, h, cc] = st.T.astype(BF16)
                st = st * dec_sc[d, cc][:, sl] + u_sc[d, h, cc]
            s_ref[d, h] = st

    for d, cc in work:
        _, _, v_ref, o_ref = dirs[d]
        for h in range(GLA_HEADS):
            o_h = (_dot(p_sc[d, h, rows(cc), :], v_ref[rows(cc), cols(h)])
                   + _dot(qm_sc[d, h, rows(cc), :], sb_sc[d, h, cc]))
            o_ref[rows(cc), cols(h)] = o_h.astype(o_ref.dtype)


def gla_mixer(z, wdec, bdec, batch, seq, tl=1024):
    t = batch * seq
    tl = min(tl, seq)
    nb = seq // tl
    fwd = lambda b, i: b * nb + i
    bwd = lambda b, i: b * nb + (nb - 1 - i)

    def specs(rb):
        return [
            pl.BlockSpec((tl, 256), lambda b, i: (rb(b, i), COL_GQ // 256)),
            pl.BlockSpec((tl, 256), lambda b, i: (rb(b, i), COL_GK // 256)),
            pl.BlockSpec((tl, 512), lambda b, i: (rb(b, i), COL_GV // 512)),
            pl.BlockSpec((tl, LANES), lambda b, i: (rb(b, i), COL_MISC // LANES)),
        ]

    out_sd = jax.ShapeDtypeStruct((t, GLA_HEADS * GLA_DV), BF16)
    return pl.pallas_call(
        functools.partial(_gla_kernel, tl=tl),
        out_shape=(out_sd, out_sd),
        grid=(batch, nb),
        in_specs=specs(fwd) + specs(bwd) + [
            pl.BlockSpec((2, LANES, 256), lambda b, i: (0, 0, 0)),
            pl.BlockSpec((2, 1, 256), lambda b, i: (0, 0, 0)),
        ],
        out_specs=(
            pl.BlockSpec((tl, 512), lambda b, i: (fwd(b, i), 0)),
            pl.BlockSpec((tl, 512), lambda b, i: (bwd(b, i), 0)),
        ),
        scratch_shapes=[
            pltpu.VMEM((2, GLA_HEADS, GLA_DV, LANES), F32),
            pltpu.VMEM((2, tl, 256), F32),
            pltpu.VMEM((2, GLA_HEADS, tl, LANES), BF16),
            pltpu.VMEM((2, GLA_HEADS, tl, LANES), BF16),
            pltpu.VMEM((2, GLA_HEADS // 2, tl, LANES), BF16),
            pltpu.VMEM((2, tl // GLA_CHUNK, 1, 256), F32),
            pltpu.VMEM((2, GLA_HEADS, tl, GLA_CHUNK), BF16),
            pltpu.VMEM((2, GLA_HEADS, tl // GLA_CHUNK, GLA_DV, LANES), F32),
            pltpu.VMEM((2, GLA_HEADS, tl // GLA_CHUNK, GLA_DV, LANES), BF16),
        ],
        compiler_params=_cparams("parallel", "arbitrary"),
        name="gla_mixer",
    )(z, z, z, z, z, z, z, z, wdec, bdec)


HALO = 16
SCAN_TILE = 8


def _lru_kernel(xf_ref, xfp_ref, xfn_ref, xb_ref, xbp_ref, xbn_ref, cw_ref, cb_ref, wa_ref, ba_ref,
                wi_ref, bi_ref, lam_ref, hf_ref, hb_ref, h_sc, ac_sc, hc_sc, cin_sc, carry_sc, *, tl, nb):
    i = pl.program_id(1)
    st = SCAN_TILE
    nt = tl // st
    ng = LRU_WIDTH // LANES

    @pl.when(i == 0)
    def _():
        carry_sc[...] = jnp.zeros_like(carry_sc)

    blocks = ((xf_ref, xfp_ref, xfn_ref, i), (xb_ref, xbp_ref, xbn_ref, nb - 1 - i))

    tile_row = lax.broadcasted_iota(jnp.int32, (nt, LRU_WIDTH), 0)
    cw = cw_ref[...]
    conv_bias = cb_ref[...]

    for d, (x_ref, xp_ref, xn_ref, li) in enumerate(blocks):
        x = x_ref[...].astype(F32)
        for g in range(ng):
            h_sc[d, g] = x[:, g * LANES:(g + 1) * LANES]
        xr = [jnp.concatenate([h_sc[d, g, pl.ds(r, nt, stride=st), :] for g in range(ng)], axis=1)
              for r in range(st)]
        prev = jnp.where(li > 0, xp_ref[...].astype(F32), 0.0)
        nxt = jnp.where(li < nb - 1, xn_ref[...].astype(F32), 0.0)

        def from_prev_tile(a, first):
            return jnp.where(tile_row == 0, first, pltpu.roll(a, 1, 0))

        def from_next_tile(a, last):
            return jnp.where(tile_row == nt - 1, last, pltpu.roll(a, nt - 1, 0))

        neighbours = {-2: from_prev_tile(xr[st - 2], prev[HALO - 2:HALO - 1, :]),
                      -1: from_prev_tile(xr[st - 1], prev[HALO - 1:HALO, :]),
                      st: from_next_tile(xr[0], nxt[0:1, :])}
        at = lambda r: xr[r] if 0 <= r < st else neighbours[r]
        u = jnp.concatenate(
            [cw[0:1, :] * at(r - 2) + cw[1:2, :] * at(r - 1) + cw[2:3, :] * at(r) + cw[3:4, :] * at(r + 1) + conv_bias
             for r in range(st)], axis=0)
        ub = u.astype(BF16)
        r_tanh = jnp.tanh(_dot(ub, wa_ref[d]) + ba_ref[d])
        i_tanh = jnp.tanh(_dot(ub, wi_ref[d]) + bi_ref[d])
        lam = lam_ref[d]
        softplus_neg = jnp.maximum(-lam, 0.0) + jnp.log(1.0 + jnp.exp(-jnp.abs(lam)))
        half_rate = (-0.5 * LRU_C * LOG2_E) * softplus_neg
        a = jnp.exp2(r_tanh * half_rate + half_rate)
        y = 1.0 - a * a
        root = y * lax.rsqrt(jnp.maximum(y, 1e-30))
        hh = (root * u) * (0.5 * i_tanh + 0.5)

        order = range(st) if d == 0 else range(st - 1, -1, -1)
        a_run = h_run = None
        for r in order:
            a_r, x_r = a[r * nt:(r + 1) * nt, :], hh[r * nt:(r + 1) * nt, :]
            if a_run is None:
                a_run, h_run = a_r, x_r
            else:
                h_run = a_r * h_run + x_r
                a_run = a_r * a_run
            ac_sc[d, r] = a_run
            hc_sc[d, r] = h_run

    carry = [carry_sc[0], carry_sc[1]]
    ends = (st - 1, 0)
    for step in range(nt):
        for d in range(2):
            j = step if d == 0 else nt - 1 - step
            cin_sc[d, j:j + 1, :] = carry[d]
            carry[d] = ac_sc[d, ends[d], j:j + 1, :] * carry[d] + hc_sc[d, ends[d], j:j + 1, :]
    carry_sc[0] = carry[0]
    carry_sc[1] = carry[1]

    for d, o_ref in enumerate((hf_ref, hb_ref)):
        cin = cin_sc[d]
        for r in range(st):
            h_r = ac_sc[d, r] * cin + hc_sc[d, r]
            for g in range(ng):
                h_sc[d, g, pl.ds(r, nt, stride=st), :] = h_r[:, g * LANES:(g + 1) * LANES]
        for g in range(ng):
            o_ref[:, g * LANES:(g + 1) * LANES] = h_sc[d, g].astype(o_ref.dtype)


def lru_mixer(z, conv_w, conv_b, wa, ba, wi, bi, lam, batch, seq, tl=1024):
    t = batch * seq
    tl = min(tl, seq)
    nb = seq // tl
    hpb = tl // HALO
    last_halo = t // HALO - 1
    fwd = lambda b, i: b * nb + i
    bwd = lambda b, i: b * nb + (nb - 1 - i)
    colx = COL_RX // LRU_WIDTH

    def specs(rb):
        return [
            pl.BlockSpec((tl, LRU_WIDTH), lambda b, i: (rb(b, i), colx)),
            pl.BlockSpec((HALO, LRU_WIDTH), lambda b, i: (jnp.maximum(rb(b, i) * hpb - 1, 0), colx)),
            pl.BlockSpec((HALO, LRU_WIDTH), lambda b, i: (jnp.minimum((rb(b, i) + 1) * hpb, last_halo), colx)),
        ]

    full = lambda shape: pl.BlockSpec(shape, lambda b, i: (0,) * len(shape))
    out_sd = jax.ShapeDtypeStruct((t, LRU_WIDTH), BF16)
    return pl.pallas_call(
        functools.partial(_lru_kernel, tl=tl, nb=nb),
        out_shape=(out_sd, out_sd),
        grid=(batch, nb),
        in_specs=specs(fwd) + specs(bwd) + [
            full((4, LRU_WIDTH)), full((1, LRU_WIDTH)),
            full((2, LRU_WIDTH, LRU_WIDTH)), full((2, 1, LRU_WIDTH)),
            full((2, LRU_WIDTH, LRU_WIDTH)), full((2, 1, LRU_WIDTH)),
            full((2, 1, LRU_WIDTH)),
        ],
        out_specs=(
            pl.BlockSpec((tl, LRU_WIDTH), lambda b, i: (fwd(b, i), 0)),
            pl.BlockSpec((tl, LRU_WIDTH), lambda b, i: (bwd(b, i), 0)),
        ),
        scratch_shapes=[
            pltpu.VMEM((2, LRU_WIDTH // LANES, tl, LANES), F32),
            pltpu.VMEM((2, SCAN_TILE, tl // SCAN_TILE, LRU_WIDTH), F32),
            pltpu.VMEM((2, SCAN_TILE, tl // SCAN_TILE, LRU_WIDTH), F32),
            pltpu.VMEM((2, tl // SCAN_TILE, LRU_WIDTH), F32),
            pltpu.VMEM((2, 1, LRU_WIDTH), F32),
        ],
        compiler_params=_cparams("parallel", "arbitrary"),
        name="lru_mixer",
    )(z, z, z, z, z, z, conv_w, conv_b, wa, ba, wi, bi, lam)


def _rope(x, cos_t, sin_a, sin_b):
    return x * cos_t + pltpu.roll(x, LANES - MLA_ROPE // 2, 1) * sin_a + pltpu.roll(x, MLA_ROPE // 2, 1) * sin_b


def _mla_prep_kernel(qa_ref, kva_ref, misc_ref, cos_ref, sa_ref, sb_ref, qag_ref, wqb_ref, kvag_ref, wkvb_ref,
                     qng_ref, kng_ref, q_out, k_out, v_out):
    cos_t, sin_a, sin_b = cos_ref[...], sa_ref[...], sb_ref[...]
    scale = LOG2_E * MLA_QK ** -0.5

    qa = qa_ref[...].astype(F32)
    qa_n = qa * lax.rsqrt(jnp.mean(qa * qa, axis=-1, keepdims=True) + EPS) * qag_ref[...]
    q = _dot(qa_n.astype(BF16), wqb_ref[...])
    qng = qng_ref[...]
    for h in range(MLA_HEADS):
        qh = q[:, h * MLA_QK_PAD:(h + 1) * MLA_QK_PAD]
        rstd = lax.rsqrt(jnp.sum(qh * qh, axis=-1, keepdims=True) * (1.0 / MLA_QK) + EPS)
        qh = qh * rstd * qng
        q_out[:, h * MLA_QK_PAD:h * MLA_QK_PAD + LANES] = (qh[:, :LANES] * scale).astype(q_out.dtype)
        q_out[:, h * MLA_QK_PAD + LANES:(h + 1) * MLA_QK_PAD] = (
            _rope(qh[:, LANES:], cos_t, sin_a, sin_b) * scale).astype(q_out.dtype)

    kva = kva_ref[...].astype(F32)
    kva_n = kva * lax.rsqrt(jnp.mean(kva * kva, axis=-1, keepdims=True) + EPS) * kvag_ref[...]
    kv = _dot(kva_n.astype(BF16), wkvb_ref[...])
    lane = lax.broadcasted_iota(jnp.int32, misc_ref.shape, 1)
    kr = jnp.where(lane < MLA_ROPE, misc_ref[...].astype(F32), 0.0)
    kr_ss = jnp.sum(kr * kr, axis=-1, keepdims=True)
    kng = kng_ref[...]
    kr_rot = _rope(kr * kng[:, LANES:], cos_t, sin_a, sin_b)
    for h in range(MLA_HEADS):
        k_nope = kv[:, h * 256:h * 256 + LANES]
        rstd = lax.rsqrt((jnp.sum(k_nope * k_nope, axis=-1, keepdims=True) + kr_ss) * (1.0 / MLA_QK) + EPS)
        k_out[:, h * MLA_QK_PAD:h * MLA_QK_PAD + LANES] = (k_nope * rstd * kng[:, :LANES]).astype(k_out.dtype)
        k_out[:, h * MLA_QK_PAD + LANES:(h + 1) * MLA_QK_PAD] = (kr_rot * rstd).astype(k_out.dtype)
        v_out[:, h * MLA_V:(h + 1) * MLA_V] = kv[:, h * 256 + LANES:(h + 1) * 256].astype(v_out.dtype)


def mla_prep(z, cos_t, sin_a, sin_b, qag, wqb, kvag, wkvb, qng, kng, batch, seq, tm=1024):
    t = batch * seq
    tm = min(tm, seq)
    nb = seq // tm
    full = lambda shape: pl.BlockSpec(shape, lambda i: (0,) * len(shape))
    tab = pl.BlockSpec((tm, LANES), lambda i: (i % nb, 0))
    return pl.pallas_call(
        _mla_prep_kernel,
        out_shape=(
            jax.ShapeDtypeStruct((t, MLA_HEADS * MLA_QK_PAD), BF16),
            jax.ShapeDtypeStruct((t, MLA_HEADS * MLA_QK_PAD), BF16),
            jax.ShapeDtypeStruct((t, MLA_HEADS * MLA_V), BF16),
        ),
        grid=(t // tm,),
        in_specs=[
            pl.BlockSpec((tm, MLA_Q_RANK), lambda i: (i, COL_MQA // MLA_Q_RANK)),
            pl.BlockSpec((tm, MLA_KV_RANK), lambda i: (i, COL_MKVA // MLA_KV_RANK)),
            pl.BlockSpec((tm, LANES), lambda i: (i, COL_MISC // LANES)),
            tab, tab, tab,
            full((1, MLA_Q_RANK)), full((MLA_Q_RANK, MLA_HEADS * MLA_QK_PAD)),
            full((1, MLA_KV_RANK)), full((MLA_KV_RANK, MLA_HEADS * 256)),
            full((1, MLA_QK_PAD)), full((1, MLA_QK_PAD)),
        ],
        out_specs=(
            pl.BlockSpec((tm, MLA_HEADS * MLA_QK_PAD), lambda i: (i, 0)),
            pl.BlockSpec((tm, MLA_HEADS * MLA_QK_PAD), lambda i: (i, 0)),
            pl.BlockSpec((tm, MLA_HEADS * MLA_V), lambda i: (i, 0)),
        ),
        compiler_params=_cparams("parallel"),
        name="mla_prep",
    )(z, z, z, cos_t, sin_a, sin_b, qag, wqb, kvag, wkvb, qng, kng)


def _attn_kernel(q_ref, k_ref, v_ref, o_ref, vp_sc, m_sc, acc_sc, s_sc, *, tk):
    nk = k_ref.shape[0] // tk
    nh = ATTN_HEADS_PER_STEP
    qcols = lambda hh: slice(hh * MLA_QK_PAD, (hh + 1) * MLA_QK_PAD)

    @pl.when(pl.program_id(2) == 0)
    def _():
        for hh in range(nh):
            vp_sc[hh, :, :MLA_V] = v_ref[:, hh * MLA_V:(hh + 1) * MLA_V]
            vp_sc[hh, :, MLA_V:] = jnp.ones((vp_sc.shape[1], MLA_V), vp_sc.dtype)

    def scores(hh, j):
        return _dot_nt(q_ref[:, qcols(hh)], k_ref[j * tk:(j + 1) * tk, qcols(hh)])

    def consume(hh, slot, j):
        s = s_sc[hh, slot]
        m_cur = jnp.max(s, axis=-1, keepdims=True)
        if j == 0:
            m_new = jnp.broadcast_to(m_cur, m_sc.shape[1:])
        else:
            m_prev = m_sc[hh]
            m_new = jnp.maximum(m_prev, m_cur)
        p = jnp.exp2(s - jnp.concatenate([m_new] * (tk // LANES), axis=1))
        pv = _dot(p.astype(BF16), vp_sc[hh, j * tk:(j + 1) * tk, :])
        if j == 0:
            acc_sc[hh] = pv
        else:
            alpha = jnp.exp2(m_prev - m_new)
            acc_sc[hh] = jnp.concatenate([alpha, alpha], axis=1) * acc_sc[hh] + pv
        m_sc[hh] = m_new

    for hh in range(nh):
        s_sc[hh, 0] = scores(hh, 0)
    for j in range(nk):
        for hh in range(nh):
            if j + 1 < nk:
                s_sc[hh, (j + 1) % 2] = scores(hh, j + 1)
            consume(hh, j % 2, j)
    for hh in range(nh):
        acc = acc_sc[hh]
        o_ref[:, hh * MLA_V:(hh + 1) * MLA_V] = (acc[:, :MLA_V] / acc[:, MLA_V:]).astype(o_ref.dtype)


ATTN_HEADS_PER_STEP = 2


def attention(q, k, v, batch, seq, tq=1024, tk=512):
    t = batch * seq
    tq = min(tq, seq)
    tk = min(tk, seq // 2)
    nq = seq // tq
    nh = ATTN_HEADS_PER_STEP
    return pl.pallas_call(
        functools.partial(_attn_kernel, tk=tk),
        out_shape=jax.ShapeDtypeStruct((t, MLA_HEADS * MLA_V), BF16),
        grid=(batch, MLA_HEADS // nh, nq),
        in_specs=[
            pl.BlockSpec((tq, nh * MLA_QK_PAD), lambda b, h, i: (b * nq + i, h)),
            pl.BlockSpec((seq, nh * MLA_QK_PAD), lambda b, h, i: (b, h)),
            pl.BlockSpec((seq, nh * MLA_V), lambda b, h, i: (b, h)),
        ],
        out_specs=pl.BlockSpec((tq, nh * MLA_V), lambda b, h, i: (b * nq + i, h)),
        scratch_shapes=[
            pltpu.VMEM((nh, seq, 2 * MLA_V), BF16),
            pltpu.VMEM((nh, tq, LANES), F32),
            pltpu.VMEM((nh, tq, 2 * MLA_V), F32),
            pltpu.VMEM((nh, 2, tq, tk), F32),
        ],
        compiler_params=_cparams("parallel", "parallel", "arbitrary"),
        name="mla_attention",
    )(q, k, v)


def _split2(x):
    hi = x.astype(BF16)
    return hi, (x - hi.astype(F32)).astype(BF16)


def _merge_kernel(*refs, with_router):
    (x_ref, og_ref, rg_ref, gf_ref, gb_ref, lf_ref, lb_ref, am_ref,
     mixg_ref, wgate_ref, gng_ref, wga_ref, wlr_ref, wml_ref, wo_ref, fng_ref) = refs[:16]
    if with_router:
        wr_ref, br_ref, x_out, h_out, comb_out = refs[16:]
    else:
        x_out, h_out = refs[16:]

    o = gf_ref[...].astype(F32) + gb_ref[...].astype(F32)
    gng = gng_ref[...]
    parts = []
    for h in range(GLA_HEADS):
        oh = o[:, h * GLA_DV:(h + 1) * GLA_DV]
        rstd = lax.rsqrt(jnp.mean(oh * oh, axis=-1, keepdims=True) + EPS)
        parts.append(oh * rstd * gng[:, h * GLA_DV:(h + 1) * GLA_DV])
    og = og_ref[...].astype(F32)
    ya_in = jnp.concatenate(parts, axis=-1) * (og * _sigmoid(og))
    y_a = _dot(ya_in.astype(BF16), wga_ref[...])

    rg = rg_ref[...].astype(F32)
    gelu = 0.5 * rg * (1.0 + jnp.tanh(0.7978845608028654 * (rg + 0.044715 * rg * rg * rg)))
    yb_in = (lf_ref[...].astype(F32) + lb_ref[...].astype(F32)) * gelu
    y_b = _dot(yb_in.astype(BF16), wlr_ref[...])

    y_c = _dot(am_ref[...], wml_ref[...])

    d = D_MODEL
    x = x_ref[...]
    h_mix = (x * lax.rsqrt(jnp.mean(x * x, axis=-1, keepdims=True) + EPS) * mixg_ref[...]).astype(BF16)
    gated = [jnp.tanh(_dot(h_mix, wgate_ref[:, k * d:(k + 1) * d])) * y for k, y in enumerate((y_a, y_b, y_c))]
    merged = 0.5 * ((gated[0] + gated[1] + gated[2]) + (y_a + y_b + y_c))
    x_new = x + _dot(merged.astype(BF16), wo_ref[...])
    x_out[...] = x_new
    h = x_new * lax.rsqrt(jnp.mean(x_new * x_new, axis=-1, keepdims=True) + EPS) * fng_ref[...]
    h_out[...] = h.astype(h_out.dtype)

    if with_router:
        h_hi, h_mid = _split2(h)
        w_parts = jnp.concatenate(_split2(wr_ref[...]), axis=1)
        part_a = _dot(h_hi, w_parts)
        part_b = _dot(h_mid, w_parts)
        logits = ((part_a[:, :LANES] + part_a[:, LANES:]) + (part_b[:, :LANES] + part_b[:, LANES:])) + br_ref[...]
        lane = lax.broadcasted_iota(jnp.int32, logits.shape, 1)
        logits = jnp.where(lane < N_EXPERTS, logits, -jnp.inf)
        v1 = jnp.max(logits, axis=-1, keepdims=True)
        i1 = jnp.min(jnp.where(logits == v1, lane, LANES), axis=-1, keepdims=True)
        rest = jnp.where(lane == i1, -jnp.inf, logits)
        v2 = jnp.max(rest, axis=-1, keepdims=True)
        i2 = jnp.min(jnp.where(rest == v2, lane, LANES), axis=-1, keepdims=True)
        e2 = jnp.exp(v2 - v1)
        w1 = 1.0 / (1.0 + e2)
        comb_out[...] = (jnp.where(lane == i1, w1, 0.0) + jnp.where(lane == i2, e2 * w1, 0.0)
                         + jnp.where(lane == i1 + ROUTE_TOP1, 1.0, 0.0) + jnp.where(lane == i2 + ROUTE_TOP2, 1.0, 0.0))


def merge(x2d, z, gla_f, gla_b, lru_f, lru_b, attn, mixg, wgate, gng, wga, wlr, wml, wo, fng, router=None, tm=512):
    t = x2d.shape[0]
    tm = min(tm, t)
    full = lambda shape: pl.BlockSpec(shape, lambda i: (0,) * len(shape), pipeline_mode=pl.Buffered(1))
    row512 = pl.BlockSpec((tm, 512), lambda i: (i, 0))
    in_specs = [
        pl.BlockSpec((tm, D_MODEL), lambda i: (i, 0)),
        pl.BlockSpec((tm, 512), lambda i: (i, COL_GOG // 512)),
        pl.BlockSpec((tm, 512), lambda i: (i, COL_RG // 512)),
        row512, row512, row512, row512, row512,
        full((1, D_MODEL)), full((D_MODEL, 3 * D_MODEL)),
        full((1, 512)), full((512, D_MODEL)), full((512, D_MODEL)), full((512, D_MODEL)),
        full((D_MODEL, D_MODEL)), full((1, D_MODEL)),
    ]
    args = [x2d, z, z, gla_f, gla_b, lru_f, lru_b, attn, mixg, wgate, gng, wga, wlr, wml, wo, fng]
    out_shape = [jax.ShapeDtypeStruct((t, D_MODEL), F32), jax.ShapeDtypeStruct((t, D_MODEL), BF16)]
    out_specs = [pl.BlockSpec((tm, D_MODEL), lambda i: (i, 0)), pl.BlockSpec((tm, D_MODEL), lambda i: (i, 0))]
    if router is not None:
        in_specs += [full((D_MODEL, LANES)), full((1, LANES))]
        args += list(router)
        out_shape.append(jax.ShapeDtypeStruct((t, LANES), F32))
        out_specs.append(pl.BlockSpec((tm, LANES), lambda i: (i, 0)))
    return pl.pallas_call(
        functools.partial(_merge_kernel, with_router=router is not None),
        out_shape=tuple(out_shape),
        grid=(t // tm,),
        in_specs=in_specs,
        out_specs=tuple(out_specs),
        compiler_params=_cparams("parallel"),
        name="merge_router" if router is not None else "merge",
    )(*args)


FFN_SLICE = 512


def _col_slices(total, step):
    return [(lo, min(lo + step, total)) for lo in range(0, total, step)]


def _ffn_kernel(h_ref, x_ref, w1_ref, w3_ref, w2_ref, o_ref):
    h = h_ref[...]
    y = x_ref[...]
    for lo, hi in _col_slices(w1_ref.shape[1], FFN_SLICE):
        a = _dot(h, w1_ref[:, lo:hi])
        t = (a * _sigmoid(a) * _dot(h, w3_ref[:, lo:hi])).astype(BF16)
        y = y + _dot(t, w2_ref[lo:hi, :])
    o_ref[...] = y


def ffn(h, x2d, w1, w3, w2, tm=1024):
    t = x2d.shape[0]
    tm = min(tm, t)
    resident = lambda shape: pl.BlockSpec(shape, lambda i: (0, 0), pipeline_mode=pl.Buffered(1))
    return pl.pallas_call(
        _ffn_kernel,
        out_shape=jax.ShapeDtypeStruct((t, D_MODEL), F32),
        grid=(t // tm,),
        in_specs=[
            pl.BlockSpec((tm, D_MODEL), lambda i: (i, 0)),
            pl.BlockSpec((tm, D_MODEL), lambda i: (i, 0)),
            resident((D_MODEL, D_FF)), resident((D_MODEL, D_FF)), resident((D_FF, D_MODEL)),
        ],
        out_specs=pl.BlockSpec((tm, D_MODEL), lambda i: (i, 0)),
        compiler_params=_cparams("parallel"),
        name="ffn",
    )(h, x2d, w1, w3, w2)


MOE_CHUNK = 128
MOE_PERM_ROWS = 512


def _moe_kernel(h_ref, x_ref, route_ref, w13_ref, w2_ref, o_ref,
                xs_sc, ys_sc, ws_sc, pos_sc, meta_sc, *, tb, rows):
    e = pl.program_id(1)
    n_perm = rows // MOE_PERM_ROWS

    def one_hot_rows(c):
        r = (lax.broadcasted_iota(jnp.int32, (MOE_PERM_ROWS, tb), 0) + c * MOE_PERM_ROWS).astype(F32)
        eq_a = r == pos_sc[0:1, :]
        eq_b = r == pos_sc[1:2, :]
        return eq_a, eq_b

    @pl.when(e == 0)
    def _():
        route_t = route_ref[...].T
        comb_t = route_t[0:N_EXPERTS, :]
        top1_t = route_t[ROUTE_TOP1:ROUTE_TOP1 + N_EXPERTS, :]
        top2_t = route_t[ROUTE_TOP2:ROUTE_TOP2 + N_EXPERTS, :]
        sel_t = top1_t + top2_t
        count = jnp.sum(sel_t, axis=1, keepdims=True)
        n_chunk = jnp.floor((count + (MOE_CHUNK - 1)) * (1.0 / MOE_CHUNK))
        base = []
        run = jnp.zeros((1, 1), F32)
        for ex in range(N_EXPERTS):
            base.append(run)
            run = run + n_chunk[ex:ex + 1, :]
        base_chunk = jnp.concatenate(base, axis=0)
        meta_sc[...] = jnp.concatenate(
            [jnp.broadcast_to(base_chunk, (N_EXPERTS, LANES)), jnp.broadcast_to(n_chunk, (N_EXPERTS, LANES))], axis=0)
        t_row = lax.broadcasted_iota(jnp.int32, (tb, tb), 0)
        t_col = lax.broadcasted_iota(jnp.int32, (tb, tb), 1)
        earlier = jnp.where(t_row < t_col, 1.0, 0.0).astype(BF16)
        rank_t = _dot(sel_t.astype(BF16), earlier)
        pos_t = base_chunk * float(MOE_CHUNK) + rank_t
        pos_sc[0:1, :] = jnp.sum(top1_t * pos_t, axis=0, keepdims=True)
        pos_sc[1:2, :] = jnp.sum(top2_t * pos_t, axis=0, keepdims=True)
        w_a = jnp.sum(top1_t * comb_t, axis=0, keepdims=True)
        w_b = jnp.sum(top2_t * comb_t, axis=0, keepdims=True)
        h = h_ref[...]
        for c in range(n_perm):
            eq_a, eq_b = one_hot_rows(c)
            sl = slice(c * MOE_PERM_ROWS, (c + 1) * MOE_PERM_ROWS)
            perm = jnp.where(eq_a, 1.0, jnp.where(eq_b, 1.0, 0.0)).astype(BF16)
            xs_sc[sl, :] = _dot(perm, h).astype(xs_sc.dtype)
            w_rows = jnp.sum(jnp.where(eq_a, w_a, 0.0) + jnp.where(eq_b, w_b, 0.0), axis=1, keepdims=True)
            ws_sc[sl, :] = jnp.broadcast_to(w_rows, (MOE_PERM_ROWS, LANES))
        ys_sc[...] = jnp.zeros_like(ys_sc)

    meta = meta_sc[...]
    sub = lax.broadcasted_iota(jnp.int32, meta.shape, 0)
    base_e = jnp.sum(jnp.where(sub == e, meta, 0.0)[:, 0:1]).astype(jnp.int32)
    n_e = jnp.sum(jnp.where(sub == e + N_EXPERTS, meta, 0.0)[:, 0:1]).astype(jnp.int32)

    def expert_rows(row0, n_rows):
        sl = pl.ds(row0, n_rows)
        xs = xs_sc[sl, :]
        ab = _dot(xs, w13_ref[...])
        a, b = ab[:, :EXPERT_FF], ab[:, EXPERT_FF:]
        hidden = a * _sigmoid(a) * b * ws_sc[sl, 0:1]
        ys_sc[sl, :] = _dot(hidden.astype(BF16), w2_ref[...]).astype(ys_sc.dtype)

    def pair(c, carry):
        expert_rows(pl.multiple_of((base_e + 2 * c) * MOE_CHUNK, MOE_CHUNK), 2 * MOE_CHUNK)
        return carry

    lax.fori_loop(0, n_e // 2, pair, 0)

    @pl.when(n_e % 2 == 1)
    def _():
        expert_rows(pl.multiple_of((base_e + n_e - 1) * MOE_CHUNK, MOE_CHUNK), MOE_CHUNK)

    @pl.when(e == N_EXPERTS - 1)
    def _():
        o_ref[...] = x_ref[...]
        for c in range(n_perm):
            eq_a, eq_b = one_hot_rows(c)
            perm = jnp.where(eq_a, 1.0, jnp.where(eq_b, 1.0, 0.0)).astype(BF16)
            o_ref[...] += _dot_tn(perm, ys_sc[c * MOE_PERM_ROWS:(c + 1) * MOE_PERM_ROWS, :])


def moe(h, x2d, route, w13, w2, tb=1024):
    t = x2d.shape[0]
    tb = min(tb, t)
    rows = -(-(2 * tb + N_EXPERTS * (MOE_CHUNK - 1)) // MOE_PERM_ROWS) * MOE_PERM_ROWS
    once = pl.Buffered(1)
    return pl.pallas_call(
        functools.partial(_moe_kernel, tb=tb, rows=rows),
        out_shape=jax.ShapeDtypeStruct((t, D_MODEL), F32),
        grid=(t // tb, N_EXPERTS),
        in_specs=[
            pl.BlockSpec((tb, D_MODEL), lambda i, e: (i, 0), pipeline_mode=once),
            pl.BlockSpec((tb, D_MODEL), lambda i, e: (i, 0), pipeline_mode=once),
            pl.BlockSpec((tb, LANES), lambda i, e: (i, 0)),
            pl.BlockSpec((None, D_MODEL, 2 * EXPERT_FF), lambda i, e: (e, 0, 0)),
            pl.BlockSpec((None, EXPERT_FF, D_MODEL), lambda i, e: (e, 0, 0)),
        ],
        out_specs=pl.BlockSpec((tb, D_MODEL), lambda i, e: (i, 0)),
        scratch_shapes=[
            pltpu.VMEM((rows, D_MODEL), BF16),
            pltpu.VMEM((rows, D_MODEL), BF16),
            pltpu.VMEM((rows, LANES), F32),
            pltpu.VMEM((8, tb), F32),
            pltpu.VMEM((2 * N_EXPERTS, LANES), F32),
        ],
        compiler_params=pltpu.CompilerParams(dimension_semantics=("parallel", "arbitrary"),
                                             vmem_limit_bytes=MOE_VMEM_LIMIT),
        name="moe",
    )(h, x2d, route, w13, w2)


def _pack_layer(layer, p):
    w_in = p["w_in"][layer]
    sizes = (256, 256, 512, 512, 32, 512, 512, 256, 128, 64, 3072)
    offs = [0]
    for s in sizes:
        offs.append(offs[-1] + s)
    seg = lambda n: w_in[:, offs[n]:offs[n + 1]]
    g_q, g_k, g_v, g_og, g_dec, r_x, r_gate, m_qa, m_kva, m_kr, gates = (seg(n) for n in range(11))
    w_packed = jnp.concatenate(
        [g_v, g_og, r_x, r_gate, g_q, g_k, m_qa, m_kva, m_kr, g_dec, jnp.zeros((D_MODEL, 32), F32)],
        axis=1).astype(BF16)
    w_gates = (0.5 * gates).astype(BF16)

    wdec = jnp.zeros((2, LANES, 256), F32)
    for d in range(2):
        lo = MLA_ROPE + d * GLA_LOWRANK
        wdec = wdec.at[d, lo:lo + GLA_LOWRANK, :].set(p["gla_w_dec"][layer, d])
    bdec = p["gla_b_dec"][layer].reshape(2, 1, 256)

    def block_diag(w):
        eye = jnp.eye(LRU_BLOCKS, dtype=w.dtype)
        out = w[:, :, :, None, :] * eye[None, :, None, :, None]
        return out.reshape(2, LRU_WIDTH, LRU_WIDTH).astype(BF16)

    wqb = p["mla_w_qb"][layer].reshape(MLA_Q_RANK, MLA_HEADS, MLA_QK)
    wqb = jnp.pad(wqb, ((0, 0), (0, 0), (0, MLA_QK_PAD - MLA_QK))).reshape(MLA_Q_RANK, MLA_HEADS * MLA_QK_PAD)
    pad_qk = lambda g: jnp.pad(g, (0, MLA_QK_PAD - MLA_QK)).reshape(1, MLA_QK_PAD)
    return dict(
        w_in=w_packed, w_gates=w_gates,
        norm_mix_g=p["norm_mix_g"][layer].reshape(1, D_MODEL),
        wdec=wdec.astype(BF16), bdec=bdec,
        conv_w=p["lru_conv_w"][layer], conv_b=p["lru_conv_b"][layer].reshape(1, LRU_WIDTH),
        wa=block_diag(0.5 * p["lru_w_a"][layer]), ba=0.5 * p["lru_b_a"][layer].reshape(2, 1, LRU_WIDTH),
        wi=block_diag(0.5 * p["lru_w_i"][layer]), bi=0.5 * p["lru_b_i"][layer].reshape(2, 1, LRU_WIDTH),
        lam=p["lru_lambda"][layer].reshape(2, 1, LRU_WIDTH),
        qag=p["mla_qa_g"][layer].reshape(1, MLA_Q_RANK), wqb=wqb.astype(BF16),
        kvag=p["mla_kva_g"][layer].reshape(1, MLA_KV_RANK), wkvb=p["mla_w_kvb"][layer].astype(BF16),
        qng=pad_qk(p["mla_qn_g"][layer]), kng=pad_qk(p["mla_kn_g"][layer]),
        gng=p["gla_norm_g"][layer].reshape(1, GLA_HEADS * GLA_DV),
        wga=p["gla_w_out"][layer].astype(BF16), wlr=p["lru_w_out"][layer].astype(BF16),
        wml=p["mla_w_out"][layer].astype(BF16), wo=p["w_o"][layer].astype(BF16),
        fng=p["norm_ffn_g"][layer].reshape(1, D_MODEL),
    )


def _rope_tables(length):
    pos = jnp.arange(length, dtype=F32)
    inv = ROPE_THETA ** (-jnp.arange(0, MLA_ROPE, 2, dtype=F32) / MLA_ROPE)
    ang = pos[:, None] * inv[None, :]
    cos, sin = jnp.cos(ang), jnp.sin(ang)
    zero = jnp.zeros_like(cos)
    cos_t = jnp.concatenate([cos, cos, zero, zero], axis=1)
    sin_a = jnp.concatenate([-sin, zero, zero, zero], axis=1)
    sin_b = jnp.concatenate([zero, sin, zero, zero], axis=1)
    return cos_t, sin_a, sin_b


def _trunk(x, layers, ffn_params, moe_params):
    batch, seq, _ = x.shape
    x2d = x.reshape(batch * seq, D_MODEL)
    tables = _rope_tables(seq)
    for layer, lp in enumerate(layers):
        z = in_proj(x2d, lp["norm_mix_g"], lp["w_in"])
        gla_f, gla_b = gla_mixer(z, lp["wdec"], lp["bdec"], batch, seq)
        lru_f, lru_b = lru_mixer(z, lp["conv_w"], lp["conv_b"], lp["wa"], lp["ba"], lp["wi"], lp["bi"], lp["lam"],
                                 batch, seq)
        q, k, v = mla_prep(z, *tables, lp["qag"], lp["wqb"], lp["kvag"], lp["wkvb"], lp["qng"], lp["kng"], batch, seq)
        attn = attention(q, k, v, batch, seq)
        j = layer // 2
        branch = (gla_f, gla_b, lru_f, lru_b, attn, lp["norm_mix_g"], lp["w_gates"], lp["gng"], lp["wga"], lp["wlr"],
                  lp["wml"], lp["wo"], lp["fng"])
        if layer % 2 == 0:
            x2d, h = merge(x2d, z, *branch)
            w1, w3, w2 = ffn_params[j]
            x2d = ffn(h, x2d, w1, w3, w2)
        else:
            x2d, h, route = merge(x2d, z, *branch, router=moe_params[j][:2])
            x2d = moe(h, x2d, route, *moe_params[j][2:])
    return x2d.reshape(batch, seq, D_MODEL)


def _pack_ffn(ffn_w1, ffn_w3, ffn_w2):
    return [(ffn_w1[j].astype(BF16), ffn_w3[j].astype(BF16), ffn_w2[j].astype(BF16))
            for j in range(ffn_w1.shape[0])]


def _pack_moe(moe_w_router, moe_b_router, moe_w1, moe_w3, moe_w2):
    out = []
    for j in range(moe_w1.shape[0]):
        w_router = jnp.pad(moe_w_router[j], ((0, 0), (0, LANES - N_EXPERTS)))
        b_router = jnp.pad(moe_b_router[j], (0, LANES - N_EXPERTS)).reshape(1, LANES)
        w13 = jnp.concatenate([moe_w1[j].astype(BF16), moe_w3[j].astype(BF16)], axis=-1)
        out.append((w_router, b_router, w13, moe_w2[j].astype(BF16)))
    return out


def kernel(x_prompt, x_sample, norm_mix_g, w_in, gla_w_dec, gla_b_dec, gla_norm_g, gla_w_out, lru_conv_w, lru_conv_b, lru_w_a, lru_b_a, lru_w_i, lru_b_i, lru_lambda, lru_w_out, mla_qa_g, mla_w_qb, mla_kva_g, mla_w_kvb, mla_qn_g, mla_kn_g, mla_w_out, w_o, norm_ffn_g, ffn_w1, ffn_w3, ffn_w2, moe_w_router, moe_b_router, moe_w1, moe_w3, moe_w2):
    p = dict(norm_mix_g=norm_mix_g, w_in=w_in, gla_w_dec=gla_w_dec, gla_b_dec=gla_b_dec, gla_norm_g=gla_norm_g,
             gla_w_out=gla_w_out, lru_conv_w=lru_conv_w, lru_conv_b=lru_conv_b, lru_w_a=lru_w_a, lru_b_a=lru_b_a,
             lru_w_i=lru_w_i, lru_b_i=lru_b_i, lru_lambda=lru_lambda, lru_w_out=lru_w_out, mla_qa_g=mla_qa_g,
             mla_w_qb=mla_w_qb, mla_kva_g=mla_kva_g, mla_w_kvb=mla_w_kvb, mla_qn_g=mla_qn_g, mla_kn_g=mla_kn_g,
             mla_w_out=mla_w_out, w_o=w_o, norm_ffn_g=norm_ffn_g)
    depth = w_in.shape[0]
    layers = [_pack_layer(layer, p) for layer in range(depth)]
    ffn_params = _pack_ffn(ffn_w1, ffn_w3, ffn_w2)
    moe_params = _pack_moe(moe_w_router, moe_b_router, moe_w1, moe_w3, moe_w2)
    y_prompt = _trunk(x_prompt, layers, ffn_params, moe_params)
    y_sample = _trunk(x_sample, layers, ffn_params, moe_params)
    return (y_prompt, y_sample)
```
